```python
import math
import jax
import jax.numpy as jnp
from jax import lax
import numpy as np

D_MODEL = 2048
BATCH = 2
SEQ = 4096
DEPTH = 4
DEC_BATCH = 8
DEC_SEQ = 8
PAST_LEN = 16384
PAGE_SIZE = 128

W_GROUP = D_MODEL // 4
H_A = 4
DH_A = W_GROUP // (2 * H_A)
DV_A = 2 * DH_A
S5_CH = 16
G_B = W_GROUP // S5_CH
P_B = 64
H_C = 4
DH_C = W_GROUP // H_C
H_IDX = 8
D_IDX = 64
TOPK_MAX = 256
H_D = 4
DV_D = W_GROUP // H_D
DK_D = DV_D // 2
GATE_RANK = 16
GATE_TAU = 16.0
GLA_CHUNK = 16
D_FF = ((8 * D_MODEL // 3 + 255) // 256) * 256
REL_BUCKETS = 32
REL_MAX_EXACT = 16
REL_MAX_DIST = 128
DEEPNORM_ALPHA = (2 * DEPTH) ** 0.25
DEEPNORM_BETA = (8 * DEPTH) ** -0.25
LN_EPS = 1e-5
Q_BLOCK = 128
IN_SIZES = (H_A * 2 * DH_A, H_A * 2 * DH_A, H_A * DV_A,
            W_GROUP,
            H_C * DH_C, H_C * DH_C, H_C * DH_C, H_IDX * D_IDX, D_IDX, H_IDX,
            H_D * DK_D, H_D * DK_D, H_D * DV_D, W_GROUP, GATE_RANK)
D_IN = sum(IN_SIZES)

kernel_name = 'hybrid_parallel_heads_step'

F32 = jnp.float32


def layer_norm(x, g, b):
    xf = x.astype(F32)
    mu = xf.mean(-1, keepdims=True)
    var = jnp.square(xf - mu).mean(-1, keepdims=True)
    return ((xf - mu) * lax.rsqrt(var + LN_EPS) * g.astype(F32) + b.astype(F32)).astype(x.dtype)


def rms_norm(x, g):
    xf = x.astype(F32)
    return (xf * lax.rsqrt(jnp.mean(xf * xf, -1, keepdims=True) + LN_EPS) * g.astype(F32)).astype(x.dtype)


def rel_bucket(dist):
    n = jnp.maximum(dist, 0)
    nf = jnp.maximum(n, REL_MAX_EXACT).astype(F32)
    large = REL_MAX_EXACT + (jnp.log(nf / REL_MAX_EXACT) / math.log(REL_MAX_DIST / REL_MAX_EXACT)
                             * (REL_BUCKETS - REL_MAX_EXACT)).astype(jnp.int32)
    large = jnp.minimum(large, REL_BUCKETS - 1)
    return jnp.where(n < REL_MAX_EXACT, n, large)


def query_block(lq):
    return Q_BLOCK if lq % Q_BLOCK == 0 else lq


def diff_attention(q, k, v, q_pos, lam, lam_init, sub_g, rel_table):
    B, Lq = q.shape[0], q.shape[1]
    Lk = k.shape[1]
    qb = query_block(Lq)
    nb = Lq // qb
    k_pos = jnp.arange(Lk, dtype=jnp.int32)
    k1, k2 = k[..., :DH_A], k[..., DH_A:]
    scale = DH_A ** -0.5
    q_blocks = q.reshape(B, nb, qb, H_A, 2 * DH_A).transpose(1, 0, 2, 3, 4)
    p_blocks = q_pos.reshape(nb, qb)

    def block(args):
        qblk, pblk = args
        bias = rel_table[rel_bucket(pblk[:, None] - k_pos[None, :])].astype(F32)
        bias = bias.transpose(2, 0, 1)[None]
        causal = k_pos[None, :] <= pblk[:, None]

        def probs(qh, kh):
            s = jnp.einsum('bqhd,bkhd->bhqk', qh, kh).astype(F32) * scale + bias
            return jax.nn.softmax(jnp.where(causal, s, -jnp.inf), axis=-1)

        p = probs(qblk[..., :DH_A], k1) - lam * probs(qblk[..., DH_A:], k2)
        return jnp.einsum('bhqk,bkhd->bqhd', p.astype(v.dtype), v)

    o = lax.map(block, (q_blocks, p_blocks))
    o = o.transpose(1, 0, 2, 3, 4).reshape(B, Lq, H_A, DV_A)
    o = rms_norm(o, sub_g) * (1.0 - lam_init)
    return o.reshape(B, Lq, H_A * DV_A)


def dsa_attention(q, k, v, iq, ik, iw, q_pos, rel_table):
    B, Lq = q.shape[0], q.shape[1]
    Lk = k.shape[1]
    topk = min(TOPK_MAX, Lk // 4)
    qb = query_block(Lq)
    nb = Lq // qb
    k_pos = jnp.arange(Lk, dtype=jnp.int32)
    scale = DH_C ** -0.5
    q_blocks = q.reshape(B, nb, qb, H_C, DH_C).transpose(1, 0, 2, 3, 4)
    iq_blocks = iq.reshape(B, nb, qb, H_IDX, D_IDX).transpose(1, 0, 2, 3, 4)
    iw_blocks = iw.reshape(B, nb, qb, H_IDX).transpose(1, 0, 2, 3)
    p_blocks = q_pos.reshape(nb, qb)
    gather_rows = jax.vmap(lambda kb, ib: kb[ib])

    def block(args):
        qblk, iqblk, iwblk, pblk = args
        causal = k_pos[None, :] <= pblk[:, None]
        hs = jax.nn.relu(jnp.einsum('bqhd,bkd->bqhk', iqblk, ik).astype(F32))
        score = jnp.einsum('bqh,bqhk->bqk', iwblk.astype(F32), hs)
        score = jnp.where(causal[None], score, -jnp.inf)
        _, sel = lax.top_k(score, topk)
        k_sel = gather_rows(k, sel)
        v_sel = gather_rows(v, sel)
        valid = sel <= pblk[None, :, None]
        bias = rel_table[rel_bucket(pblk[None, :, None] - sel)].astype(F32).transpose(0, 1, 3, 2)
        s = jnp.einsum('bqhd,bqkhd->bqhk', qblk, k_sel).astype(F32) * scale + bias
        s = jnp.where(valid[:, :, None, :], s, -jnp.inf)
        p = jax.nn.softmax(s, axis=-1)
        return jnp.einsum('bqhk,bqkhd->bqhd', p.astype(v.dtype), v_sel)

    o = lax.map(block, (q_blocks, iq_blocks, iw_blocks, p_blocks))
    return o.transpose(1, 0, 2, 3, 4).reshape(B, Lq, H_C * DH_C)


def s5_mixer(u, h0, a_re, a_im, log_dt, b_re, b_im, c_re, c_im, d_skip, w_glu):
    B, L = u.shape[0], u.shape[1]
    ug = u.reshape(B, L, G_B, S5_CH).astype(F32)
    dt = jnp.exp(log_dt.astype(F32))[:, None]
    lam_re, lam_im = a_re.astype(F32), a_im.astype(F32)
    z_re, z_im = lam_re * dt, lam_im * dt
    mag = jnp.exp(z_re)
    e_re, e_im = mag * jnp.cos(z_im), mag * jnp.sin(z_im)
    den = lam_re * lam_re + lam_im * lam_im
    f_re = ((e_re - 1.0) * lam_re + e_im * lam_im) / den
    f_im = (e_im * lam_re - (e_re - 1.0) * lam_im) / den
    br, bi = b_re.astype(F32), b_im.astype(F32)
    bb_re = f_re[..., None] * br - f_im[..., None] * bi
    bb_im = f_re[..., None] * bi + f_im[..., None] * br
    bu_re = jnp.einsum('blgc,gpc->blgp', ug, bb_re)
    bu_im = jnp.einsum('blgc,gpc->blgp', ug, bb_im)
    ar = jnp.broadcast_to(e_re, bu_re.shape)
    ai = jnp.broadcast_to(e_im, bu_re.shape)

    def combine(c1, c2):
        a1r, a1i, b1r, b1i = c1
        a2r, a2i, b2r, b2i = c2
        return (a2r * a1r - a2i * a1i, a2r * a1i + a2i * a1r,
                a2r * b1r - a2i * b1i + b2r, a2r * b1i + a2i * b1r + b2i)

    _, _, h_re, h_im = lax.associative_scan(combine, (ar, ai, bu_re, bu_im), axis=1)
    if h0 is not None:
        t = jnp.arange(1, L + 1, dtype=F32)[:, None, None]
        pm = jnp.exp(z_re[None] * t)
        p_re, p_im = pm * jnp.cos(z_im[None] * t), pm * jnp.sin(z_im[None] * t)
        h0r, h0i = h0[0].astype(F32)[:, None], h0[1].astype(F32)[:, None]
        h_re = h_re + p_re * h0r - p_im * h0i
        h_im = h_im + p_re * h0i + p_im * h0r
    y = (jnp.einsum('blgp,gcp->blgc', h_re, c_re.astype(F32))
         - jnp.einsum('blgp,gcp->blgc', h_im, c_im.astype(F32)))
    y = y.reshape(B, L, W_GROUP) + d_skip.astype(F32) * u.astype(F32)
    y = jax.nn.gelu(y).astype(u.dtype)
    gu = jnp.einsum('bld,de->ble', y, w_glu)
    out = gu[..., :W_GROUP] * jax.nn.sigmoid(gu[..., W_GROUP:])
    return out, h_re[:, -1], h_im[:, -1]


def gla_mixer(q, k, v, g_low, r, S0, w_gate, b_gate, norm_g):
    B, L = q.shape[0], q.shape[1]
    logit = jnp.einsum('blr,re->ble', g_low, w_gate) + b_gate
    log_a = jax.nn.log_sigmoid(logit.astype(F32)).reshape(B, L, H_D, DK_D) / GATE_TAU
    C = math.gcd(L, GLA_CHUNK)
    n = L // C
    qf = (q.astype(F32) * DK_D ** -0.5).reshape(B, n, C, H_D, DK_D)
    kf = k.astype(F32).reshape(B, n, C, H_D, DK_D)
    vf = v.astype(F32).reshape(B, n, C, H_D, DV_D)
    bcum = jnp.cumsum(log_a.reshape(B, n, C, H_D, DK_D), axis=2)
    tri = jnp.tril(jnp.ones((C, C), dtype=bool))[None, None, :, :, None, None]
    diff = bcum[:, :, :, None] - bcum[:, :, None, :]
    decay = jnp.exp(jnp.where(tri, diff, -jnp.inf))
    att = jnp.einsum('bnthd,bnshd,bntshd->bnhts', qf, kf, decay)
    o_intra = jnp.einsum('bnhts,bnshv->bnthv', att, vf)
    b_last = bcum[:, :, -1]
    kv = jnp.einsum('bnshd,bnshv->bnhdv', kf * jnp.exp(b_last[:, :, None] - bcum), vf)

    def step(S, xs):
        dec, kv_c = xs
        return dec[..., None] * S + kv_c, S

    S_init = jnp.zeros((B, H_D, DK_D, DV_D), F32) if S0 is None else S0.astype(F32)
    S_fin, S_prev = lax.scan(step, S_init, (jnp.exp(b_last).transpose(1, 0, 2, 3),
                                            kv.transpose(1, 0, 2, 3, 4)))
    S_prev = S_prev.transpose(1, 0, 2, 3, 4)
    o_inter = jnp.einsum('bnthd,bnhdv->bnthv', qf * jnp.exp(bcum), S_prev)
    o = rms_norm((o_intra + o_inter).reshape(B, L, H_D, DV_D), norm_g)
    out = o.reshape(B, L, W_GROUP) * jax.nn.silu(r.astype(F32))
    return out.astype(q.dtype), S_fin


def hybrid_layer(x, l, offset, past, W):
    B, L, _ = x.shape
    q_pos = offset + jnp.arange(L, dtype=jnp.int32)
    proj = jnp.einsum('bld,de->ble', x, W['w_in'][l])
    split_points = np.cumsum(IN_SIZES)[:-1].tolist()
    (a_q, a_k, a_v, b_u, c_q, c_k, c_v, i_q, i_k, i_w,
     d_q, d_k, d_v, d_r, d_g) = jnp.split(proj, split_points, axis=-1)
    a_q = a_q.reshape(B, L, H_A, 2 * DH_A)
    a_k = a_k.reshape(B, L, H_A, 2 * DH_A)
    a_v = a_v.reshape(B, L, H_A, DV_A)
    c_q = c_q.reshape(B, L, H_C, DH_C)
    c_k = c_k.reshape(B, L, H_C, DH_C)
    c_v = c_v.reshape(B, L, H_C, DH_C)
    i_q = i_q.reshape(B, L, H_IDX, D_IDX)
    d_q = d_q.reshape(B, L, H_D, DK_D)
    d_k = d_k.reshape(B, L, H_D, DK_D)
    d_v = d_v.reshape(B, L, H_D, DV_D)
    if past is None:
        ak_all, av_all, ck_all, cv_all, ik_all = a_k, a_v, c_k, c_v, i_k
        h0, S0 = None, None
    else:
        ak_all = jnp.concatenate([past['a_k'].astype(x.dtype), a_k], axis=1)
        av_all = jnp.concatenate([past['a_v'].astype(x.dtype), a_v], axis=1)
        ck_all = jnp.concatenate([past['c_k'].astype(x.dtype), c_k], axis=1)
        cv_all = jnp.concatenate([past['c_v'].astype(x.dtype), c_v], axis=1)
        ik_all = jnp.concatenate([past['c_idx'].astype(x.dtype), i_k], axis=1)
        h0, S0 = (past['b_re'], past['b_im']), past['d']
    rel = W['rel_bias']
    lam_init = 0.8 - 0.6 * math.exp(-0.3 * l)
    lam = (jnp.exp(jnp.sum(W['lam_q1'][l].astype(F32) * W['lam_k1'][l].astype(F32)))
           - jnp.exp(jnp.sum(W['lam_q2'][l].astype(F32) * W['lam_k2'][l].astype(F32))) + lam_init)
    y_a = diff_attention(a_q, ak_all, av_all, q_pos, lam, lam_init, W['a_subln'][l], rel[:, :H_A])
    y_b, h_re, h_im = s5_mixer(b_u, h0, W['s5_a_re'][l], W['s5_a_im'][l], W['s5_log_dt'][l],
                               W['s5_b_re'][l], W['s5_b_im'][l], W['s5_c_re'][l], W['s5_c_im'][l],
                               W['s5_d'][l], W['s5_w_glu'][l])
    y_c = dsa_attention(c_q, ck_all, cv_all, i_q, ik_all, i_w, q_pos, rel[:, H_A:])
    y_d, S_fin = gla_mixer(d_q, d_k, d_v, d_g, d_r, S0, W['gla_w_gate'][l], W['gla_b_gate'][l], W['gla_norm'][l])
    mix = jnp.concatenate([y_a, y_b, y_c, y_d], axis=-1)
    x = layer_norm(DEEPNORM_ALPHA * x + jnp.einsum('ble,ed->bld', mix, W['w_out'][l]),
                   W['ln1_g'][l], W['ln1_b'][l])
    hid = jax.nn.silu(jnp.einsum('bld,df->blf', x, W['ffn_w_gate'][l])) * jnp.einsum('bld,df->blf', x, W['ffn_w_up'][l])
    x = layer_norm(DEEPNORM_ALPHA * x + jnp.einsum('blf,fd->bld', hid, W['ffn_w_down'][l]),
                   W['ln2_g'][l], W['ln2_b'][l])
    return x, (a_k, a_v, c_k, c_v, i_k, h_re, h_im, S_fin)


def run_trunk(x, offset, W, past_fn):
    news = []
    for l in range(DEPTH):
        x, new = hybrid_layer(x, l, offset, past_fn(l), W)
        news.append(new)
    stacked = [jnp.stack(z, axis=0) for z in zip(*news)]
    return x, stacked


def setup_inputs(seed: int = 0) -> dict:
    key = jax.random.key(seed)
    ks = iter(jax.random.split(key, 48))

    def nrm(shape, scale):
        return jax.random.normal(next(ks), shape, F32) * scale

    n_pages = PAST_LEN // PAGE_SIZE
    n_used = DEC_BATCH * n_pages
    n_pool = n_used + max(1, n_used // 4)
    x_prompt = nrm((BATCH, SEQ, D_MODEL), 1.0)
    x_sample = nrm((DEC_BATCH, DEC_SEQ, D_MODEL), 1.0)
    cache_a_k = nrm((DEPTH, n_pool, PAGE_SIZE, H_A, 2 * DH_A), 1.0)
    cache_a_v = nrm((DEPTH, n_pool, PAGE_SIZE, H_A, DV_A), 1.0)
    cache_c_k = nrm((DEPTH, n_pool, PAGE_SIZE, H_C, DH_C), 1.0)
    cache_c_v = nrm((DEPTH, n_pool, PAGE_SIZE, H_C, DH_C), 1.0)
    cache_c_idx = nrm((DEPTH, n_pool, PAGE_SIZE, D_IDX), 1.0)
    state_b_re = nrm((DEPTH, DEC_BATCH, G_B, P_B), 0.3)
    state_b_im = nrm((DEPTH, DEC_BATCH, G_B, P_B), 0.3)
    state_d = nrm((DEPTH, DEC_BATCH, H_D, DK_D, DV_D), 0.3)
    page_table = jax.random.permutation(next(ks), n_pool)[:n_used].reshape(DEC_BATCH, n_pages).astype(jnp.int32)
    rel_bias = nrm((REL_BUCKETS, H_A + H_C), 0.2)
    w_in = nrm((DEPTH, D_MODEL, D_IN), D_MODEL ** -0.5)
    w_out = nrm((DEPTH, D_MODEL, D_MODEL), D_MODEL ** -0.5 * DEEPNORM_BETA)
    lam_q1 = nrm((DEPTH, DH_A), 0.1)
    lam_k1 = nrm((DEPTH, DH_A), 0.1)
    lam_q2 = nrm((DEPTH, DH_A), 0.1)
    lam_k2 = nrm((DEPTH, DH_A), 0.1)
    a_subln = 1.0 + nrm((DEPTH, DV_A), 0.02)
    s5_a_re = -0.5 + nrm((DEPTH, G_B, P_B), 0.01)
    s5_a_im = math.pi * jnp.arange(P_B, dtype=F32) + nrm((DEPTH, G_B, P_B), 0.01)
    s5_log_dt = jax.random.uniform(next(ks), (DEPTH, G_B), F32, math.log(1e-3), math.log(1e-1))
    s5_b_re = nrm((DEPTH, G_B, P_B, S5_CH), (2 * S5_CH) ** -0.5)
    s5_b_im = nrm((DEPTH, G_B, P_B, S5_CH), (2 * S5_CH) ** -0.5)
    s5_c_re = nrm((DEPTH, G_B, S5_CH, P_B), P_B ** -0.5)
    s5_c_im = nrm((DEPTH, G_B, S5_CH, P_B), P_B ** -0.5)
    s5_d = nrm((DEPTH, W_GROUP), 1.0)
    s5_w_glu = nrm((DEPTH, W_GROUP, 2 * W_GROUP), W_GROUP ** -0.5)
    gla_w_gate = nrm((DEPTH, GATE_RANK, H_D * DK_D), GATE_RANK ** -0.5)
    gla_b_gate = nrm((DEPTH, H_D * DK_D), 0.1)
    gla_norm = 1.0 + nrm((DEPTH, DV_D), 0.02)
    ln1_g = 1.0 + nrm((DEPTH, D_MODEL), 0.02)
    ln1_b = nrm((DEPTH, D_MODEL), 0.02)
    ffn_w_gate = nrm((DEPTH, D_MODEL, D_FF), D_MODEL ** -0.5)
    ffn_w_up = nrm((DEPTH, D_MODEL, D_FF), D_MODEL ** -0.5)
    ffn_w_down = nrm((DEPTH, D_FF, D_MODEL), D_FF ** -0.5 * DEEPNORM_BETA)
    ln2_g = 1.0 + nrm((DEPTH, D_MODEL), 0.02)
    ln2_b = nrm((DEPTH, D_MODEL), 0.02)
    return {'x_prompt': x_prompt, 'x_sample': x_sample,
            'cache_a_k': cache_a_k, 'cache_a_v': cache_a_v,
            'cache_c_k': cache_c_k, 'cache_c_v': cache_c_v, 'cache_c_idx': cache_c_idx,
            'state_b_re': state_b_re, 'state_b_im': state_b_im, 'state_d': state_d,
            'page_table': page_table, 'rel_bias': rel_bias, 'w_in': w_in, 'w_out': w_out,
            'lam_q1': lam_q1, 'lam_k1': lam_k1, 'lam_q2': lam_q2, 'lam_k2': lam_k2, 'a_subln': a_subln,
            's5_a_re': s5_a_re, 's5_a_im': s5_a_im, 's5_log_dt': s5_log_dt,
            's5_b_re': s5_b_re, 's5_b_im': s5_b_im, 's5_c_re': s5_c_re, 's5_c_im': s5_c_im,
            's5_d': s5_d, 's5_w_glu': s5_w_glu,
            'gla_w_gate': gla_w_gate, 'gla_b_gate': gla_b_gate, 'gla_norm': gla_norm,
            'ln1_g': ln1_g, 'ln1_b': ln1_b,
            'ffn_w_gate': ffn_w_gate, 'ffn_w_up': ffn_w_up, 'ffn_w_down': ffn_w_down,
            'ln2_g': ln2_g, 'ln2_b': ln2_b}


def reference(x_prompt, x_sample, cache_a_k, cache_a_v, cache_c_k, cache_c_v, cache_c_idx,
              state_b_re, state_b_im, state_d, page_table, rel_bias, w_in, w_out,
              lam_q1, lam_k1, lam_q2, lam_k2, a_subln, s5_a_re, s5_a_im, s5_log_dt,
              s5_b_re, s5_b_im, s5_c_re, s5_c_im, s5_d, s5_w_glu, gla_w_gate, gla_b_gate, gla_norm,
              ln1_g, ln1_b, ffn_w_gate, ffn_w_up, ffn_w_down, ln2_g, ln2_b):
    W = {'rel_bias': rel_bias, 'w_in': w_in, 'w_out': w_out,
         'lam_q1': lam_q1, 'lam_k1': lam_k1, 'lam_q2': lam_q2, 'lam_k2': lam_k2, 'a_subln': a_subln,
         's5_a_re': s5_a_re, 's5_a_im': s5_a_im, 's5_log_dt': s5_log_dt,
         's5_b_re': s5_b_re, 's5_b_im': s5_b_im, 's5_c_re': s5_c_re, 's5_c_im': s5_c_im,
         's5_d': s5_d, 's5_w_glu': s5_w_glu,
         'gla_w_gate': gla_w_gate, 'gla_b_gate': gla_b_gate, 'gla_norm': gla_norm,
         'ln1_g': ln1_g, 'ln1_b': ln1_b, 'ffn_w_gate': ffn_w_gate, 'ffn_w_up': ffn_w_up,
         'ffn_w_down': ffn_w_down, 'ln2_g': ln2_g, 'ln2_b': ln2_b}

    def paged(cache, l):
        rows = cache[l][page_table]
        return rows.reshape((rows.shape[0], rows.shape[1] * rows.shape[2]) + rows.shape[3:])

    def sample_past(l):
        return {'a_k': paged(cache_a_k, l), 'a_v': paged(cache_a_v, l),
                'c_k': paged(cache_c_k, l), 'c_v': paged(cache_c_v, l), 'c_idx': paged(cache_c_idx, l),
                'b_re': state_b_re[l], 'b_im': state_b_im[l], 'd': state_d[l]}

    y_prompt, new_p = run_trunk(x_prompt, 0, W, lambda l: None)
    y_sample, new_s = run_trunk(x_sample, PAST_LEN, W, sample_past)
    p_a_k, p_a_v, p_c_k, p_c_v, p_c_idx, p_b_re, p_b_im, p_d = new_p
    s_a_k, s_a_v, s_c_k, s_c_v, s_c_idx, s_b_re, s_b_im, s_d = new_s
    return (y_prompt, y_sample,
            p_a_k, p_a_v, p_c_k, p_c_v, p_c_idx, p_b_re, p_b_im, p_d,
            s_a_k, s_a_v, s_c_k, s_c_v, s_c_idx, s_b_re, s_b_im, s_d)
```

```python
import functools
import math

import jax
import jax.numpy as jnp
from jax import lax
from jax.experimental import pallas as pl
from jax.experimental.pallas import tpu as pltpu

F32 = jnp.float32
BF16 = jnp.bfloat16

D_MODEL = 2048
DEPTH = 4
PAST_LEN = 16384
PAGE_SIZE = 128
W_GROUP = 512
H_A = 4
DH_A = 64
H_C = 4
DH_C = 128
H_IDX = 8
D_IDX = 64
TOPK_MAX = 256
G_B = 32
P_B = 64
S5_CH = 16
H_D = 4
DK_D = 64
DV_D = 128
GATE_RANK = 16
GATE_TAU = 16.0
GLA_CHUNK = 16
D_FF = 5632
REL_BUCKETS = 32
REL_MAX_EXACT = 16
REL_MAX_DIST = 128
DEEPNORM_ALPHA = (2 * DEPTH) ** 0.25
LN_EPS = 1e-5

N_MAIN = 4096
N_TAIL = 1664
TAIL_MISC = 1536
NSTATE = G_B * P_B

NEG_BIAS = -1e30
INT_MIN = -2 ** 31
KEY_NEG_INF = -2139095041

VMEM_LIMIT_BYTES = 56 * 1024 * 1024
LANES = 128


def _cparams(sem):
    return pltpu.CompilerParams(dimension_semantics=sem, vmem_limit_bytes=VMEM_LIMIT_BYTES)


def _dot(a, b):
    return jnp.dot(a, b, preferred_element_type=F32)


def _dot_nt(a, b):
    return lax.dot_general(a, b, (((1,), (1,)), ((), ())), preferred_element_type=F32)


def _mm_kernel(a_ref, w_ref, o_ref):
    o_ref[...] = _dot(a_ref[...].astype(BF16), w_ref[...].astype(BF16)).astype(o_ref.dtype)


def _proj_main(x, w_in, l, tm):
    M, K = x.shape
    tn = W_GROUP
    return pl.pallas_call(
        _mm_kernel,
        out_shape=jax.ShapeDtypeStruct((N_MAIN // tn, M, tn), F32),
        grid=(M // tm, N_MAIN // tn),
        in_specs=[pl.BlockSpec((tm, K), lambda i, j: (i, 0)),
                  pl.BlockSpec((None, K, tn), lambda i, j: (l, 0, j))],
        out_specs=pl.BlockSpec((None, tm, tn), lambda i, j: (j, i, 0)),
        compiler_params=_cparams(("parallel", "arbitrary")),
    )(x, w_in)


def _proj_tail(x, w_tail, l, tm):
    M, K = x.shape
    return pl.pallas_call(
        _mm_kernel,
        out_shape=jax.ShapeDtypeStruct((M, N_TAIL), F32),
        grid=(M // tm,),
        in_specs=[pl.BlockSpec((tm, K), lambda i: (i, 0)),
                  pl.BlockSpec((None, K, N_TAIL), lambda i: (l, 0, 0))],
        out_specs=pl.BlockSpec((tm, N_TAIL), lambda i: (i, 0)),
        compiler_params=_cparams(("parallel",)),
    )(x, w_tail)


def _mm_ln_kernel(*refs, n_a, n_k):
    a_refs = refs[:n_a]
    w_ref, x_ref, g_ref, b_ref, o32_ref, o16_ref, acc_ref = refs[n_a:]
    k = pl.program_id(1)

    @pl.when(k == 0)
    def _():
        acc_ref[...] = jnp.zeros_like(acc_ref)

    if n_a == 1:
        acc_ref[...] += _dot(a_refs[0][...].astype(BF16), w_ref[...].astype(BF16))
    else:
        for kk in range(n_a):
            @pl.when(k == kk)
            def _(a_ref=a_refs[kk]):
                acc_ref[...] += _dot(a_ref[...].astype(BF16), w_ref[...].astype(BF16))

    @pl.when(k == n_k - 1)
    def _():
        z = DEEPNORM_ALPHA * x_ref[...] + acc_ref[...]
        mu = jnp.mean(z, axis=1, keepdims=True)
        zc = z - mu
        var = jnp.mean(zc * zc, axis=1, keepdims=True)
        y = zc * lax.rsqrt(var + LN_EPS) * g_ref[...] + b_ref[...]
        o32_ref[...] = y
        o16_ref[...] = y.astype(BF16)


def _mm_ln(a_list, w, l, x, g, b, tm, tk):
    M, N = x.shape
    n_a = len(a_list)
    n_k = n_a if n_a > 1 else a_list[0].shape[1] // tk
    if n_a > 1:
        a_specs = [pl.BlockSpec((tm, tk), lambda i, k: (i, 0)) for _ in a_list]
    else:
        a_specs = [pl.BlockSpec((tm, tk), lambda i, k: (i, k))]
    row = pl.BlockSpec((None, 1, N), lambda i, k: (l, 0, 0))
    return pl.pallas_call(
        functools.partial(_mm_ln_kernel, n_a=n_a, n_k=n_k),
        out_shape=(jax.ShapeDtypeStruct((M, N), F32), jax.ShapeDtypeStruct((M, N), BF16)),
        grid=(M // tm, n_k),
        in_specs=a_specs + [pl.BlockSpec((None, tk, N), lambda i, k: (l, k, 0)),
                            pl.BlockSpec((tm, N), lambda i, k: (i, 0)), row, row],
        out_specs=(pl.BlockSpec((tm, N), lambda i, k: (i, 0)),
                   pl.BlockSpec((tm, N), lambda i, k: (i, 0))),
        scratch_shapes=[pltpu.VMEM((tm, N), F32)],
        compiler_params=_cparams(("parallel", "arbitrary")),
    )(*a_list, w, x, g, b)


def _swiglu_kernel(x_ref, wg_ref, wu_ref, o_ref):
    a = x_ref[...].astype(BF16)
    gate = _dot(a, wg_ref[...].astype(BF16))
    up = _dot(a, wu_ref[...].astype(BF16))
    o_ref[...] = (gate * jax.nn.sigmoid(gate) * up).astype(o_ref.dtype)


def _swiglu(x16, wg, wu, l, tm, tn):
    M, K = x16.shape
    return pl.pallas_call(
        _swiglu_kernel,
        out_shape=jax.ShapeDtypeStruct((M, D_FF), BF16),
        grid=(D_FF // tn, M // tm),
        in_specs=[pl.BlockSpec((tm, K), lambda j, i: (i, 0)),
                  pl.BlockSpec((None, K, tn), lambda j, i: (l, 0, j)),
                  pl.BlockSpec((None, K, tn), lambda j, i: (l, 0, j))],
        out_specs=pl.BlockSpec((tm, tn), lambda j, i: (i, j)),
        compiler_params=_cparams(("parallel", "arbitrary")),
    )(x16, wg, wu)


def _flash_kernel(*refs, groups, dk, scale, has_mask, diff):
    if has_mask:
        q_ref, k_ref, v_ref, bias_ref, mask_ref, lam_ref, gain_ref, o_ref, m_sc, l_sc, acc_sc = refs
    else:
        q_ref, k_ref, v_ref, bias_ref, lam_ref, gain_ref, o_ref, m_sc, l_sc, acc_sc = refs
    i = pl.program_id(2)
    j = pl.program_id(3)

    @pl.when(j == 0)
    def _():
        m_sc[...] = jnp.full(m_sc.shape, -jnp.inf, F32)
        l_sc[...] = jnp.zeros_like(l_sc)
        acc_sc[...] = jnp.zeros_like(acc_sc)

    @pl.when(j <= i)
    def _():
        q = q_ref[...].astype(BF16)
        k = k_ref[...].astype(BF16)
        v = v_ref[...].astype(BF16)
        bias = bias_ref[...]
        if has_mask:
            bias = bias + mask_ref[...].astype(F32)
        for g in range(groups):
            s = _dot_nt(q[:, g * dk:(g + 1) * dk], k[:, g * dk:(g + 1) * dk]) * scale + bias
            m_prev = m_sc[g]
            m_new = jnp.maximum(m_prev, jnp.max(s, axis=1, keepdims=True))
            p = jnp.exp(s - m_new)
            a = jnp.exp(m_prev - m_new)
            l_sc[g] = a * l_sc[g] + jnp.sum(p, axis=1, keepdims=True)
            acc_sc[g] = a * acc_sc[g] + _dot(p.astype(BF16), v)
            m_sc[g] = m_new

    @pl.when(j == i)
    def _():
        if diff:
            o = acc_sc[0] / l_sc[0] - lam_ref[...] * (acc_sc[1] / l_sc[1])
            ms = jnp.mean(o * o, axis=1, keepdims=True)
            o = o * lax.rsqrt(ms + LN_EPS) * gain_ref[...]
        else:
            o = acc_sc[0] / l_sc[0]
        o_ref[...] = o


def _flash_prompt(p8, slabs, bias_tab, mask, lam_row, gain_row, *, B, L, T, groups, dk, scale, diff):
    nq = L // T
    H = 4
    has_mask = mask is not None
    sq, sk, sv = slabs
    qspec = pl.BlockSpec((None, T, LANES), lambda b, h, i, j: (sq, b * nq + i, h))
    kspec = pl.BlockSpec((None, T, LANES), lambda b, h, i, j: (sk, b * nq + jnp.minimum(j, i), h))
    vspec = pl.BlockSpec((None, T, LANES), lambda b, h, i, j: (sv, b * nq + jnp.minimum(j, i), h))
    bspec = pl.BlockSpec((None, None, T, T),
                         lambda b, h, i, j: (h, jnp.minimum(jnp.maximum(i - j, 0), 2), 0, 0))
    rowspec = pl.BlockSpec((1, LANES), lambda b, h, i, j: (0, 0))
    in_specs = [qspec, kspec, vspec, bspec]
    args = [p8, p8, p8, bias_tab]
    if has_mask:
        in_specs.append(pl.BlockSpec((T, T), lambda b, h, i, j: (b * nq + i, jnp.minimum(j, i))))
        args.append(mask)
    in_specs += [rowspec, rowspec]
    args += [lam_row, gain_row]
    return pl.pallas_call(
        functools.partial(_flash_kernel, groups=groups, dk=dk, scale=scale, has_mask=has_mask, diff=diff),
        out_shape=jax.ShapeDtypeStruct((B * L, H * LANES), F32),
        grid=(B, H, nq, nq),
        in_specs=in_specs,
        out_specs=pl.BlockSpec((T, LANES), lambda b, h, i, j: (b * nq + i, h)),
        scratch_shapes=[pltpu.VMEM((groups, T, 1), F32), pltpu.VMEM((groups, T, 1), F32),
                        pltpu.VMEM((groups, T, LANES), F32)],
        compiler_params=_cparams(("parallel", "parallel", "parallel", "arbitrary")),
    )(*args)


def _sortable(x):
    i = lax.bitcast_convert_type(x, jnp.int32)
    return jnp.where(i < 0, i ^ jnp.int32(0x7FFFFFFF), i)


def _topk_bias(key_ref, o_ref, topk):
    R, W = key_ref.shape
    kf = float(topk)

    def count_ge(c):
        return jnp.sum(jnp.where(key_ref[...] >= c, 1.0, 0.0), axis=1, keepdims=True)

    base = jnp.where(count_ge(jnp.zeros((R, 1), jnp.int32)) >= kf, 0, INT_MIN).astype(jnp.int32)

    def bit_step(it, base):
        cand = base | jnp.left_shift(jnp.int32(1), 30 - it)
        return jnp.where(count_ge(cand) >= kf, cand, base)

    base = lax.fori_loop(0, 31, bit_step, base)
    n_ge = count_ge(base)
    has_excess = jnp.max(n_ge) > kf

    @pl.when(jnp.logical_not(has_excess))
    def _():
        key = key_ref[...]
        sel = (key >= base) & (key > KEY_NEG_INF)
        o_ref[...] = jnp.where(sel, 0.0, NEG_BIAS).astype(o_ref.dtype)

    @pl.when(has_excess)
    def _():
        n_gt = jnp.sum(jnp.where(key_ref[...] > base, 1.0, 0.0), axis=1, keepdims=True)
        need = kf - n_gt
        before = (lax.broadcasted_iota(jnp.int32, (LANES, LANES), 0)
                  < lax.broadcasted_iota(jnp.int32, (LANES, LANES), 1))
        before = jnp.where(before, 1.0, 0.0)

        def chunk(c, seen):
            sl = pl.ds(pl.multiple_of(c * LANES, LANES), LANES)
            kc = key_ref[:, sl]
            eq = kc == base
            eqf = jnp.where(eq, 1.0, 0.0)
            rank = _dot(eqf, before) + seen
            sel = ((kc > base) | (eq & (rank < need))) & (kc > KEY_NEG_INF)
            o_ref[:, sl] = jnp.where(sel, 0.0, NEG_BIAS).astype(o_ref.dtype)
            return seen + jnp.sum(eqf, axis=1, keepdims=True)

        lax.fori_loop(0, W // LANES, chunk, jnp.zeros((R, 1), F32))


def _idx_prompt_kernel(iq_ref, ik_ref, iw_ref, o_ref, key_sc, *, tq, L, kc, topk):
    i = pl.program_id(1)
    iq = iq_ref[...]
    lhs = jnp.concatenate([iq[:, h * D_IDX:(h + 1) * D_IDX] for h in range(H_IDX)], axis=0).astype(BF16)
    iw = iw_ref[:, D_IDX:D_IDX + H_IDX]
    qpos = i * tq + lax.broadcasted_iota(jnp.int32, (tq, 1), 0)
    for c in range(L // kc):
        ik = ik_ref[c * kc:(c + 1) * kc, 0:D_IDX].astype(BF16)
        s = _dot_nt(lhs, ik)
        score = jnp.zeros((tq, kc), F32)
        for h in range(H_IDX):
            score = score + jnp.maximum(s[h * tq:(h + 1) * tq], 0.0) * iw[:, h:h + 1]
        kpos = c * kc + lax.broadcasted_iota(jnp.int32, (1, kc), 1)
        score = jnp.where(kpos <= qpos, score, -jnp.inf)
        key_sc[:, c * kc:(c + 1) * kc] = _sortable(score)
    _topk_bias(key_sc, o_ref, topk)


def _idx_prompt(p8, tail, *, B, L, tq, topk):
    nq = L // tq
    nmisc = TAIL_MISC // LANES
    return pl.pallas_call(
        functools.partial(_idx_prompt_kernel, tq=tq, L=L, kc=min(L, 512), topk=topk),
        out_shape=jax.ShapeDtypeStruct((B * L, L), BF16),
        grid=(B, nq),
        in_specs=[pl.BlockSpec((None, tq, W_GROUP), lambda b, i: (7, b * nq + i, 0)),
                  pl.BlockSpec((L, LANES), lambda b, i: (b, nmisc)),
                  pl.BlockSpec((tq, LANES), lambda b, i: (b * nq + i, nmisc))],
        out_specs=pl.BlockSpec((tq, L), lambda b, i: (b * nq + i, 0)),
        scratch_shapes=[pltpu.VMEM((tq, L), jnp.int32)],
        compiler_params=_cparams(("parallel", "arbitrary")),
    )(p8, tail, tail)


def _gelu_tanh(x):
    return 0.5 * x * (1.0 + jnp.tanh(math.sqrt(2.0 / math.pi) * (x + 0.044715 * (x * x * x))))


def _s5_kernel(u_ref, h0r_ref, h0i_ref, bre_ref, bim_ref, apr_ref, api_ref, pwr_ref, pwi_ref,
               cre_ref, cim_ref, d_ref, wglu_ref, y_ref, hr_ref, hi_ref, cr_sc, ci_sc, *, T, nsteps):
    t = pl.program_id(1)

    @pl.when(t == 0)
    def _():
        cr_sc[...] = h0r_ref[...]
        ci_sc[...] = h0i_ref[...]

    u = u_ref[...]
    ub = u.astype(BF16)
    hr = _dot(ub, bre_ref[...])
    hi = _dot(ub, bim_ref[...])
    row = lax.broadcasted_iota(jnp.int32, (T, 1), 0)
    for j in range(nsteps):
        d = 1 << j
        sr = jnp.where(row >= d, pltpu.roll(hr, d, 0), 0.0)
        si = jnp.where(row >= d, pltpu.roll(hi, d, 0), 0.0)
        ar = apr_ref[j:j + 1, :]
        ai = api_ref[j:j + 1, :]
        hr, hi = hr + (ar * sr - ai * si), hi + (ar * si + ai * sr)
    cr = cr_sc[...]
    ci = ci_sc[...]
    pr = pwr_ref[...]
    pi_ = pwi_ref[...]
    hr, hi = hr + (pr * cr - pi_ * ci), hi + (pr * ci + pi_ * cr)
    cr_sc[...] = hr[T - 1:T, :]
    ci_sc[...] = hi[T - 1:T, :]
    hr_ref[...] = hr[T - 1:T, :]
    hi_ref[...] = hi[T - 1:T, :]
    y = _dot(hr.astype(BF16), cre_ref[...]) - _dot(hi.astype(BF16), cim_ref[...]) + d_ref[...] * u
    y = _gelu_tanh(y)
    gu = _dot(y.astype(BF16), wglu_ref[...])
    y_ref[...] = gu[:, :W_GROUP] * jax.nn.sigmoid(gu[:, W_GROUP:])


def _s5(p8, h0r, h0i, tabs, l, *, B, L, T):
    nt = L // T
    nsteps = T.bit_length() - 1
    bre, bim, apr, api, pwr, pwi, cre, cim, dsk, wglu = tabs
    lay = lambda *shape: pl.BlockSpec((None,) + shape, lambda b, t: (l,) + (0,) * len(shape))
    st = pl.BlockSpec((None, 1, NSTATE), lambda b, t: (b, 0, 0))
    return pl.pallas_call(
        functools.partial(_s5_kernel, T=T, nsteps=nsteps),
        out_shape=(jax.ShapeDtypeStruct((B * L, W_GROUP), F32),
                   jax.ShapeDtypeStruct((B, 1, NSTATE), F32), jax.ShapeDtypeStruct((B, 1, NSTATE), F32)),
        grid=(B, nt),
        in_specs=[pl.BlockSpec((None, T, W_GROUP), lambda b, t: (3, b * nt + t, 0)), st, st,
                  lay(W_GROUP, NSTATE), lay(W_GROUP, NSTATE), lay(8, NSTATE), lay(8, NSTATE),
                  lay(T, NSTATE), lay(T, NSTATE), lay(NSTATE, W_GROUP), lay(NSTATE, W_GROUP),
                  lay(1, W_GROUP), lay(W_GROUP, 2 * W_GROUP)],
        out_specs=(pl.BlockSpec((T, W_GROUP), lambda b, t: (b * nt + t, 0)), st, st),
        scratch_shapes=[pltpu.VMEM((1, NSTATE), F32), pltpu.VMEM((1, NSTATE), F32)],
        compiler_params=_cparams(("parallel", "arbitrary")),
    )(p8, h0r, h0i, bre, bim, apr, api, pwr, pwi, cre, cim, dsk, wglu)


def _log_sigmoid(x):
    return jnp.minimum(x, 0.0) - jnp.log1p(jnp.exp(-jnp.abs(x)))


def _gla_kernel(q_ref, k_ref, v_ref, r_ref, misc_ref, wg_ref, bg_ref, ng_ref, s0_ref,
                y_ref, sfin_ref, s_sc, *, TB, CH, nt, valid_len):
    t = pl.program_id(1)
    HK = H_D * DK_D
    HV = H_D * DV_D

    @pl.when(t == 0)
    def _():
        s_sc[...] = jnp.zeros_like(s_sc)
        for h in range(H_D):
            s_sc[h * DK_D:(h + 1) * DK_D, h * DV_D:(h + 1) * DV_D] = s0_ref[h]

    row = lax.broadcasted_iota(jnp.int32, (TB, 1), 0)
    glow = misc_ref[:, D_IDX + H_IDX:D_IDX + H_IDX + GATE_RANK]
    logit = _dot(glow.astype(BF16), wg_ref[...].astype(BF16)) + bg_ref[...]
    la = _log_sigmoid(logit) * (1.0 / GATE_TAU)
    k = k_ref[...]
    if valid_len < TB:
        la = jnp.where(row < valid_len, la, 0.0)
        k = jnp.where(row < valid_len, k, 0.0)
    ri = lax.broadcasted_iota(jnp.int32, (TB, TB), 0)
    ci = lax.broadcasted_iota(jnp.int32, (TB, TB), 1)
    same = (ri // CH) == (ci // CH)
    lower = same & (ci <= ri)
    tri = jnp.where(lower, 1.0, 0.0).astype(BF16)
    blk = jnp.where(same, 1.0, 0.0).astype(BF16)
    la_hi = la.astype(BF16)
    la_lo = (la - la_hi.astype(F32)).astype(BF16)
    bc = _dot(tri, la_hi) + _dot(tri, la_lo)
    bl = _dot(blk, la_hi) + _dot(blk, la_lo)
    q = q_ref[...] * (DK_D ** -0.5)
    v = v_ref[...]
    vb = v.astype(BF16)
    qt = (q * jnp.exp(bc)).astype(BF16)
    kc = (k * jnp.exp(-bc)).astype(BF16)
    kh_t = jnp.transpose(k * jnp.exp(bl - bc))
    dec_t = jnp.transpose(jnp.exp(bl))

    o_parts = []
    for h in range(H_D):
        att = _dot_nt(qt[:, h * DK_D:(h + 1) * DK_D], kc[:, h * DK_D:(h + 1) * DK_D])
        att = jnp.where(lower, att, 0.0)
        o_parts.append(_dot(att.astype(BF16), vb[:, h * DV_D:(h + 1) * DV_D]))
    o_intra = jnp.concatenate(o_parts, axis=1)

    head_blk = (lax.broadcasted_iota(jnp.int32, (HK, HV), 0) // DK_D
                == lax.broadcasted_iota(jnp.int32, (HK, HV), 1) // DV_D)
    col = lax.broadcasted_iota(jnp.int32, (1, TB), 1)
    S = s_sc[...]
    o_rows = []
    for c in range(TB // CH):
        o_rows.append(_dot(qt[c * CH:(c + 1) * CH, :], S.astype(BF16)))
        in_chunk = (col >= c * CH) & (col < (c + 1) * CH)
        kv = _dot(jnp.where(in_chunk, kh_t, 0.0).astype(BF16), vb)
        S = dec_t[:, c * CH:c * CH + 1] * S + jnp.where(head_blk, kv, 0.0)
    s_sc[...] = S
    o = o_intra + jnp.concatenate(o_rows, axis=0)

    outs = []
    for h in range(H_D):
        oh = o[:, h * DV_D:(h + 1) * DV_D]
        ms = jnp.mean(oh * oh, axis=1, keepdims=True)
        outs.append(oh * lax.rsqrt(ms + LN_EPS) * ng_ref[...])
    r = r_ref[...]
    y_ref[...] = jnp.concatenate(outs, axis=1) * (r * jax.nn.sigmoid(r))

    @pl.when(t == nt - 1)
    def _():
        for h in range(H_D):
            sfin_ref[h] = S[h * DK_D:(h + 1) * DK_D, h * DV_D:(h + 1) * DV_D]


def _gla(tail, s0, w_gate, b_gate, norm_g, l, *, B, L, TB, valid_len):
    nt = L // TB
    lay = lambda *shape: pl.BlockSpec((None,) + shape, lambda b, t: (l,) + (0,) * len(shape))
    sspec = pl.BlockSpec((None, H_D, DK_D, DV_D), lambda b, t: (b, 0, 0, 0))
    return pl.pallas_call(
        functools.partial(_gla_kernel, TB=TB, CH=GLA_CHUNK, nt=nt, valid_len=valid_len),
        out_shape=(jax.ShapeDtypeStruct((B * L, W_GROUP), F32),
                   jax.ShapeDtypeStruct((B, H_D, DK_D, DV_D), F32)),
        grid=(B, nt),
        in_specs=[pl.BlockSpec((TB, 256), lambda b, t: (b * nt + t, 0)),
                  pl.BlockSpec((TB, 256), lambda b, t: (b * nt + t, 1)),
                  pl.BlockSpec((TB, 512), lambda b, t: (b * nt + t, 1)),
                  pl.BlockSpec((TB, 512), lambda b, t: (b * nt + t, 2)),
                  pl.BlockSpec((TB, LANES), lambda b, t: (b * nt + t, TAIL_MISC // LANES)),
                  lay(GATE_RANK, H_D * DK_D), lay(1, H_D * DK_D), lay(1, DV_D), sspec],
        out_specs=(pl.BlockSpec((TB, W_GROUP), lambda b, t: (b * nt + t, 0)), sspec),
        scratch_shapes=[pltpu.VMEM((H_D * DK_D, H_D * DV_D), F32)],
        compiler_params=_cparams(("parallel", "arbitrary")),
    )(tail, tail, tail, tail, tail, w_gate, b_gate, norm_g, s0)


def _sidx_kernel(pt_ref, iq_ref, iw_ref, *rest, PPS, NS, n_new, topk):
    page_refs = rest[:PPS]
    new_ref, o_ref, key_sc = rest[PPS:]
    s_id = pl.program_id(1)
    iq = iq_ref[...].astype(BF16)
    iw = iw_ref[...]

    def scores(keys):
        hs = jnp.maximum(_dot_nt(iq, keys.astype(BF16)), 0.0) * iw
        sc = jnp.zeros((n_new, LANES), F32)
        for h in range(H_IDX):
            sc = sc + hs[h * n_new:(h + 1) * n_new]
        return sc

    for p in range(PPS):
        off = pl.multiple_of((s_id * PPS + p) * LANES, LANES)
        key_sc[:, pl.ds(off, LANES)] = _sortable(scores(page_refs[p][...]))

    @pl.when(s_id == NS - 1)
    def _():
        sc = scores(new_ref[...])
        qi = lax.broadcasted_iota(jnp.int32, (n_new, LANES), 0)
        kj = lax.broadcasted_iota(jnp.int32, (n_new, LANES), 1)
        sc = jnp.where(kj <= qi, sc, -jnp.inf)
        key_sc[:, NS * PPS * LANES:(NS * PPS + 1) * LANES] = _sortable(sc)
        _topk_bias(key_sc, o_ref, topk)


def _sidx(page_table, iq_rows, iw_tile, cache_idx, idx_new, l, *, PPS, topk):
    Bs, n_pages = page_table.shape
    NS = n_pages // PPS
    n_new = iq_rows.shape[1] // H_IDX
    W = (n_pages + 1) * LANES

    def page_spec(p):
        return pl.BlockSpec((None, None, PAGE_SIZE, D_IDX),
                            lambda b, s, pt: (l, pt[b, s * PPS + p], 0, 0))

    grid_spec = pltpu.PrefetchScalarGridSpec(
        num_scalar_prefetch=1,
        grid=(Bs, NS),
        in_specs=[pl.BlockSpec((None, H_IDX * n_new, D_IDX), lambda b, s, pt: (b, 0, 0)),
                  pl.BlockSpec((None, H_IDX * n_new, LANES), lambda b, s, pt: (b, 0, 0))]
                 + [page_spec(p) for p in range(PPS)]
                 + [pl.BlockSpec((None, PAGE_SIZE, D_IDX), lambda b, s, pt: (b, 0, 0))],
        out_specs=pl.BlockSpec((None, n_new, W), lambda b, s, pt: (b, 0, 0)),
        scratch_shapes=[pltpu.VMEM((n_new, W), jnp.int32)],
    )
    return pl.pallas_call(
        functools.partial(_sidx_kernel, PPS=PPS, NS=NS, n_new=n_new, topk=topk),
        out_shape=jax.ShapeDtypeStruct((Bs, n_new, W), F32),
        grid_spec=grid_spec,
        compiler_params=_cparams(("parallel", "arbitrary")),
    )(page_table, iq_rows, iw_tile, *([cache_idx] * PPS), idx_new)


def _sattn_kernel(pt_ref, qbd_ref, *rest, PPS, NS, R, n_new, scale, has_mask, diff):
    k_refs = rest[:PPS]
    v_refs = rest[PPS:2 * PPS]
    rest = rest[2 * PPS:]
    knew_ref, vnew_ref, far_ref, near_ref = rest[:4]
    rest = rest[4:]
    if has_mask:
        mask_ref, masknew_ref = rest[:2]
        rest = rest[2:]
    lam_ref, gain_ref, o_ref, m_sc, l_sc, acc_sc = rest
    s_id = pl.program_id(1)

    @pl.when(s_id == 0)
    def _():
        m_sc[...] = jnp.full(m_sc.shape, -jnp.inf, F32)
        l_sc[...] = jnp.zeros_like(l_sc)
        acc_sc[...] = jnp.zeros_like(acc_sc)

    qbd = qbd_ref[...].astype(BF16)

    def tile_rows(m):
        return jnp.concatenate([m] * (R // n_new), axis=0)

    def page(kp, vp, bias):
        s = _dot_nt(qbd, kp.astype(BF16)) * scale + bias
        m_prev = m_sc[...]
        m_new = jnp.maximum(m_prev, jnp.max(s, axis=1, keepdims=True))
        p = jnp.exp(s - m_new)
        a = jnp.exp(m_prev - m_new)
        l_sc[...] = a * l_sc[...] + jnp.sum(p, axis=1, keepdims=True)
        acc_sc[...] = a * acc_sc[...] + _dot(p.astype(BF16), vp.astype(BF16))
        m_sc[...] = m_new

    far = far_ref[...]
    for p in range(PPS):
        bias = far
        if p == PPS - 1:
            bias = jnp.where(s_id == NS - 1, near_ref[:, 0:LANES], far)
        if has_mask:
            bias = bias + tile_rows(mask_ref[:, p * LANES:(p + 1) * LANES])
        page(k_refs[p][...], v_refs[p][...], bias)

    @pl.when(s_id == NS - 1)
    def _():
        bias = near_ref[:, LANES:2 * LANES]
        if has_mask:
            bias = bias + tile_rows(masknew_ref[...])
        page(knew_ref[...], vnew_ref[...], bias)
        o_all = acc_sc[...] / l_sc[...]
        outs = []
        for h in range(4):
            cols = slice(h * LANES, (h + 1) * LANES)
            if diff:
                o1 = o_all[(2 * h) * n_new:(2 * h + 1) * n_new, cols]
                o2 = o_all[(2 * h + 1) * n_new:(2 * h + 2) * n_new, cols]
                o = o1 - lam_ref[...] * o2
                ms = jnp.mean(o * o, axis=1, keepdims=True)
                outs.append(o * lax.rsqrt(ms + LN_EPS) * gain_ref[...])
            else:
                outs.append(o_all[h * n_new:(h + 1) * n_new, cols])
        o_ref[...] = jnp.concatenate(outs, axis=1)


def _sattn(page_table, qbd, cache_k, cache_v, knew, vnew, far, near, mask, lam_row, gain_row, l,
           *, PPS, scale, diff):
    Bs, n_pages = page_table.shape
    NS = n_pages // PPS
    R = qbd.shape[1]
    n_new = 8
    has_mask = mask is not None

    def page_spec(p):
        return pl.BlockSpec((None, None, PAGE_SIZE, W_GROUP),
                            lambda b, s, pt: (l, pt[b, s * PPS + p], 0, 0))

    per_b = lambda *shape: pl.BlockSpec((None,) + shape, lambda b, s, pt: (b,) + (0,) * len(shape))
    const = lambda *shape: pl.BlockSpec(shape, lambda b, s, pt: (0,) * len(shape))
    in_specs = ([per_b(R, W_GROUP)] + [page_spec(p) for p in range(PPS)] * 2
                + [per_b(PAGE_SIZE, W_GROUP), per_b(PAGE_SIZE, W_GROUP), const(R, LANES), const(R, 2 * LANES)])
    args = [qbd] + [cache_k] * PPS + [cache_v] * PPS + [knew, vnew, far, near]
    if has_mask:
        in_specs += [pl.BlockSpec((None, n_new, PPS * LANES), lambda b, s, pt: (b, 0, s)),
                     pl.BlockSpec((None, n_new, LANES), lambda b, s, pt: (b, 0, n_pages))]
        args += [mask, mask]
    in_specs += [const(1, LANES), const(1, LANES)]
    args += [lam_row, gain_row]
    grid_spec = pltpu.PrefetchScalarGridSpec(
        num_scalar_prefetch=1,
        grid=(Bs, NS),
        in_specs=in_specs,
        out_specs=pl.BlockSpec((None, n_new, W_GROUP), lambda b, s, pt: (b, 0, 0)),
        scratch_shapes=[pltpu.VMEM((R, 1), F32), pltpu.VMEM((R, 1), F32), pltpu.VMEM((R, W_GROUP), F32)],
    )
    return pl.pallas_call(
        functools.partial(_sattn_kernel, PPS=PPS, NS=NS, R=R, n_new=n_new, scale=scale,
                          has_mask=has_mask, diff=diff),
        out_shape=jax.ShapeDtypeStruct((Bs, n_new, W_GROUP), F32),
        grid_spec=grid_spec,
        compiler_params=_cparams(("parallel", "arbitrary")),
    )(page_table, *args)


def _rel_bucket(dist):
    n = jnp.maximum(dist, 0)
    nf = jnp.maximum(n, REL_MAX_EXACT).astype(F32)
    large = REL_MAX_EXACT + (jnp.log(nf / REL_MAX_EXACT) / math.log(REL_MAX_DIST / REL_MAX_EXACT)
                             * (REL_BUCKETS - REL_MAX_EXACT)).astype(jnp.int32)
    large = jnp.minimum(large, REL_BUCKETS - 1)
    return jnp.where(n < REL_MAX_EXACT, n, large)


def _prompt_bias_tables(rel, T):
    assert T >= REL_MAX_DIST
    r = jnp.arange(T, dtype=jnp.int32)[:, None]
    c = jnp.arange(T, dtype=jnp.int32)[None, :]
    d0 = r - c
    t0 = jnp.where((d0 >= 0)[..., None], rel[_rel_bucket(d0)], NEG_BIAS)
    t1 = rel[_rel_bucket(T + r - c)]
    t2 = jnp.broadcast_to(rel[_rel_bucket(jnp.int32(2 * T))], t1.shape)
    return jnp.stack([t0, t1, t2], axis=0).transpose(3, 0, 1, 2).astype(F32)


def _sample_bias_tables(rel, n_new, reps):
    t = jnp.arange(n_new, dtype=jnp.int32)[:, None]
    c = jnp.arange(PAGE_SIZE, dtype=jnp.int32)[None, :]
    last = rel[_rel_bucket(PAGE_SIZE + t - c)]
    dn = t - c
    new = jnp.where(((dn >= 0) & (c < n_new))[..., None], rel[_rel_bucket(dn)], NEG_BIAS)
    farv = jnp.broadcast_to(rel[_rel_bucket(jnp.int32(2 * PAGE_SIZE))], last.shape)

    def rows(x):
        x = x.transpose(2, 0, 1)[:, None]
        x = jnp.broadcast_to(x, (x.shape[0], reps) + x.shape[2:])
        return x.reshape(-1, PAGE_SIZE).astype(F32)

    return rows(farv), jnp.concatenate([rows(last), rows(new)], axis=1)


def _s5_tables(a_re, a_im, log_dt, b_re, b_im, c_re, c_im, d_skip, w_glu, T):
    dt = jnp.exp(log_dt.astype(F32))[..., None]
    lam_re, lam_im = a_re.astype(F32), a_im.astype(F32)
    z_re, z_im = lam_re * dt, lam_im * dt
    mag = jnp.exp(z_re)
    e_re, e_im = mag * jnp.cos(z_im), mag * jnp.sin(z_im)
    den = lam_re * lam_re + lam_im * lam_im
    f_re = ((e_re - 1.0) * lam_re + e_im * lam_im) / den
    f_im = (e_im * lam_re - (e_re - 1.0) * lam_im) / den
    br, bi = b_re.astype(F32), b_im.astype(F32)
    bb_re = f_re[..., None] * br - f_im[..., None] * bi
    bb_im = f_re[..., None] * bi + f_im[..., None] * br
    eye = jnp.eye(G_B, dtype=F32)
    bd_in = lambda bb: jnp.einsum('lgpc,gh->lgchp', bb, eye).reshape(DEPTH, W_GROUP, NSTATE).astype(BF16)
    bd_out = lambda cc: jnp.einsum('lgcp,gh->lgphc', cc.astype(F32), eye).reshape(DEPTH, NSTATE, W_GROUP).astype(BF16)

    def powers(ts):
        tt = ts[None, :, None, None]
        pm = jnp.exp(z_re[:, None] * tt)
        return ((pm * jnp.cos(z_im[:, None] * tt)).reshape(DEPTH, -1, NSTATE),
                (pm * jnp.sin(z_im[:, None] * tt)).reshape(DEPTH, -1, NSTATE))

    apr, api = powers(jnp.asarray([2.0 ** j for j in range(8)], F32))
    pwr, pwi = powers(jnp.arange(1, T + 1, dtype=F32))
    return (bd_in(bb_re), bd_in(bb_im), apr, api, pwr, pwi, bd_out(c_re), bd_out(c_im),
            d_skip.astype(F32).reshape(DEPTH, 1, W_GROUP), w_glu.astype(BF16))


def _tail_weights(w_in):
    pad = jnp.zeros(w_in.shape[:2] + (N_TAIL - 1624,), w_in.dtype)
    return jnp.concatenate([w_in[..., 4168:5704], w_in[..., 4096:4168], w_in[..., 5704:5720], pad],
                           axis=-1).astype(BF16)


def _channel_mix(ys, x, l, W, tm):
    x1, x1b = _mm_ln(ys, W['w_out'], l, x, W['ln1_g'], W['ln1_b'], tm, W_GROUP)
    hid = _swiglu(x1b, W['ffn_w_gate'], W['ffn_w_up'], l, tm, 512)
    return _mm_ln([hid], W['ffn_w_down'], l, x1, W['ln2_g'], W['ln2_b'], tm, 512)


def _prompt_layer(x, xb, l, W, B, L):
    T_ATT = 512
    p8 = _proj_main(xb, W['w_in'], l, 512)
    tail = _proj_tail(xb, W['w_tail'], l, 512)
    y_a = _flash_prompt(p8, (0, 1, 2), W['tab_a'], None, W['lam_row'][l], W['gain_a'][l],
                        B=B, L=L, T=T_ATT, groups=2, dk=DH_A, scale=DH_A ** -0.5, diff=True)
    zeros_h = jnp.zeros((B, 1, NSTATE), F32)
    y_b, h_re, h_im = _s5(p8, zeros_h, zeros_h, W['s5_prompt'], l, B=B, L=L, T=W['s5_T'])
    mask = _idx_prompt(p8, tail, B=B, L=L, tq=128, topk=min(TOPK_MAX, L // 4))
    y_c = _flash_prompt(p8, (4, 5, 6), W['tab_c'], mask, W['lam_row'][l], W['gain_a'][l],
                        B=B, L=L, T=T_ATT, groups=1, dk=DH_C, scale=DH_C ** -0.5, diff=False)
    y_d, s_fin = _gla(tail, jnp.zeros((B, H_D, DK_D, DV_D), F32), W['gla_w_gate'], W['gla_b_gate'],
                      W['gla_norm'], l, B=B, L=L, TB=128, valid_len=128)
    x, xb = _channel_mix([y_a, y_b, y_c, y_d], x, l, W, 512)
    news = (p8[1].reshape(B, L, H_A, 128), p8[2].reshape(B, L, H_A, 128),
            p8[5].reshape(B, L, H_C, 128), p8[6].reshape(B, L, H_C, 128),
            tail[:, TAIL_MISC:TAIL_MISC + D_IDX].reshape(B, L, D_IDX),
            h_re.reshape(B, G_B, P_B), h_im.reshape(B, G_B, P_B), s_fin)
    return x, xb, news


def _sample_layer(x, xb, l, W, C, Bs, Ls):
    M = Bs * Ls
    PPS = 8
    p8 = _proj_main(xb, W['w_in'], l, M)
    tail = _proj_tail(xb, W['w_tail'], l, M)
    pt = C['page_table']

    def pad_new(z):
        z = z.reshape(Bs, Ls, z.shape[-1])
        return jnp.pad(z, ((0, 0), (0, PAGE_SIZE - Ls), (0, 0)))

    qa = p8[0].reshape(Bs, Ls, H_A * 2, DH_A)
    qbd_a = jnp.einsum('bqgd,gh->bgqhd', qa, jnp.eye(H_A * 2, dtype=F32)).reshape(Bs, H_A * 2 * Ls, W_GROUP)
    y_a = _sattn(pt, qbd_a, C['a_k'], C['a_v'], pad_new(p8[1]), pad_new(p8[2]), W['far_a'], W['near_a'],
                 None, W['lam_row'][l], W['gain_a'][l], l, PPS=PPS, scale=DH_A ** -0.5, diff=True)
    y_b, h_re, h_im = _s5(p8, C['b_re'][l].reshape(Bs, 1, NSTATE), C['b_im'][l].reshape(Bs, 1, NSTATE),
                          W['s5_sample'], l, B=Bs, L=Ls, T=Ls)
    misc = tail[:, TAIL_MISC:]
    iq_rows = p8[7].reshape(Bs, Ls, H_IDX, D_IDX).transpose(0, 2, 1, 3).reshape(Bs, H_IDX * Ls, D_IDX)
    iw = misc[:, D_IDX:D_IDX + H_IDX].reshape(Bs, Ls, H_IDX).transpose(0, 2, 1).reshape(Bs, H_IDX * Ls, 1)
    iw_tile = jnp.broadcast_to(iw, (Bs, H_IDX * Ls, LANES))
    mask = _sidx(pt, iq_rows, iw_tile, C['c_idx'], pad_new(misc[:, :D_IDX]), l, PPS=PPS,
                 topk=min(TOPK_MAX, (PAST_LEN + Ls) // 4))
    qc = p8[4].reshape(Bs, Ls, H_C, DH_C)
    qbd_c = jnp.einsum('bqgd,gh->bgqhd', qc, jnp.eye(H_C, dtype=F32)).reshape(Bs, H_C * Ls, W_GROUP)
    y_c = _sattn(pt, qbd_c, C['c_k'], C['c_v'], pad_new(p8[5]), pad_new(p8[6]), W['far_c'], W['near_c'],
                 mask, W['lam_row'][l], W['gain_a'][l], l, PPS=PPS, scale=DH_C ** -0.5, diff=False)
    tail_pad = pad_new(tail).reshape(Bs * PAGE_SIZE, N_TAIL)
    y_d, s_fin = _gla(tail_pad, C['d'][l], W['gla_w_gate'], W['gla_b_gate'], W['gla_norm'], l,
                      B=Bs, L=PAGE_SIZE, TB=PAGE_SIZE, valid_len=Ls)
    y_d = y_d.reshape(Bs, PAGE_SIZE, W_GROUP)[:, :Ls].reshape(M, W_GROUP)
    x, xb = _channel_mix([y_a.reshape(M, W_GROUP), y_b, y_c.reshape(M, W_GROUP), y_d], x, l, W, M)
    news = (p8[1].reshape(Bs, Ls, H_A, 128), p8[2].reshape(Bs, Ls, H_A, 128),
            p8[5].reshape(Bs, Ls, H_C, 128), p8[6].reshape(Bs, Ls, H_C, 128),
            misc[:, :D_IDX].reshape(Bs, Ls, D_IDX),
            h_re.reshape(Bs, G_B, P_B), h_im.reshape(Bs, G_B, P_B), s_fin)
    return x, xb, news


def kernel(x_prompt, x_sample, cache_a_k, cache_a_v, cache_c_k, cache_c_v, cache_c_idx, state_b_re, state_b_im, state_d, page_table, rel_bias, w_in, w_out, lam_q1, lam_k1, lam_q2, lam_k2, a_subln, s5_a_re, s5_a_im, s5_log_dt, s5_b_re, s5_b_im, s5_c_re, s5_c_im, s5_d, s5_w_glu, gla_w_gate, gla_b_gate, gla_norm, ln1_g, ln1_b, ffn_w_gate, ffn_w_up, ffn_w_down, ln2_g, ln2_b):
    B, L, _ = x_prompt.shape
    Bs, Ls, _ = x_sample.shape
    n_pool = cache_a_k.shape[1]
    S5_T = 256

    lam_init = jnp.asarray([0.8 - 0.6 * math.exp(-0.3 * l) for l in range(DEPTH)], F32)
    lam = (jnp.exp(jnp.sum(lam_q1.astype(F32) * lam_k1.astype(F32), axis=-1))
           - jnp.exp(jnp.sum(lam_q2.astype(F32) * lam_k2.astype(F32), axis=-1)) + lam_init)
    s5_args = (s5_a_re, s5_a_im, s5_log_dt, s5_b_re, s5_b_im, s5_c_re, s5_c_im, s5_d, s5_w_glu)
    rel = rel_bias.astype(F32)
    far_a, near_a = _sample_bias_tables(rel[:, :H_A], Ls, 2)
    far_c, near_c = _sample_bias_tables(rel[:, H_A:], Ls, 1)
    W = {
        'w_in': w_in, 'w_tail': _tail_weights(w_in), 'w_out': w_out,
        'ffn_w_gate': ffn_w_gate, 'ffn_w_up': ffn_w_up, 'ffn_w_down': ffn_w_down,
        'ln1_g': ln1_g.reshape(DEPTH, 1, D_MODEL), 'ln1_b': ln1_b.reshape(DEPTH, 1, D_MODEL),
        'ln2_g': ln2_g.reshape(DEPTH, 1, D_MODEL), 'ln2_b': ln2_b.reshape(DEPTH, 1, D_MODEL),
        'lam_row': jnp.broadcast_to(lam[:, None, None], (DEPTH, 1, LANES)),
        'gain_a': (a_subln.astype(F32) * (1.0 - lam_init)[:, None]).reshape(DEPTH, 1, LANES),
        'tab_a': _prompt_bias_tables(rel[:, :H_A], 512), 'tab_c': _prompt_bias_tables(rel[:, H_A:], 512),
        'far_a': far_a, 'near_a': near_a, 'far_c': far_c, 'near_c': near_c,
        's5_prompt': _s5_tables(*s5_args, S5_T), 's5_T': S5_T, 's5_sample': _s5_tables(*s5_args, Ls),
        'gla_w_gate': gla_w_gate, 'gla_b_gate': gla_b_gate.reshape(DEPTH, 1, H_D * DK_D),
        'gla_norm': gla_norm.reshape(DEPTH, 1, DV_D),
    }
    C = {
        'page_table': page_table,
        'a_k': cache_a_k.reshape(DEPTH, n_pool, PAGE_SIZE, W_GROUP),
        'a_v': cache_a_v.reshape(DEPTH, n_pool, PAGE_SIZE, W_GROUP),
        'c_k': cache_c_k.reshape(DEPTH, n_pool, PAGE_SIZE, W_GROUP),
        'c_v': cache_c_v.reshape(DEPTH, n_pool, PAGE_SIZE, W_GROUP),
        'c_idx': cache_c_idx, 'b_re': state_b_re, 'b_im': state_b_im, 'd': state_d,
    }

    xp = x_prompt.reshape(B * L, D_MODEL)
    xs = x_sample.reshape(Bs * Ls, D_MODEL)
    xpb, xsb = xp, xs
    news_p, news_s = [], []
    for l in range(DEPTH):
        xp, xpb, n_p = _prompt_layer(xp, xpb, l, W, B, L)
        xs, xsb, n_s = _sample_layer(xs, xsb, l, W, C, Bs, Ls)
        news_p.append(n_p)
        news_s.append(n_s)
    stack = lambda news: [jnp.stack(z, axis=0) for z in zip(*news)]
    return (xp.reshape(B, L, D_MODEL), xs.reshape(Bs, Ls, D_MODEL), *stack(news_p), *stack(news_s))
```

```python
import functools
import math

import jax
import jax.numpy as jnp
from jax import lax
from jax.experimental import pallas as pl
from jax.experimental.pallas import tpu as pltpu

F32 = jnp.float32
BF16 = jnp.bfloat16

D_MODEL = 2048
DEPTH = 4
PAST_LEN = 16384
PAGE_SIZE = 128
W_GROUP = 512
H_A = 4
DH_A = 64
H_C = 4
DH_C = 128
H_IDX = 8
D_IDX = 64
TOPK_MAX = 256
G_B = 32
P_B = 64
S5_CH = 16
H_D = 4
DK_D = 64
DV_D = 128
GATE_RANK = 16
GATE_TAU = 16.0
GLA_CHUNK = 16
D_FF = 5632
REL_BUCKETS = 32
REL_MAX_EXACT = 16
REL_MAX_DIST = 128
DEEPNORM_ALPHA = (2 * DEPTH) ** 0.25
LN_EPS = 1e-5

N_MAIN = 4096
N_TAIL = 1664
TAIL_MISC = 1536
NSTATE = G_B * P_B

NEG_BIAS = -1e30
INT_MIN = -2 ** 31
KEY_NEG_INF = -2139095041

VMEM_LIMIT_BYTES = 56 * 1024 * 1024
LANES = 128
LOG2E = 1.4426950408889634
FLASH_ROW_BLOCK = 128


def _cparams(sem):
    return pltpu.CompilerParams(dimension_semantics=sem, vmem_limit_bytes=VMEM_LIMIT_BYTES)


def _dot(a, b):
    return jnp.dot(a, b, preferred_element_type=F32)


def _dot_nt(a, b):
    return lax.dot_general(a, b, (((1,), (1,)), ((), ())), preferred_element_type=F32)


def _mm_kernel(a_ref, w_ref, o_ref):
    o_ref[...] = _dot(a_ref[...].astype(BF16), w_ref[...].astype(BF16)).astype(o_ref.dtype)


def _proj_main(x, w_in, l, tm):
    M, K = x.shape
    tn = W_GROUP
    return pl.pallas_call(
        _mm_kernel,
        out_shape=jax.ShapeDtypeStruct((N_MAIN // tn, M, tn), F32),
        grid=(M // tm, N_MAIN // tn),
        in_specs=[pl.BlockSpec((tm, K), lambda i, j: (i, 0)),
                  pl.BlockSpec((None, K, tn), lambda i, j: (l, 0, j))],
        out_specs=pl.BlockSpec((None, tm, tn), lambda i, j: (j, i, 0)),
        compiler_params=_cparams(("parallel", "arbitrary")),
    )(x, w_in)


def _proj_tail(x, w_tail, l, tm):
    M, K = x.shape
    return pl.pallas_call(
        _mm_kernel,
        out_shape=jax.ShapeDtypeStruct((M, N_TAIL), F32),
        grid=(M // tm,),
        in_specs=[pl.BlockSpec((tm, K), lambda i: (i, 0)),
                  pl.BlockSpec((None, K, N_TAIL), lambda i: (l, 0, 0))],
        out_specs=pl.BlockSpec((tm, N_TAIL), lambda i: (i, 0)),
        compiler_params=_cparams(("parallel",)),
    )(x, w_tail)


def _mm_ln_kernel(*refs, n_a, n_k):
    a_refs = refs[:n_a]
    w_ref, x_ref, g_ref, b_ref, o32_ref, o16_ref, acc_ref = refs[n_a:]
    k = pl.program_id(1)

    @pl.when(k == 0)
    def _():
        acc_ref[...] = jnp.zeros_like(acc_ref)

    if n_a == 1:
        acc_ref[...] += _dot(a_refs[0][...].astype(BF16), w_ref[...].astype(BF16))
    else:
        for kk in range(n_a):
            @pl.when(k == kk)
            def _(a_ref=a_refs[kk]):
                acc_ref[...] += _dot(a_ref[...].astype(BF16), w_ref[...].astype(BF16))

    @pl.when(k == n_k - 1)
    def _():
        z = DEEPNORM_ALPHA * x_ref[...] + acc_ref[...]
        mu = jnp.mean(z, axis=1, keepdims=True)
        zc = z - mu
        var = jnp.mean(zc * zc, axis=1, keepdims=True)
        y = zc * lax.rsqrt(var + LN_EPS) * g_ref[...] + b_ref[...]
        o32_ref[...] = y
        o16_ref[...] = y.astype(BF16)


def _mm_ln(a_list, w, l, x, g, b, tm, tk):
    M, N = x.shape
    n_a = len(a_list)
    n_k = n_a if n_a > 1 else a_list[0].shape[1] // tk
    if n_a > 1:
        a_specs = [pl.BlockSpec((tm, tk), lambda i, k: (i, 0)) for _ in a_list]
    else:
        a_specs = [pl.BlockSpec((tm, tk), lambda i, k: (i, k))]
    row = pl.BlockSpec((None, 1, N), lambda i, k: (l, 0, 0))
    return pl.pallas_call(
        functools.partial(_mm_ln_kernel, n_a=n_a, n_k=n_k),
        out_shape=(jax.ShapeDtypeStruct((M, N), F32), jax.ShapeDtypeStruct((M, N), BF16)),
        grid=(M // tm, n_k),
        in_specs=a_specs + [pl.BlockSpec((None, tk, N), lambda i, k: (l, k, 0)),
                            pl.BlockSpec((tm, N), lambda i, k: (i, 0)), row, row],
        out_specs=(pl.BlockSpec((tm, N), lambda i, k: (i, 0)),
                   pl.BlockSpec((tm, N), lambda i, k: (i, 0))),
        scratch_shapes=[pltpu.VMEM((tm, N), F32)],
        compiler_params=_cparams(("parallel", "arbitrary")),
    )(*a_list, w, x, g, b)


def _swiglu_kernel(x_ref, wg_ref, wu_ref, o_ref):
    a = x_ref[...].astype(BF16)
    gate = _dot(a, wg_ref[...].astype(BF16))
    up = _dot(a, wu_ref[...].astype(BF16))
    o_ref[...] = (gate * jax.nn.sigmoid(gate) * up).astype(o_ref.dtype)


def _swiglu(x16, wg, wu, l, tm, tn):
    M, K = x16.shape
    return pl.pallas_call(
        _swiglu_kernel,
        out_shape=jax.ShapeDtypeStruct((M, D_FF), BF16),
        grid=(D_FF // tn, M // tm),
        in_specs=[pl.BlockSpec((tm, K), lambda j, i: (i, 0)),
                  pl.BlockSpec((None, K, tn), lambda j, i: (l, 0, j)),
                  pl.BlockSpec((None, K, tn), lambda j, i: (l, 0, j))],
        out_specs=pl.BlockSpec((tm, tn), lambda j, i: (i, j)),
        compiler_params=_cparams(("parallel", "arbitrary")),
    )(x16, wg, wu)


def _flash_kernel(*refs, groups, dk, scale, has_mask, diff, T):
    if has_mask:
        q_ref, k_ref, v_ref, bias_ref, mask_ref, lam_ref, gain_ref, o_ref, m_sc, l_sc, acc_sc = refs
    else:
        q_ref, k_ref, v_ref, bias_ref, lam_ref, gain_ref, o_ref, m_sc, l_sc, acc_sc = refs
    i = pl.program_id(2)
    j = pl.program_id(3)
    RB = FLASH_ROW_BLOCK

    @pl.when(j == 0)
    def _():
        m_sc[...] = jnp.full(m_sc.shape, -jnp.inf, F32)
        l_sc[...] = jnp.zeros_like(l_sc)
        acc_sc[...] = jnp.zeros_like(acc_sc)

    def step(diagonal):
        q = (q_ref[...] * (scale * LOG2E)).astype(BF16)
        k = k_ref[...].astype(BF16)
        v_ext = jnp.concatenate([v_ref[...].astype(BF16), jnp.ones((T, LANES), BF16)], axis=1)
        for r in range(T // RB):
            rows = slice(r * RB, (r + 1) * RB)
            nc = (r + 1) * RB if diagonal else T
            bias = bias_ref[rows, 0:nc]
            if has_mask:
                bias = bias + mask_ref[rows, 0:nc].astype(F32)
            for g in range(groups):
                s = _dot_nt(q[rows, g * dk:(g + 1) * dk], k[0:nc, g * dk:(g + 1) * dk]) + bias
                m_prev = m_sc[g, rows, :]
                m_new = jnp.maximum(m_prev, jnp.max(s, axis=1, keepdims=True))
                p = jnp.exp2(s - jnp.concatenate([m_new] * (nc // LANES), axis=1))
                a = jnp.exp2(m_prev - m_new)
                pv = _dot(p.astype(BF16), v_ext[0:nc])
                acc_sc[g, rows, :] = a * acc_sc[g, rows, :] + pv[:, :LANES]
                l_sc[g, rows, :] = a * l_sc[g, rows, :] + pv[:, LANES:]
                m_sc[g, rows, :] = m_new

    @pl.when(j < i)
    def _():
        step(False)

    @pl.when(j == i)
    def _():
        step(True)
        if diff:
            o = acc_sc[0] / l_sc[0] - lam_ref[...] * (acc_sc[1] / l_sc[1])
            ms = jnp.mean(o * o, axis=1, keepdims=True)
            o = o * lax.rsqrt(ms + LN_EPS) * gain_ref[...]
        else:
            o = acc_sc[0] / l_sc[0]
        o_ref[...] = o


def _flash_prompt(p8, slabs, bias_tab, mask, lam_row, gain_row, *, B, L, T, groups, dk, scale, diff):
    nq = L // T
    H = 4
    has_mask = mask is not None
    sq, sk, sv = slabs
    qspec = pl.BlockSpec((None, T, LANES), lambda b, h, i, j: (sq, b * nq + i, h))
    kspec = pl.BlockSpec((None, T, LANES), lambda b, h, i, j: (sk, b * nq + jnp.minimum(j, i), h))
    vspec = pl.BlockSpec((None, T, LANES), lambda b, h, i, j: (sv, b * nq + jnp.minimum(j, i), h))
    bspec = pl.BlockSpec((None, None, T, T),
                         lambda b, h, i, j: (h, jnp.minimum(jnp.maximum(i - j, 0), 2), 0, 0))
    rowspec = pl.BlockSpec((1, LANES), lambda b, h, i, j: (0, 0))
    in_specs = [qspec, kspec, vspec, bspec]
    args = [p8, p8, p8, bias_tab]
    if has_mask:
        in_specs.append(pl.BlockSpec((T, T), lambda b, h, i, j: (b * nq + i, jnp.minimum(j, i))))
        args.append(mask)
    in_specs += [rowspec, rowspec]
    args += [lam_row, gain_row]
    return pl.pallas_call(
        functools.partial(_flash_kernel, groups=groups, dk=dk, scale=scale, has_mask=has_mask, diff=diff, T=T),
        out_shape=jax.ShapeDtypeStruct((B * L, H * LANES), F32),
        grid=(B, H, nq, nq),
        in_specs=in_specs,
        out_specs=pl.BlockSpec((T, LANES), lambda b, h, i, j: (b * nq + i, h)),
        scratch_shapes=[pltpu.VMEM((groups, T, LANES), F32), pltpu.VMEM((groups, T, LANES), F32),
                        pltpu.VMEM((groups, T, LANES), F32)],
        compiler_params=_cparams(("parallel", "parallel", "parallel", "arbitrary")),
    )(*args)


def _sortable(x):
    i = lax.bitcast_convert_type(x, jnp.int32)
    return jnp.where(i < 0, i ^ jnp.int32(0x7FFFFFFF), i)


def _topk_bias(key_ref, o_ref, topk, n_chunks, cw):
    R = key_ref.shape[0]
    kf = float(topk)
    nl = cw // LANES

    def chunk_at(c):
        return pl.ds(pl.multiple_of(c * cw, LANES), cw)

    def count(pred):
        def body(c, acc):
            hit = jnp.where(pred(key_ref[:, chunk_at(c)]), 1.0, 0.0)
            for t in range(nl):
                acc = acc + hit[:, t * LANES:(t + 1) * LANES]
            return acc
        acc = lax.fori_loop(0, n_chunks, body, jnp.zeros((R, LANES), F32))
        return jnp.sum(acc, axis=1, keepdims=True)

    base = jnp.where(count(lambda k: k >= 0) >= kf, 0, INT_MIN).astype(jnp.int32)

    def bit_step(it, base):
        cand = base | jnp.left_shift(jnp.int32(1), 30 - it)
        return jnp.where(count(lambda k: k >= cand) >= kf, cand, base)

    base = lax.fori_loop(0, 31, bit_step, base)
    has_excess = jnp.max(count(lambda k: k >= base)) > kf

    @pl.when(jnp.logical_not(has_excess))
    def _():
        def write(c, carry):
            k = key_ref[:, chunk_at(c)]
            sel = (k >= base) & (k > KEY_NEG_INF)
            o_ref[:, chunk_at(c)] = jnp.where(sel, 0.0, NEG_BIAS).astype(o_ref.dtype)
            return carry
        lax.fori_loop(0, n_chunks, write, 0)

    @pl.when(has_excess)
    def _():
        need = kf - count(lambda k: k > base)
        before = (lax.broadcasted_iota(jnp.int32, (LANES, LANES), 0)
                  < lax.broadcasted_iota(jnp.int32, (LANES, LANES), 1))
        before = jnp.where(before, 1.0, 0.0)

        def lane_tile(c, seen):
            sl = pl.ds(pl.multiple_of(c * LANES, LANES), LANES)
            k = key_ref[:, sl]
            eq = k == base
            eqf = jnp.where(eq, 1.0, 0.0)
            rank = _dot(eqf, before) + seen
            sel = ((k > base) | (eq & (rank < need))) & (k > KEY_NEG_INF)
            o_ref[:, sl] = jnp.where(sel, 0.0, NEG_BIAS).astype(o_ref.dtype)
            return seen + jnp.sum(eqf, axis=1, keepdims=True)

        lax.fori_loop(0, n_chunks * nl, lane_tile, jnp.zeros((R, 1), F32))


def _idx_prompt_kernel(iq_ref, ik_ref, iw_ref, o_ref, key_sc, *, tq, kc, topk):
    i = pl.program_id(1)
    iq = iq_ref[...]
    lhs = jnp.concatenate([iq[:, h * D_IDX:(h + 1) * D_IDX] for h in range(H_IDX)], axis=0).astype(BF16)
    iw = iw_ref[:, D_IDX:D_IDX + H_IDX]
    qpos = i * tq + lax.broadcasted_iota(jnp.int32, (tq, 1), 0)
    n_need = (i * tq + tq + kc - 1) // kc
    o_ref[...] = jnp.full(o_ref.shape, NEG_BIAS, o_ref.dtype)

    def score_chunk(c, carry):
        off = pl.multiple_of(c * kc, kc)
        ik = ik_ref[pl.ds(off, kc), 0:D_IDX].astype(BF16)
        s = _dot_nt(lhs, ik)
        score = jnp.zeros((tq, kc), F32)
        for h in range(H_IDX):
            score = score + jnp.maximum(s[h * tq:(h + 1) * tq], 0.0) * iw[:, h:h + 1]
        kpos = off + lax.broadcasted_iota(jnp.int32, (1, kc), 1)
        score = jnp.where(kpos <= qpos, score, -jnp.inf)
        key_sc[:, pl.ds(off, kc)] = _sortable(score)
        return carry

    lax.fori_loop(0, n_need, score_chunk, 0)
    _topk_bias(key_sc, o_ref, topk, n_need, kc)


def _idx_prompt(p8, tail, *, B, L, tq, topk):
    nq = L // tq
    nmisc = TAIL_MISC // LANES
    return pl.pallas_call(
        functools.partial(_idx_prompt_kernel, tq=tq, kc=min(L, 512), topk=topk),
        out_shape=jax.ShapeDtypeStruct((B * L, L), BF16),
        grid=(B, nq),
        in_specs=[pl.BlockSpec((None, tq, W_GROUP), lambda b, i: (7, b * nq + i, 0)),
                  pl.BlockSpec((L, LANES), lambda b, i: (b, nmisc)),
                  pl.BlockSpec((tq, LANES), lambda b, i: (b * nq + i, nmisc))],
        out_specs=pl.BlockSpec((tq, L), lambda b, i: (b * nq + i, 0)),
        scratch_shapes=[pltpu.VMEM((tq, L), jnp.int32)],
        compiler_params=_cparams(("parallel", "arbitrary")),
    )(p8, tail, tail)


def _gelu_tanh(x):
    return 0.5 * x * (1.0 + jnp.tanh(math.sqrt(2.0 / math.pi) * (x + 0.044715 * (x * x * x))))


def _s5_kernel(u_ref, h0r_ref, h0i_ref, bre_ref, bim_ref, apr_ref, api_ref, pwr_ref, pwi_ref,
               cre_ref, cim_ref, d_ref, wglu_ref, y_ref, hr_ref, hi_ref, cr_sc, ci_sc, *, T, nsteps):
    t = pl.program_id(1)

    @pl.when(t == 0)
    def _():
        cr_sc[...] = h0r_ref[...]
        ci_sc[...] = h0i_ref[...]

    u = u_ref[...]
    ub = u.astype(BF16)
    hr = _dot(ub, bre_ref[...])
    hi = _dot(ub, bim_ref[...])
    row = lax.broadcasted_iota(jnp.int32, (T, 1), 0)
    for j in range(nsteps):
        d = 1 << j
        sr = jnp.where(row >= d, pltpu.roll(hr, d, 0), 0.0)
        si = jnp.where(row >= d, pltpu.roll(hi, d, 0), 0.0)
        ar = apr_ref[j:j + 1, :]
        ai = api_ref[j:j + 1, :]
        hr, hi = hr + (ar * sr - ai * si), hi + (ar * si + ai * sr)
    cr = cr_sc[...]
    ci = ci_sc[...]
    pr = pwr_ref[...]
    pi_ = pwi_ref[...]
    hr, hi = hr + (pr * cr - pi_ * ci), hi + (pr * ci + pi_ * cr)
    cr_sc[...] = hr[T - 1:T, :]
    ci_sc[...] = hi[T - 1:T, :]
    hr_ref[...] = hr[T - 1:T, :]
    hi_ref[...] = hi[T - 1:T, :]
    y = _dot(hr.astype(BF16), cre_ref[...]) - _dot(hi.astype(BF16), cim_ref[...]) + d_ref[...] * u
    y = _gelu_tanh(y)
    gu = _dot(y.astype(BF16), wglu_ref[...])
    y_ref[...] = gu[:, :W_GROUP] * jax.nn.sigmoid(gu[:, W_GROUP:])


def _s5(p8, h0r, h0i, tabs, l, *, B, L, T):
    nt = L // T
    nsteps = T.bit_length() - 1
    bre, bim, apr, api, pwr, pwi, cre, cim, dsk, wglu = tabs
    lay = lambda *shape: pl.BlockSpec((None,) + shape, lambda b, t: (l,) + (0,) * len(shape))
    st = pl.BlockSpec((None, 1, NSTATE), lambda b, t: (b, 0, 0))
    return pl.pallas_call(
        functools.partial(_s5_kernel, T=T, nsteps=nsteps),
        out_shape=(jax.ShapeDtypeStruct((B * L, W_GROUP), F32),
                   jax.ShapeDtypeStruct((B, 1, NSTATE), F32), jax.ShapeDtypeStruct((B, 1, NSTATE), F32)),
        grid=(B, nt),
        in_specs=[pl.BlockSpec((None, T, W_GROUP), lambda b, t: (3, b * nt + t, 0)), st, st,
                  lay(W_GROUP, NSTATE), lay(W_GROUP, NSTATE), lay(8, NSTATE), lay(8, NSTATE),
                  lay(T, NSTATE), lay(T, NSTATE), lay(NSTATE, W_GROUP), lay(NSTATE, W_GROUP),
                  lay(1, W_GROUP), lay(W_GROUP, 2 * W_GROUP)],
        out_specs=(pl.BlockSpec((T, W_GROUP), lambda b, t: (b * nt + t, 0)), st, st),
        scratch_shapes=[pltpu.VMEM((1, NSTATE), F32), pltpu.VMEM((1, NSTATE), F32)],
        compiler_params=_cparams(("parallel", "arbitrary")),
    )(p8, h0r, h0i, bre, bim, apr, api, pwr, pwi, cre, cim, dsk, wglu)


def _log_sigmoid(x):
    return jnp.minimum(x, 0.0) - jnp.log1p(jnp.exp(-jnp.abs(x)))


def _gla_kernel(q_ref, k_ref, v_ref, r_ref, misc_ref, wg_ref, bg_ref, ng_ref, s0_ref,
                y_ref, sfin_ref, s_sc, *, TB, CH, nt, valid_len):
    t = pl.program_id(1)
    HK = H_D * DK_D
    HV = H_D * DV_D

    @pl.when(t == 0)
    def _():
        s_sc[...] = jnp.zeros_like(s_sc)
        for h in range(H_D):
            s_sc[h * DK_D:(h + 1) * DK_D, h * DV_D:(h + 1) * DV_D] = s0_ref[h]

    row = lax.broadcasted_iota(jnp.int32, (TB, 1), 0)
    glow = misc_ref[:, D_IDX + H_IDX:D_IDX + H_IDX + GATE_RANK]
    logit = _dot(glow.astype(BF16), wg_ref[...].astype(BF16)) + bg_ref[...]
    la = _log_sigmoid(logit) * (1.0 / GATE_TAU)
    k = k_ref[...]
    if valid_len < TB:
        la = jnp.where(row < valid_len, la, 0.0)
        k = jnp.where(row < valid_len, k, 0.0)
    ri = lax.broadcasted_iota(jnp.int32, (TB, TB), 0)
    ci = lax.broadcasted_iota(jnp.int32, (TB, TB), 1)
    same = (ri // CH) == (ci // CH)
    lower = same & (ci <= ri)
    tri = jnp.where(lower, 1.0, 0.0).astype(BF16)
    blk = jnp.where(same, 1.0, 0.0).astype(BF16)
    la_hi = la.astype(BF16)
    la_lo = (la - la_hi.astype(F32)).astype(BF16)
    bc = _dot(tri, la_hi) + _dot(tri, la_lo)
    bl = _dot(blk, la_hi) + _dot(blk, la_lo)
    q = q_ref[...] * (DK_D ** -0.5)
    v = v_ref[...]
    vb = v.astype(BF16)
    qt = (q * jnp.exp(bc)).astype(BF16)
    kc = (k * jnp.exp(-bc)).astype(BF16)
    kh_t = jnp.transpose(k * jnp.exp(bl - bc))
    dec_t = jnp.transpose(jnp.exp(bl))

    o_parts = []
    for h in range(H_D):
        att = _dot_nt(qt[:, h * DK_D:(h + 1) * DK_D], kc[:, h * DK_D:(h + 1) * DK_D])
        att = jnp.where(lower, att, 0.0)
        o_parts.append(_dot(att.astype(BF16), vb[:, h * DV_D:(h + 1) * DV_D]))
    o_intra = jnp.concatenate(o_parts, axis=1)

    head_blk = (lax.broadcasted_iota(jnp.int32, (HK, HV), 0) // DK_D
                == lax.broadcasted_iota(jnp.int32, (HK, HV), 1) // DV_D)
    col = lax.broadcasted_iota(jnp.int32, (1, TB), 1)
    S = s_sc[...]
    o_rows = []
    for c in range(TB // CH):
        o_rows.append(_dot(qt[c * CH:(c + 1) * CH, :], S.astype(BF16)))
        in_chunk = (col >= c * CH) & (col < (c + 1) * CH)
        kv = _dot(jnp.where(in_chunk, kh_t, 0.0).astype(BF16), vb)
        S = dec_t[:, c * CH:c * CH + 1] * S + jnp.where(head_blk, kv, 0.0)
    s_sc[...] = S
    o = o_intra + jnp.concatenate(o_rows, axis=0)

    outs = []
    for h in range(H_D):
        oh = o[:, h * DV_D:(h + 1) * DV_D]
        ms = jnp.mean(oh * oh, axis=1, keepdims=True)
        outs.append(oh * lax.rsqrt(ms + LN_EPS) * ng_ref[...])
    r = r_ref[...]
    y_ref[...] = jnp.concatenate(outs, axis=1) * (r * jax.nn.sigmoid(r))

    @pl.when(t == nt - 1)
    def _():
        for h in range(H_D):
            sfin_ref[h] = S[h * DK_D:(h + 1) * DK_D, h * DV_D:(h + 1) * DV_D]


def _gla(tail, s0, w_gate, b_gate, norm_g, l, *, B, L, TB, valid_len):
    nt = L // TB
    lay = lambda *shape: pl.BlockSpec((None,) + shape, lambda b, t: (l,) + (0,) * len(shape))
    sspec = pl.BlockSpec((None, H_D, DK_D, DV_D), lambda b, t: (b, 0, 0, 0))
    return pl.pallas_call(
        functools.partial(_gla_kernel, TB=TB, CH=GLA_CHUNK, nt=nt, valid_len=valid_len),
        out_shape=(jax.ShapeDtypeStruct((B * L, W_GROUP), F32),
                   jax.ShapeDtypeStruct((B, H_D, DK_D, DV_D), F32)),
        grid=(B, nt),
        in_specs=[pl.BlockSpec((TB, 256), lambda b, t: (b * nt + t, 0)),
                  pl.BlockSpec((TB, 256), lambda b, t: (b * nt + t, 1)),
                  pl.BlockSpec((TB, 512), lambda b, t: (b * nt + t, 1)),
                  pl.BlockSpec((TB, 512), lambda b, t: (b * nt + t, 2)),
                  pl.BlockSpec((TB, LANES), lambda b, t: (b * nt + t, TAIL_MISC // LANES)),
                  lay(GATE_RANK, H_D * DK_D), lay(1, H_D * DK_D), lay(1, DV_D), sspec],
        out_specs=(pl.BlockSpec((TB, W_GROUP), lambda b, t: (b * nt + t, 0)), sspec),
        scratch_shapes=[pltpu.VMEM((H_D * DK_D, H_D * DV_D), F32)],
        compiler_params=_cparams(("parallel", "arbitrary")),
    )(tail, tail, tail, tail, tail, w_gate, b_gate, norm_g, s0)


def _sidx_kernel(pt_ref, iq_ref, iw_ref, *rest, PPS, NS, n_new, topk):
    page_refs = rest[:PPS]
    new_ref, o_ref, key_sc = rest[PPS:]
    s_id = pl.program_id(1)
    iq = iq_ref[...].astype(BF16)
    iw = iw_ref[...]

    def scores(keys):
        hs = jnp.maximum(_dot_nt(iq, keys.astype(BF16)), 0.0) * iw
        sc = jnp.zeros((n_new, LANES), F32)
        for h in range(H_IDX):
            sc = sc + hs[h * n_new:(h + 1) * n_new]
        return sc

    for p in range(PPS):
        off = pl.multiple_of((s_id * PPS + p) * LANES, LANES)
        key_sc[:, pl.ds(off, LANES)] = _sortable(scores(page_refs[p][...]))

    @pl.when(s_id == NS - 1)
    def _():
        sc = scores(new_ref[...])
        qi = lax.broadcasted_iota(jnp.int32, (n_new, LANES), 0)
        kj = lax.broadcasted_iota(jnp.int32, (n_new, LANES), 1)
        sc = jnp.where(kj <= qi, sc, -jnp.inf)
        key_sc[:, NS * PPS * LANES:(NS * PPS + 1) * LANES] = _sortable(sc)
        n_tiles = NS * PPS + 1
        n_chunks = 3 if n_tiles % 3 == 0 else 1
        _topk_bias(key_sc, o_ref, topk, n_chunks, n_tiles // n_chunks * LANES)


def _sidx(page_table, iq_rows, iw_tile, cache_idx, idx_new, l, *, PPS, topk):
    Bs, n_pages = page_table.shape
    NS = n_pages // PPS
    n_new = iq_rows.shape[1] // H_IDX
    W = (n_pages + 1) * LANES

    def page_spec(p):
        return pl.BlockSpec((None, None, PAGE_SIZE, D_IDX),
                            lambda b, s, pt: (l, pt[b, s * PPS + p], 0, 0))

    grid_spec = pltpu.PrefetchScalarGridSpec(
        num_scalar_prefetch=1,
        grid=(Bs, NS),
        in_specs=[pl.BlockSpec((None, H_IDX * n_new, D_IDX), lambda b, s, pt: (b, 0, 0)),
                  pl.BlockSpec((None, H_IDX * n_new, LANES), lambda b, s, pt: (b, 0, 0))]
                 + [page_spec(p) for p in range(PPS)]
                 + [pl.BlockSpec((None, PAGE_SIZE, D_IDX), lambda b, s, pt: (b, 0, 0))],
        out_specs=pl.BlockSpec((None, n_new, W), lambda b, s, pt: (b, 0, 0)),
        scratch_shapes=[pltpu.VMEM((n_new, W), jnp.int32)],
    )
    return pl.pallas_call(
        functools.partial(_sidx_kernel, PPS=PPS, NS=NS, n_new=n_new, topk=topk),
        out_shape=jax.ShapeDtypeStruct((Bs, n_new, W), F32),
        grid_spec=grid_spec,
        compiler_params=_cparams(("parallel", "arbitrary")),
    )(page_table, iq_rows, iw_tile, *([cache_idx] * PPS), idx_new)


def _sattn_kernel(pt_ref, q_ref, *rest, PPS, NS, R, n_new, has_mask, diff):
    k_refs = rest[:PPS]
    v_refs = rest[PPS:2 * PPS]
    rest = rest[2 * PPS:]
    knew_ref, vnew_ref, far_ref, near_ref = rest[:4]
    rest = rest[4:]
    if has_mask:
        mask_ref, masknew_ref, spread_ref = rest[:3]
        rest = rest[3:]
    lam_ref, gain_ref, o_ref, m_sc, l_sc, acc_sc = rest
    s_id = pl.program_id(1)

    @pl.when(s_id == 0)
    def _():
        m_sc[...] = jnp.full(m_sc.shape, -jnp.inf, F32)
        l_sc[...] = jnp.zeros_like(l_sc)
        acc_sc[...] = jnp.zeros_like(acc_sc)

    q = q_ref[...].astype(BF16)
    PR = PAGE_SIZE * 4

    def expand(m):
        return jnp.concatenate([_dot(m, spread_ref[...])] * (R // n_new), axis=0)

    def pages(kps, vps, biases):
        ss = [_dot_nt(q, kp.astype(BF16)) + b for kp, b in zip(kps, biases)]
        top = ss[0]
        for s in ss[1:]:
            top = jnp.maximum(top, s)
        m_prev = m_sc[...]
        m_new = jnp.maximum(m_prev, jnp.max(top, axis=1, keepdims=True))
        a = jnp.exp2(m_prev - m_new)
        ps = [jnp.exp2(s - m_new) for s in ss]
        tot = ps[0]
        for p in ps[1:]:
            tot = tot + p
        pv = _dot(ps[0].astype(BF16), vps[0].astype(BF16))
        for p, vp in zip(ps[1:], vps[1:]):
            pv = pv + _dot(p.astype(BF16), vp.astype(BF16))
        l_sc[...] = a * l_sc[...] + jnp.sum(tot, axis=1, keepdims=True)
        acc_sc[...] = a * acc_sc[...] + pv
        m_sc[...] = m_new

    far = far_ref[...]
    biases = []
    for p in range(PPS):
        bias = far
        if p == PPS - 1:
            bias = jnp.where(s_id == NS - 1, near_ref[:, 0:PR], far)
        if has_mask:
            bias = bias + expand(mask_ref[:, p * LANES:(p + 1) * LANES])
        biases.append(bias)
    pages([r[...] for r in k_refs], [r[...] for r in v_refs], biases)

    @pl.when(s_id == NS - 1)
    def _():
        bias = near_ref[:, PR:2 * PR]
        if has_mask:
            bias = bias + expand(masknew_ref[...])
        pages([knew_ref[...]], [vnew_ref[...]], [bias])
        o_all = acc_sc[...] / l_sc[...]
        outs = []
        for h in range(4):
            if diff:
                o1 = o_all[(2 * h) * n_new:(2 * h + 1) * n_new]
                o2 = o_all[(2 * h + 1) * n_new:(2 * h + 2) * n_new]
                o = o1 - lam_ref[...] * o2
                ms = jnp.mean(o * o, axis=1, keepdims=True)
                outs.append(o * lax.rsqrt(ms + LN_EPS) * gain_ref[...])
            else:
                outs.append(o_all[h * n_new:(h + 1) * n_new])
        o_ref[...] = jnp.concatenate(outs, axis=1)


def _sattn(page_table, q_rows, cache_k, cache_v, knew, vnew, far, near, mask, lam_row, gain_row, l,
           *, PPS, diff):
    Bs, n_pages = page_table.shape
    NS = n_pages // PPS
    R = q_rows.shape[1]
    n_new = 8
    PR = PAGE_SIZE * 4
    has_mask = mask is not None

    def page_spec(p):
        return pl.BlockSpec((None, None, PR, LANES), lambda b, s, pt: (l, pt[b, s * PPS + p], 0, 0))

    per_b = lambda *shape: pl.BlockSpec((None,) + shape, lambda b, s, pt: (b,) + (0,) * len(shape))
    const = lambda *shape: pl.BlockSpec(shape, lambda b, s, pt: (0,) * len(shape))
    in_specs = ([per_b(R, LANES)] + [page_spec(p) for p in range(PPS)] * 2
                + [per_b(PR, LANES), per_b(PR, LANES), const(R, PR), const(R, 2 * PR)])
    args = [q_rows] + [cache_k] * PPS + [cache_v] * PPS + [knew, vnew, far, near]
    if has_mask:
        spread = (jnp.arange(PR, dtype=jnp.int32)[None, :] // 4
                  == jnp.arange(PAGE_SIZE, dtype=jnp.int32)[:, None]).astype(F32)
        in_specs += [pl.BlockSpec((None, n_new, PPS * LANES), lambda b, s, pt: (b, 0, s)),
                     pl.BlockSpec((None, n_new, LANES), lambda b, s, pt: (b, 0, n_pages)),
                     const(PAGE_SIZE, PR)]
        args += [mask, mask, spread]
    in_specs += [const(1, LANES), const(1, LANES)]
    args += [lam_row, gain_row]
    grid_spec = pltpu.PrefetchScalarGridSpec(
        num_scalar_prefetch=1,
        grid=(Bs, NS),
        in_specs=in_specs,
        out_specs=pl.BlockSpec((None, n_new, W_GROUP), lambda b, s, pt: (b, 0, 0)),
        scratch_shapes=[pltpu.VMEM((R, 1), F32), pltpu.VMEM((R, 1), F32), pltpu.VMEM((R, LANES), F32)],
    )
    return pl.pallas_call(
        functools.partial(_sattn_kernel, PPS=PPS, NS=NS, R=R, n_new=n_new,
                          has_mask=has_mask, diff=diff),
        out_shape=jax.ShapeDtypeStruct((Bs, n_new, W_GROUP), F32),
        grid_spec=grid_spec,
        compiler_params=_cparams(("parallel", "arbitrary")),
    )(page_table, *args)


def _rel_bucket(dist):
    n = jnp.maximum(dist, 0)
    nf = jnp.maximum(n, REL_MAX_EXACT).astype(F32)
    large = REL_MAX_EXACT + (jnp.log(nf / REL_MAX_EXACT) / math.log(REL_MAX_DIST / REL_MAX_EXACT)
                             * (REL_BUCKETS - REL_MAX_EXACT)).astype(jnp.int32)
    large = jnp.minimum(large, REL_BUCKETS - 1)
    return jnp.where(n < REL_MAX_EXACT, n, large)


def _rel_lookup(rel, bucket):
    out = jnp.zeros((rel.shape[1],) + bucket.shape, F32)
    for b in range(REL_BUCKETS):
        out = jnp.where(bucket[None] == b, rel[b].reshape((-1,) + (1,) * bucket.ndim), out)
    return out * LOG2E


def _prompt_bias_tables(rel, T):
    assert T >= REL_MAX_DIST
    r = jnp.arange(T, dtype=jnp.int32)[:, None]
    c = jnp.arange(T, dtype=jnp.int32)[None, :]
    d0 = r - c
    t0 = jnp.where((d0 >= 0)[None], _rel_lookup(rel, _rel_bucket(d0)), NEG_BIAS)
    t1 = _rel_lookup(rel, _rel_bucket(T + r - c))
    t2 = _rel_lookup(rel, _rel_bucket(jnp.full((T, T), 2 * T, jnp.int32)))
    return jnp.stack([t0, t1, t2], axis=1)


def _sample_bias_tables(rel, n_new, reps):
    H = rel.shape[1]
    t = jnp.arange(n_new, dtype=jnp.int32)[:, None]
    c = jnp.arange(PAGE_SIZE, dtype=jnp.int32)[None, :]
    last = _rel_lookup(rel, _rel_bucket(PAGE_SIZE + t - c))
    dn = t - c
    new = jnp.where(((dn >= 0) & (c < n_new))[None], _rel_lookup(rel, _rel_bucket(dn)), NEG_BIAS)
    far = _rel_lookup(rel, _rel_bucket(jnp.full((n_new, PAGE_SIZE), 2 * PAGE_SIZE, jnp.int32)))

    def rows(x):
        own = jnp.eye(H, dtype=bool)[:, None, None, :]
        y = jnp.where(own, x[..., None], NEG_BIAS).reshape(H, 1, n_new, PAGE_SIZE * H)
        return jnp.broadcast_to(y, (H, reps, n_new, PAGE_SIZE * H)).reshape(H * reps * n_new, PAGE_SIZE * H)

    return rows(far), jnp.concatenate([rows(last), rows(new)], axis=1)


def _s5_tables(a_re, a_im, log_dt, b_re, b_im, c_re, c_im, d_skip, w_glu, T):
    dt = jnp.exp(log_dt.astype(F32))[..., None]
    lam_re, lam_im = a_re.astype(F32), a_im.astype(F32)
    z_re, z_im = lam_re * dt, lam_im * dt
    mag = jnp.exp(z_re)
    e_re, e_im = mag * jnp.cos(z_im), mag * jnp.sin(z_im)
    den = lam_re * lam_re + lam_im * lam_im
    f_re = ((e_re - 1.0) * lam_re + e_im * lam_im) / den
    f_im = (e_im * lam_re - (e_re - 1.0) * lam_im) / den
    br, bi = b_re.astype(F32), b_im.astype(F32)
    bb_re = f_re[..., None] * br - f_im[..., None] * bi
    bb_im = f_re[..., None] * bi + f_im[..., None] * br
    eye = jnp.eye(G_B, dtype=F32)
    bd_in = lambda bb: jnp.einsum('lgpc,gh->lgchp', bb, eye).reshape(DEPTH, W_GROUP, NSTATE).astype(BF16)
    bd_out = lambda cc: jnp.einsum('lgcp,gh->lgphc', cc.astype(F32), eye).reshape(DEPTH, NSTATE, W_GROUP).astype(BF16)

    def powers(ts):
        tt = ts[None, :, None, None]
        pm = jnp.exp(z_re[:, None] * tt)
        return ((pm * jnp.cos(z_im[:, None] * tt)).reshape(DEPTH, -1, NSTATE),
                (pm * jnp.sin(z_im[:, None] * tt)).reshape(DEPTH, -1, NSTATE))

    apr, api = powers(jnp.asarray([2.0 ** j for j in range(8)], F32))
    pwr, pwi = powers(jnp.arange(1, T + 1, dtype=F32))
    return (bd_in(bb_re), bd_in(bb_im), apr, api, pwr, pwi, bd_out(c_re), bd_out(c_im),
            d_skip.astype(F32).reshape(DEPTH, 1, W_GROUP), w_glu.astype(BF16))


def _tail_weights(w_in):
    pad = jnp.zeros(w_in.shape[:2] + (N_TAIL - 1624,), w_in.dtype)
    return jnp.concatenate([w_in[..., 4168:5704], w_in[..., 4096:4168], w_in[..., 5704:5720], pad],
                           axis=-1).astype(BF16)


def _channel_mix(ys, x, l, W, tm):
    x1, x1b = _mm_ln(ys, W['w_out'], l, x, W['ln1_g'], W['ln1_b'], tm, W_GROUP)
    hid = _swiglu(x1b, W['ffn_w_gate'], W['ffn_w_up'], l, tm, 512)
    return _mm_ln([hid], W['ffn_w_down'], l, x1, W['ln2_g'], W['ln2_b'], tm, 512)


def _prompt_layer(x, xb, l, W, B, L):
    T_ATT = 512
    p8 = _proj_main(xb, W['w_in'], l, 512)
    tail = _proj_tail(xb, W['w_tail'], l, 512)
    y_a = _flash_prompt(p8, (0, 1, 2), W['tab_a'], None, W['lam_row'][l], W['gain_a'][l],
                        B=B, L=L, T=T_ATT, groups=2, dk=DH_A, scale=DH_A ** -0.5, diff=True)
    zeros_h = jnp.zeros((B, 1, NSTATE), F32)
    y_b, h_re, h_im = _s5(p8, zeros_h, zeros_h, W['s5_prompt'], l, B=B, L=L, T=W['s5_T'])
    mask = _idx_prompt(p8, tail, B=B, L=L, tq=128, topk=min(TOPK_MAX, L // 4))
    y_c = _flash_prompt(p8, (4, 5, 6), W['tab_c'], mask, W['lam_row'][l], W['gain_a'][l],
                        B=B, L=L, T=T_ATT, groups=1, dk=DH_C, scale=DH_C ** -0.5, diff=False)
    y_d, s_fin = _gla(tail, jnp.zeros((B, H_D, DK_D, DV_D), F32), W['gla_w_gate'], W['gla_b_gate'],
                      W['gla_norm'], l, B=B, L=L, TB=128, valid_len=128)
    x, xb = _channel_mix([y_a, y_b, y_c, y_d], x, l, W, 512)
    news = (p8[1].reshape(B, L, H_A, 128), p8[2].reshape(B, L, H_A, 128),
            p8[5].reshape(B, L, H_C, 128), p8[6].reshape(B, L, H_C, 128),
            tail[:, TAIL_MISC:TAIL_MISC + D_IDX].reshape(B, L, D_IDX),
            h_re.reshape(B, G_B, P_B), h_im.reshape(B, G_B, P_B), s_fin)
    return x, xb, news


def _sample_layer(x, xb, l, W, C, Bs, Ls):
    M = Bs * Ls
    PPS = 8
    p8 = _proj_main(xb, W['w_in'], l, M)
    tail = _proj_tail(xb, W['w_tail'], l, M)
    pt = C['page_table']

    def pad_new(z):
        z = z.reshape(Bs, Ls, z.shape[-1])
        return jnp.pad(z, ((0, 0), (0, PAGE_SIZE - Ls), (0, 0)))

    def pad_rows(z):
        z = z.reshape(Bs, Ls * 4, LANES)
        return jnp.pad(z, ((0, 0), (0, PAGE_SIZE * 4 - Ls * 4), (0, 0)))

    qa = p8[0].reshape(Bs, Ls, H_A, 2, DH_A) * (DH_A ** -0.5 * LOG2E)
    q_a = jnp.einsum('bqhwd,wv->bhwqvd', qa, jnp.eye(2, dtype=F32)).reshape(Bs, H_A * 2 * Ls, LANES)
    y_a = _sattn(pt, q_a, C['a_k'], C['a_v'], pad_rows(p8[1]), pad_rows(p8[2]), W['far_a'], W['near_a'],
                 None, W['lam_row'][l], W['gain_a'][l], l, PPS=PPS, diff=True)
    y_b, h_re, h_im = _s5(p8, C['b_re'][l].reshape(Bs, 1, NSTATE), C['b_im'][l].reshape(Bs, 1, NSTATE),
                          W['s5_sample'], l, B=Bs, L=Ls, T=Ls)
    misc = tail[:, TAIL_MISC:]
    iq_rows = p8[7].reshape(Bs, Ls, H_IDX, D_IDX).transpose(0, 2, 1, 3).reshape(Bs, H_IDX * Ls, D_IDX)
    iw = misc[:, D_IDX:D_IDX + H_IDX].reshape(Bs, Ls, H_IDX).transpose(0, 2, 1).reshape(Bs, H_IDX * Ls, 1)
    iw_tile = jnp.broadcast_to(iw, (Bs, H_IDX * Ls, LANES))
    mask = _sidx(pt, iq_rows, iw_tile, C['c_idx'], pad_new(misc[:, :D_IDX]), l, PPS=PPS,
                 topk=min(TOPK_MAX, (pt.shape[1] * PAGE_SIZE + Ls) // 4))
    q_c = (p8[4].reshape(Bs, Ls, H_C, DH_C) * (DH_C ** -0.5 * LOG2E)).transpose(0, 2, 1, 3)
    y_c = _sattn(pt, q_c.reshape(Bs, H_C * Ls, LANES), C['c_k'], C['c_v'], pad_rows(p8[5]), pad_rows(p8[6]),
                 W['far_c'], W['near_c'], mask, W['lam_row'][l], W['gain_a'][l], l, PPS=PPS, diff=False)
    tail_pad = pad_new(tail).reshape(Bs * PAGE_SIZE, N_TAIL)
    y_d, s_fin = _gla(tail_pad, C['d'][l], W['gla_w_gate'], W['gla_b_gate'], W['gla_norm'], l,
                      B=Bs, L=PAGE_SIZE, TB=PAGE_SIZE, valid_len=Ls)
    y_d = y_d.reshape(Bs, PAGE_SIZE, W_GROUP)[:, :Ls].reshape(M, W_GROUP)
    x, xb = _channel_mix([y_a.reshape(M, W_GROUP), y_b, y_c.reshape(M, W_GROUP), y_d], x, l, W, M)
    news = (p8[1].reshape(Bs, Ls, H_A, 128), p8[2].reshape(Bs, Ls, H_A, 128),
            p8[5].reshape(Bs, Ls, H_C, 128), p8[6].reshape(Bs, Ls, H_C, 128),
            misc[:, :D_IDX].reshape(Bs, Ls, D_IDX),
            h_re.reshape(Bs, G_B, P_B), h_im.reshape(Bs, G_B, P_B), s_fin)
    return x, xb, news


def kernel(x_prompt, x_sample, cache_a_k, cache_a_v, cache_c_k, cache_c_v, cache_c_idx, state_b_re, state_b_im, state_d, page_table, rel_bias, w_in, w_out, lam_q1, lam_k1, lam_q2, lam_k2, a_subln, s5_a_re, s5_a_im, s5_log_dt, s5_b_re, s5_b_im, s5_c_re, s5_c_im, s5_d, s5_w_glu, gla_w_gate, gla_b_gate, gla_norm, ln1_g, ln1_b, ffn_w_gate, ffn_w_up, ffn_w_down, ln2_g, ln2_b):
    B, L, _ = x_prompt.shape
    Bs, Ls, _ = x_sample.shape
    n_pool = cache_a_k.shape[1]
    S5_T = 256

    lam_init = jnp.asarray([0.8 - 0.6 * math.exp(-0.3 * l) for l in range(DEPTH)], F32)
    lam = (jnp.exp(jnp.sum(lam_q1.astype(F32) * lam_k1.astype(F32), axis=-1))
           - jnp.exp(jnp.sum(lam_q2.astype(F32) * lam_k2.astype(F32), axis=-1)) + lam_init)
    s5_args = (s5_a_re, s5_a_im, s5_log_dt, s5_b_re, s5_b_im, s5_c_re, s5_c_im, s5_d, s5_w_glu)
    rel = rel_bias.astype(F32)
    far_a, near_a = _sample_bias_tables(rel[:, :H_A], Ls, 2)
    far_c, near_c = _sample_bias_tables(rel[:, H_A:], Ls, 1)
    W = {
        'w_in': w_in, 'w_tail': _tail_weights(w_in), 'w_out': w_out,
        'ffn_w_gate': ffn_w_gate, 'ffn_w_up': ffn_w_up, 'ffn_w_down': ffn_w_down,
        'ln1_g': ln1_g.reshape(DEPTH, 1, D_MODEL), 'ln1_b': ln1_b.reshape(DEPTH, 1, D_MODEL),
        'ln2_g': ln2_g.reshape(DEPTH, 1, D_MODEL), 'ln2_b': ln2_b.reshape(DEPTH, 1, D_MODEL),
        'lam_row': jnp.broadcast_to(lam[:, None, None], (DEPTH, 1, LANES)),
        'gain_a': (a_subln.astype(F32) * (1.0 - lam_init)[:, None]).reshape(DEPTH, 1, LANES),
        'tab_a': _prompt_bias_tables(rel[:, :H_A], 512), 'tab_c': _prompt_bias_tables(rel[:, H_A:], 512),
        'far_a': far_a, 'near_a': near_a, 'far_c': far_c, 'near_c': near_c,
        's5_prompt': _s5_tables(*s5_args, S5_T), 's5_T': S5_T, 's5_sample': _s5_tables(*s5_args, Ls),
        'gla_w_gate': gla_w_gate, 'gla_b_gate': gla_b_gate.reshape(DEPTH, 1, H_D * DK_D),
        'gla_norm': gla_norm.reshape(DEPTH, 1, DV_D),
    }
    C = {
        'page_table': page_table,
        'a_k': cache_a_k.reshape(DEPTH, n_pool, PAGE_SIZE * H_A, LANES),
        'a_v': cache_a_v.reshape(DEPTH, n_pool, PAGE_SIZE * H_A, LANES),
        'c_k': cache_c_k.reshape(DEPTH, n_pool, PAGE_SIZE * H_C, LANES),
        'c_v': cache_c_v.reshape(DEPTH, n_pool, PAGE_SIZE * H_C, LANES),
        'c_idx': cache_c_idx, 'b_re': state_b_re, 'b_im': state_b_im, 'd': state_d,
    }

    xp = x_prompt.reshape(B * L, D_MODEL)
    xs = x_sample.reshape(Bs * Ls, D_MODEL)
    xpb, xsb = xp, xs
    news_p, news_s = [], []
    for l in range(DEPTH):
        xp, xpb, n_p = _prompt_layer(xp, xpb, l, W, B, L)
        xs, xsb, n_s = _sample_layer(xs, xsb, l, W, C, Bs, Ls)
        news_p.append(n_p)
        news_s.append(n_s)
    stack = lambda news: [jnp.stack(z, axis=0) for z in zip(*news)]
    return (xp.reshape(B, L, D_MODEL), xs.reshape(Bs, Ls, D_MODEL), *stack(news_p), *stack(news_s))
```

```python
import functools
import math

import jax
import jax.numpy as jnp
from jax import lax
from jax.experimental import pallas as pl
from jax.experimental.pallas import tpu as pltpu

F32 = jnp.float32
BF16 = jnp.bfloat16

D_MODEL = 2048
DEPTH = 4
PAST_LEN = 16384
PAGE_SIZE = 128
W_GROUP = 512
H_A = 4
DH_A = 64
H_C = 4
DH_C = 128
H_IDX = 8
D_IDX = 64
TOPK_MAX = 256
G_B = 32
P_B = 64
S5_CH = 16
H_D = 4
DK_D = 64
DV_D = 128
GATE_RANK = 16
GATE_TAU = 16.0
GLA_CHUNK = 16
D_FF = 5632
REL_BUCKETS = 32
REL_MAX_EXACT = 16
REL_MAX_DIST = 128
DEEPNORM_ALPHA = (2 * DEPTH) ** 0.25
LN_EPS = 1e-5

N_MAIN = 4096
N_TAIL = 1664
TAIL_MISC = 1536
NSTATE = G_B * P_B

NEG_BIAS = -1e30
INT_MIN = -2 ** 31
KEY_NEG_INF = -2139095041

VMEM_LIMIT_BYTES = 56 * 1024 * 1024
LANES = 128
LOG2E = 1.4426950408889634
FLASH_ROW_BLOCK = 128
S5_GROUP = 8
S5_BLOCK = 256


def _cparams(sem):
    return pltpu.CompilerParams(dimension_semantics=sem, vmem_limit_bytes=VMEM_LIMIT_BYTES)


def _dot(a, b):
    return jnp.dot(a, b, preferred_element_type=F32)


def _dot_nt(a, b):
    return lax.dot_general(a, b, (((1,), (1,)), ((), ())), preferred_element_type=F32)


def _mm_kernel(a_ref, w_ref, o_ref):
    o_ref[...] = _dot(a_ref[...].astype(BF16), w_ref[...].astype(BF16)).astype(o_ref.dtype)


def _proj_main(x, w_in, l, tm):
    M, K = x.shape
    tn = W_GROUP
    return pl.pallas_call(
        _mm_kernel,
        out_shape=jax.ShapeDtypeStruct((N_MAIN // tn, M, tn), F32),
        grid=(M // tm, N_MAIN // tn),
        in_specs=[pl.BlockSpec((tm, K), lambda i, j: (i, 0)),
                  pl.BlockSpec((None, K, tn), lambda i, j: (l, 0, j))],
        out_specs=pl.BlockSpec((None, tm, tn), lambda i, j: (j, i, 0)),
        compiler_params=_cparams(("parallel", "arbitrary")),
    )(x, w_in)


def _proj_tail(x, w_tail, l, tm):
    M, K = x.shape
    return pl.pallas_call(
        _mm_kernel,
        out_shape=jax.ShapeDtypeStruct((M, N_TAIL), F32),
        grid=(M // tm,),
        in_specs=[pl.BlockSpec((tm, K), lambda i: (i, 0)),
                  pl.BlockSpec((None, K, N_TAIL), lambda i: (l, 0, 0))],
        out_specs=pl.BlockSpec((tm, N_TAIL), lambda i: (i, 0)),
        compiler_params=_cparams(("parallel",)),
    )(x, w_tail)


def _mm_ln_kernel(*refs, n_a, n_k):
    a_refs = refs[:n_a]
    w_ref, x_ref, g_ref, b_ref, o32_ref, o16_ref, acc_ref = refs[n_a:]
    k = pl.program_id(1)

    @pl.when(k == 0)
    def _():
        acc_ref[...] = jnp.zeros_like(acc_ref)

    if n_a == 1:
        acc_ref[...] += _dot(a_refs[0][...].astype(BF16), w_ref[...].astype(BF16))
    else:
        for kk in range(n_a):
            @pl.when(k == kk)
            def _(a_ref=a_refs[kk]):
                acc_ref[...] += _dot(a_ref[...].astype(BF16), w_ref[...].astype(BF16))

    @pl.when(k == n_k - 1)
    def _():
        z = DEEPNORM_ALPHA * x_ref[...] + acc_ref[...]
        mu = jnp.mean(z, axis=1, keepdims=True)
        zc = z - mu
        var = jnp.mean(zc * zc, axis=1, keepdims=True)
        y = zc * lax.rsqrt(var + LN_EPS) * g_ref[...] + b_ref[...]
        o32_ref[...] = y
        o16_ref[...] = y.astype(BF16)


def _mm_ln(a_list, w, l, x, g, b, tm, tk):
    M, N = x.shape
    n_a = len(a_list)
    n_k = n_a if n_a > 1 else a_list[0].shape[1] // tk
    if n_a > 1:
        a_specs = [pl.BlockSpec((tm, tk), lambda i, k: (i, 0)) for _ in a_list]
    else:
        a_specs = [pl.BlockSpec((tm, tk), lambda i, k: (i, k))]
    row = pl.BlockSpec((None, 1, N), lambda i, k: (l, 0, 0))
    return pl.pallas_call(
        functools.partial(_mm_ln_kernel, n_a=n_a, n_k=n_k),
        out_shape=(jax.ShapeDtypeStruct((M, N), F32), jax.ShapeDtypeStruct((M, N), BF16)),
        grid=(M // tm, n_k),
        in_specs=a_specs + [pl.BlockSpec((None, tk, N), lambda i, k: (l, k, 0)),
                            pl.BlockSpec((tm, N), lambda i, k: (i, 0)), row, row],
        out_specs=(pl.BlockSpec((tm, N), lambda i, k: (i, 0)),
                   pl.BlockSpec((tm, N), lambda i, k: (i, 0))),
        scratch_shapes=[pltpu.VMEM((tm, N), F32)],
        compiler_params=_cparams(("parallel", "arbitrary")),
    )(*a_list, w, x, g, b)


def _swiglu_kernel(x_ref, wg_ref, wu_ref, o_ref):
    a = x_ref[...].astype(BF16)
    gate = _dot(a, wg_ref[...].astype(BF16))
    up = _dot(a, wu_ref[...].astype(BF16))
    o_ref[...] = (gate * jax.nn.sigmoid(gate) * up).astype(o_ref.dtype)


def _swiglu(x16, wg, wu, l, tm, tn):
    M, K = x16.shape
    return pl.pallas_call(
        _swiglu_kernel,
        out_shape=jax.ShapeDtypeStruct((M, D_FF), BF16),
        grid=(D_FF // tn, M // tm),
        in_specs=[pl.BlockSpec((tm, K), lambda j, i: (i, 0)),
                  pl.BlockSpec((None, K, tn), lambda j, i: (l, 0, j)),
                  pl.BlockSpec((None, K, tn), lambda j, i: (l, 0, j))],
        out_specs=pl.BlockSpec((tm, tn), lambda j, i: (i, j)),
        compiler_params=_cparams(("parallel", "arbitrary")),
    )(x16, wg, wu)


def _flash_kernel(*refs, groups, dk, scale, has_mask, diff, T):
    if has_mask:
        q_ref, k_ref, v_ref, bias_ref, mask_ref, lam_ref, gain_ref, o_ref, m_sc, l_sc, acc_sc = refs
    else:
        q_ref, k_ref, v_ref, bias_ref, lam_ref, gain_ref, o_ref, m_sc, l_sc, acc_sc = refs
    i = pl.program_id(2)
    j = pl.program_id(3)
    RB = FLASH_ROW_BLOCK

    @pl.when(j == 0)
    def _():
        m_sc[...] = jnp.full(m_sc.shape, -jnp.inf, F32)
        l_sc[...] = jnp.zeros_like(l_sc)
        acc_sc[...] = jnp.zeros_like(acc_sc)

    def step(diagonal):
        q = (q_ref[...] * (scale * LOG2E)).astype(BF16)
        k = k_ref[...].astype(BF16)
        v_ext = jnp.concatenate([v_ref[...].astype(BF16), jnp.ones((T, LANES), BF16)], axis=1)
        for r in range(T // RB):
            rows = slice(r * RB, (r + 1) * RB)
            nc = (r + 1) * RB if diagonal else T
            bias = bias_ref[rows, 0:nc]
            if has_mask:
                bias = bias + mask_ref[rows, 0:nc].astype(F32)
            for g in range(groups):
                s = _dot_nt(q[rows, g * dk:(g + 1) * dk], k[0:nc, g * dk:(g + 1) * dk]) + bias
                m_prev = m_sc[g, rows, :]
                m_new = jnp.maximum(m_prev, jnp.max(s, axis=1, keepdims=True))
                p = jnp.exp2(s - jnp.concatenate([m_new] * (nc // LANES), axis=1))
                a = jnp.exp2(m_prev - m_new)
                pv = _dot(p.astype(BF16), v_ext[0:nc])
                acc_sc[g, rows, :] = a * acc_sc[g, rows, :] + pv[:, :LANES]
                l_sc[g, rows, :] = a * l_sc[g, rows, :] + pv[:, LANES:]
                m_sc[g, rows, :] = m_new

    @pl.when(j < i)
    def _():
        step(False)

    @pl.when(j == i)
    def _():
        step(True)
        if diff:
            o = acc_sc[0] / l_sc[0] - lam_ref[...] * (acc_sc[1] / l_sc[1])
            ms = jnp.mean(o * o, axis=1, keepdims=True)
            o = o * lax.rsqrt(ms + LN_EPS) * gain_ref[...]
        else:
            o = acc_sc[0] / l_sc[0]
        o_ref[...] = o


def _flash_prompt(p8, slabs, bias_tab, mask, lam_row, gain_row, *, B, L, T, groups, dk, scale, diff):
    nq = L // T
    H = 4
    has_mask = mask is not None
    sq, sk, sv = slabs
    qspec = pl.BlockSpec((None, T, LANES), lambda b, h, i, j: (sq, b * nq + i, h))
    kspec = pl.BlockSpec((None, T, LANES), lambda b, h, i, j: (sk, b * nq + jnp.minimum(j, i), h))
    vspec = pl.BlockSpec((None, T, LANES), lambda b, h, i, j: (sv, b * nq + jnp.minimum(j, i), h))
    bspec = pl.BlockSpec((None, None, T, T),
                         lambda b, h, i, j: (h, jnp.minimum(jnp.maximum(i - j, 0), 2), 0, 0))
    rowspec = pl.BlockSpec((1, LANES), lambda b, h, i, j: (0, 0))
    in_specs = [qspec, kspec, vspec, bspec]
    args = [p8, p8, p8, bias_tab]
    if has_mask:
        in_specs.append(pl.BlockSpec((T, T), lambda b, h, i, j: (b * nq + i, jnp.minimum(j, i))))
        args.append(mask)
    in_specs += [rowspec, rowspec]
    args += [lam_row, gain_row]
    return pl.pallas_call(
        functools.partial(_flash_kernel, groups=groups, dk=dk, scale=scale, has_mask=has_mask, diff=diff, T=T),
        out_shape=jax.ShapeDtypeStruct((B * L, H * LANES), F32),
        grid=(B, H, nq, nq),
        in_specs=in_specs,
        out_specs=pl.BlockSpec((T, LANES), lambda b, h, i, j: (b * nq + i, h)),
        scratch_shapes=[pltpu.VMEM((groups, T, LANES), F32), pltpu.VMEM((groups, T, LANES), F32),
                        pltpu.VMEM((groups, T, LANES), F32)],
        compiler_params=_cparams(("parallel", "parallel", "parallel", "arbitrary")),
    )(*args)


def _sortable(x):
    i = lax.bitcast_convert_type(x, jnp.int32)
    return jnp.where(i < 0, i ^ jnp.int32(0x7FFFFFFF), i)


def _topk_bias(key_ref, o_ref, topk, n_chunks, cw):
    R = key_ref.shape[0]
    kf = float(topk)
    nl = cw // LANES

    def chunk_at(c):
        return pl.ds(pl.multiple_of(c * cw, LANES), cw)

    def count(pred):
        def body(c, acc):
            hit = jnp.where(pred(key_ref[:, chunk_at(c)]), 1.0, 0.0)
            parts = [hit[:, t * LANES:(t + 1) * LANES] for t in range(nl)]
            while len(parts) > 1:
                parts = [a + b for a, b in zip(parts[::2], parts[1::2])] + parts[len(parts) & ~1:]
            return acc + parts[0]
        acc = lax.fori_loop(0, n_chunks, body, jnp.zeros((R, LANES), F32))
        return jnp.sum(acc, axis=1, keepdims=True)

    base = jnp.where(count(lambda k: k >= 0) >= kf, 0, INT_MIN).astype(jnp.int32)

    def bit_step(it, base):
        cand = base | jnp.left_shift(jnp.int32(1), 30 - it)
        return jnp.where(count(lambda k: k >= cand) >= kf, cand, base)

    base = lax.fori_loop(0, 31, bit_step, base)
    has_excess = jnp.max(count(lambda k: k >= base)) > kf

    @pl.when(jnp.logical_not(has_excess))
    def _():
        def write(c, carry):
            k = key_ref[:, chunk_at(c)]
            sel = (k >= base) & (k > KEY_NEG_INF)
            o_ref[:, chunk_at(c)] = jnp.where(sel, 0.0, NEG_BIAS).astype(o_ref.dtype)
            return carry
        lax.fori_loop(0, n_chunks, write, 0)

    @pl.when(has_excess)
    def _():
        need = kf - count(lambda k: k > base)
        before = (lax.broadcasted_iota(jnp.int32, (LANES, LANES), 0)
                  < lax.broadcasted_iota(jnp.int32, (LANES, LANES), 1))
        before = jnp.where(before, 1.0, 0.0)

        def lane_tile(c, seen):
            sl = pl.ds(pl.multiple_of(c * LANES, LANES), LANES)
            k = key_ref[:, sl]
            eq = k == base
            eqf = jnp.where(eq, 1.0, 0.0)
            rank = _dot(eqf, before) + seen
            sel = ((k > base) | (eq & (rank < need))) & (k > KEY_NEG_INF)
            o_ref[:, sl] = jnp.where(sel, 0.0, NEG_BIAS).astype(o_ref.dtype)
            return seen + jnp.sum(eqf, axis=1, keepdims=True)

        lax.fori_loop(0, n_chunks * nl, lane_tile, jnp.zeros((R, 1), F32))


def _topk_bias_t(key_ref, o_ref, topk, n_chunks, kc):
    Q = key_ref.shape[1]
    kf = float(topk)
    SUB = 8

    def rows_at(c):
        return pl.ds(pl.multiple_of(c * kc, kc), kc)

    def count(pred):
        def body(c, acc):
            hit = jnp.where(pred(key_ref[rows_at(c), :]), 1.0, 0.0).reshape(kc // SUB, SUB, Q)
            n = kc // SUB
            while n > 1:
                n //= 2
                hit = hit[:n] + hit[n:2 * n]
            return acc + hit[0]
        acc = lax.fori_loop(0, n_chunks, body, jnp.zeros((SUB, Q), F32))
        return jnp.sum(acc, axis=0, keepdims=True)

    base = jnp.where(count(lambda k: k >= 0) >= kf, 0, INT_MIN).astype(jnp.int32)

    def bit_step(it, base):
        cand = base | jnp.left_shift(jnp.int32(1), 30 - it)
        return jnp.where(count(lambda k: k >= cand) >= kf, cand, base)

    base = lax.fori_loop(0, 31, bit_step, base)
    has_excess = jnp.max(count(lambda k: k >= base)) > kf

    def store(c, sel):
        bias_t = jnp.where(sel, 0.0, NEG_BIAS)
        o_ref[:, rows_at(c)] = jnp.transpose(bias_t).astype(o_ref.dtype)

    @pl.when(jnp.logical_not(has_excess))
    def _():
        def write(c, carry):
            k = key_ref[rows_at(c), :]
            store(c, (k >= base) & (k > KEY_NEG_INF))
            return carry
        lax.fori_loop(0, n_chunks, write, 0)

    @pl.when(has_excess)
    def _():
        need = kf - count(lambda k: k > base)
        earlier = (lax.broadcasted_iota(jnp.int32, (kc, kc), 1)
                   < lax.broadcasted_iota(jnp.int32, (kc, kc), 0))
        earlier = jnp.where(earlier, 1.0, 0.0)

        def write(c, seen):
            k = key_ref[rows_at(c), :]
            eq = k == base
            eqf = jnp.where(eq, 1.0, 0.0)
            rank = _dot(earlier, eqf) + seen
            store(c, ((k > base) | (eq & (rank < need))) & (k > KEY_NEG_INF))
            return seen + jnp.sum(eqf, axis=0, keepdims=True)

        lax.fori_loop(0, n_chunks, write, jnp.zeros((1, Q), F32))


def _idx_prompt_kernel(iq_ref, ik_ref, iwt_ref, o_ref, key_sc, *, tq, kc, topk):
    i = pl.program_id(1)
    iq = iq_ref[...]
    lhs = jnp.concatenate([iq[:, h * D_IDX:(h + 1) * D_IDX] for h in range(H_IDX)], axis=0).astype(BF16)
    iwt = iwt_ref[...]
    qpos = i * tq + lax.broadcasted_iota(jnp.int32, (1, tq), 1)
    n_need = (i * tq + tq + kc - 1) // kc
    o_ref[...] = jnp.full(o_ref.shape, NEG_BIAS, o_ref.dtype)

    def score_chunk(c, carry):
        off = pl.multiple_of(c * kc, kc)
        ik = ik_ref[pl.ds(off, kc), 0:D_IDX].astype(BF16)
        s = _dot_nt(ik, lhs)
        score = jnp.zeros((kc, tq), F32)
        for h in range(H_IDX):
            score = score + jnp.maximum(s[:, h * tq:(h + 1) * tq], 0.0) * iwt[h:h + 1, :]
        kpos = off + lax.broadcasted_iota(jnp.int32, (kc, 1), 0)
        score = jnp.where(kpos <= qpos, score, -jnp.inf)
        key_sc[pl.ds(off, kc), :] = _sortable(score)
        return carry

    lax.fori_loop(0, n_need, score_chunk, 0)
    _topk_bias_t(key_sc, o_ref, topk, n_need, kc)


def _idx_prompt(p8, tail, *, B, L, tq, topk):
    nq = L // tq
    nmisc = TAIL_MISC // LANES
    return pl.pallas_call(
        functools.partial(_idx_prompt_kernel, tq=tq, kc=min(L, 512), topk=topk),
        out_shape=jax.ShapeDtypeStruct((B * L, L), BF16),
        grid=(B, nq),
        in_specs=[pl.BlockSpec((None, tq, W_GROUP), lambda b, i: (7, b * nq + i, 0)),
                  pl.BlockSpec((L, LANES), lambda b, i: (b, nmisc)),
                  pl.BlockSpec((H_IDX, tq), lambda b, i: (0, b * nq + i))],
        out_specs=pl.BlockSpec((tq, L), lambda b, i: (b * nq + i, 0)),
        scratch_shapes=[pltpu.VMEM((L, tq), jnp.int32)],
        compiler_params=_cparams(("parallel", "arbitrary")),
    )(p8, tail, jnp.transpose(tail[:, TAIL_MISC + D_IDX:TAIL_MISC + D_IDX + H_IDX]))


def _gelu_tanh(x):
    return 0.5 * x * (1.0 + jnp.tanh(math.sqrt(2.0 / math.pi) * (x + 0.044715 * (x * x * x))))


def _s5_kernel(u_ref, h0r_ref, h0i_ref, bre_ref, bim_ref, apr_ref, api_ref, pwr_ref, pwi_ref,
               cre_ref, cim_ref, d_ref, wglu_ref, y_ref, hr_ref, hi_ref, cr_sc, ci_sc, hsr_sc, hsi_sc, *, T):
    t = pl.program_id(1)

    @pl.when(t == 0)
    def _():
        cr_sc[...] = h0r_ref[...]
        ci_sc[...] = h0i_ref[...]

    u = u_ref[...]
    ub = u.astype(BF16)
    G = T // S5_GROUP
    hr = _dot(ub, bre_ref[...]).reshape(G, S5_GROUP, NSTATE)
    hi = _dot(ub, bim_ref[...]).reshape(G, S5_GROUP, NSTATE)
    for j in range(S5_GROUP.bit_length() - 1):
        sr = pltpu.roll(hr, 1 << j, 1)
        si = pltpu.roll(hi, 1 << j, 1)
        ar = apr_ref[j]
        ai = api_ref[j]
        hr, hi = hr + (ar * sr - ai * si), hi + (ar * si + ai * sr)
    cr = cr_sc[...]
    ci = ci_sc[...]
    pr = pwr_ref[...]
    pi_ = pwi_ref[...]
    for g in range(G):
        rows = slice(g * S5_GROUP, (g + 1) * S5_GROUP)
        gr = hr[g] + (pr * cr - pi_ * ci)
        gi = hi[g] + (pr * ci + pi_ * cr)
        hsr_sc[rows, :] = gr
        hsi_sc[rows, :] = gi
        cr = gr[S5_GROUP - 1:S5_GROUP, :]
        ci = gi[S5_GROUP - 1:S5_GROUP, :]
    cr_sc[...] = cr
    ci_sc[...] = ci
    hr_ref[...] = cr
    hi_ref[...] = ci
    y = (_dot(hsr_sc[...].astype(BF16), cre_ref[...]) - _dot(hsi_sc[...].astype(BF16), cim_ref[...])
         + d_ref[...] * u)
    y = _gelu_tanh(y)
    gu = _dot(y.astype(BF16), wglu_ref[...])
    y_ref[...] = gu[:, :W_GROUP] * jax.nn.sigmoid(gu[:, W_GROUP:])


def _s5(p8, h0r, h0i, tabs, l, *, B, L, T):
    nt = L // T
    bre, bim, apr, api, pwr, pwi, cre, cim, dsk, wglu = tabs
    lay = lambda *shape: pl.BlockSpec((None,) + shape, lambda b, t: (l,) + (0,) * len(shape))
    st = pl.BlockSpec((None, 1, NSTATE), lambda b, t: (b, 0, 0))
    return pl.pallas_call(
        functools.partial(_s5_kernel, T=T),
        out_shape=(jax.ShapeDtypeStruct((B * L, W_GROUP), F32),
                   jax.ShapeDtypeStruct((B, 1, NSTATE), F32), jax.ShapeDtypeStruct((B, 1, NSTATE), F32)),
        grid=(B, nt),
        in_specs=[pl.BlockSpec((None, T, W_GROUP), lambda b, t: (3, b * nt + t, 0)), st, st,
                  lay(W_GROUP, NSTATE), lay(W_GROUP, NSTATE), lay(3, S5_GROUP, NSTATE), lay(3, S5_GROUP, NSTATE),
                  lay(S5_GROUP, NSTATE), lay(S5_GROUP, NSTATE), lay(NSTATE, W_GROUP), lay(NSTATE, W_GROUP),
                  lay(1, W_GROUP), lay(W_GROUP, 2 * W_GROUP)],
        out_specs=(pl.BlockSpec((T, W_GROUP), lambda b, t: (b * nt + t, 0)), st, st),
        scratch_shapes=[pltpu.VMEM((1, NSTATE), F32), pltpu.VMEM((1, NSTATE), F32),
                        pltpu.VMEM((T, NSTATE), F32), pltpu.VMEM((T, NSTATE), F32)],
        compiler_params=_cparams(("parallel", "arbitrary")),
    )(p8, h0r, h0i, bre, bim, apr, api, pwr, pwi, cre, cim, dsk, wglu)


def _log_sigmoid(x):
    return jnp.minimum(x, 0.0) - jnp.log1p(jnp.exp(-jnp.abs(x)))


def _gla_kernel(q_ref, k_ref, v_ref, r_ref, misc_ref, wg_ref, bg_ref, ng_ref, s0_ref,
                y_ref, sfin_ref, s_sc, *, TB, CH, nt, valid_len):
    t = pl.program_id(1)
    HK = H_D * DK_D
    HV = H_D * DV_D

    @pl.when(t == 0)
    def _():
        s_sc[...] = jnp.zeros_like(s_sc)
        for h in range(H_D):
            s_sc[h * DK_D:(h + 1) * DK_D, h * DV_D:(h + 1) * DV_D] = s0_ref[h]

    row = lax.broadcasted_iota(jnp.int32, (TB, 1), 0)
    glow = misc_ref[:, D_IDX + H_IDX:D_IDX + H_IDX + GATE_RANK]
    logit = _dot(glow.astype(BF16), wg_ref[...].astype(BF16)) + bg_ref[...]
    la = _log_sigmoid(logit) * (1.0 / GATE_TAU)
    k = k_ref[...]
    if valid_len < TB:
        la = jnp.where(row < valid_len, la, 0.0)
        k = jnp.where(row < valid_len, k, 0.0)
    ri = lax.broadcasted_iota(jnp.int32, (TB, TB), 0)
    ci = lax.broadcasted_iota(jnp.int32, (TB, TB), 1)
    same = (ri // CH) == (ci // CH)
    lower = same & (ci <= ri)
    tri = jnp.where(lower, 1.0, 0.0).astype(BF16)
    blk = jnp.where(same, 1.0, 0.0).astype(BF16)
    la_hi = la.astype(BF16)
    la_lo = (la - la_hi.astype(F32)).astype(BF16)
    bc = _dot(tri, la_hi) + _dot(tri, la_lo)
    bl = _dot(blk, la_hi) + _dot(blk, la_lo)
    q = q_ref[...] * (DK_D ** -0.5)
    v = v_ref[...]
    vb = v.astype(BF16)
    qt = (q * jnp.exp(bc)).astype(BF16)
    kc = (k * jnp.exp(-bc)).astype(BF16)
    kh_t = jnp.transpose(k * jnp.exp(bl - bc))
    dec_t = jnp.transpose(jnp.exp(bl))

    o_parts = []
    for h in range(H_D):
        att = _dot_nt(qt[:, h * DK_D:(h + 1) * DK_D], kc[:, h * DK_D:(h + 1) * DK_D])
        att = jnp.where(lower, att, 0.0)
        o_parts.append(_dot(att.astype(BF16), vb[:, h * DV_D:(h + 1) * DV_D]))
    o_intra = jnp.concatenate(o_parts, axis=1)

    head_blk = (lax.broadcasted_iota(jnp.int32, (HK, HV), 0) // DK_D
                == lax.broadcasted_iota(jnp.int32, (HK, HV), 1) // DV_D)
    col = lax.broadcasted_iota(jnp.int32, (1, TB), 1)
    S = s_sc[...]
    o_rows = []
    for c in range(TB // CH):
        o_rows.append(_dot(qt[c * CH:(c + 1) * CH, :], S.astype(BF16)))
        in_chunk = (col >= c * CH) & (col < (c + 1) * CH)
        kv = _dot(jnp.where(in_chunk, kh_t, 0.0).astype(BF16), vb)
        S = dec_t[:, c * CH:c * CH + 1] * S + jnp.where(head_blk, kv, 0.0)
    s_sc[...] = S
    o = o_intra + jnp.concatenate(o_rows, axis=0)

    outs = []
    for h in range(H_D):
        oh = o[:, h * DV_D:(h + 1) * DV_D]
        ms = jnp.mean(oh * oh, axis=1, keepdims=True)
        outs.append(oh * lax.rsqrt(ms + LN_EPS) * ng_ref[...])
    r = r_ref[...]
    y_ref[...] = jnp.concatenate(outs, axis=1) * (r * jax.nn.sigmoid(r))

    @pl.when(t == nt - 1)
    def _():
        for h in range(H_D):
            sfin_ref[h] = S[h * DK_D:(h + 1) * DK_D, h * DV_D:(h + 1) * DV_D]


def _gla(tail, s0, w_gate, b_gate, norm_g, l, *, B, L, TB, valid_len):
    nt = L // TB
    lay = lambda *shape: pl.BlockSpec((None,) + shape, lambda b, t: (l,) + (0,) * len(shape))
    sspec = pl.BlockSpec((None, H_D, DK_D, DV_D), lambda b, t: (b, 0, 0, 0))
    return pl.pallas_call(
        functools.partial(_gla_kernel, TB=TB, CH=GLA_CHUNK, nt=nt, valid_len=valid_len),
        out_shape=(jax.ShapeDtypeStruct((B * L, W_GROUP), F32),
                   jax.ShapeDtypeStruct((B, H_D, DK_D, DV_D), F32)),
        grid=(B, nt),
        in_specs=[pl.BlockSpec((TB, 256), lambda b, t: (b * nt + t, 0)),
                  pl.BlockSpec((TB, 256), lambda b, t: (b * nt + t, 1)),
                  pl.BlockSpec((TB, 512), lambda b, t: (b * nt + t, 1)),
                  pl.BlockSpec((TB, 512), lambda b, t: (b * nt + t, 2)),
                  pl.BlockSpec((TB, LANES), lambda b, t: (b * nt + t, TAIL_MISC // LANES)),
                  lay(GATE_RANK, H_D * DK_D), lay(1, H_D * DK_D), lay(1, DV_D), sspec],
        out_specs=(pl.BlockSpec((TB, W_GROUP), lambda b, t: (b * nt + t, 0)), sspec),
        scratch_shapes=[pltpu.VMEM((H_D * DK_D, H_D * DV_D), F32)],
        compiler_params=_cparams(("parallel", "arbitrary")),
    )(tail, tail, tail, tail, tail, w_gate, b_gate, norm_g, s0)


def _sidx_kernel(pt_ref, iq_ref, iw_ref, *rest, PPS, NS, n_new, topk):
    page_refs = rest[:PPS]
    new_ref, o_ref, key_sc = rest[PPS:]
    s_id = pl.program_id(1)
    iq = iq_ref[...].astype(BF16)
    iw = iw_ref[...]

    def scores(keys_t):
        hs = jnp.maximum(_dot(iq, keys_t.astype(BF16)), 0.0) * iw
        sc = jnp.zeros((n_new, LANES), F32)
        for h in range(H_IDX):
            sc = sc + hs[h * n_new:(h + 1) * n_new]
        return sc

    for p in range(PPS):
        off = pl.multiple_of((s_id * PPS + p) * LANES, LANES)
        key_sc[:, pl.ds(off, LANES)] = _sortable(scores(page_refs[p][...]))

    @pl.when(s_id == NS - 1)
    def _():
        sc = scores(new_ref[...])
        qi = lax.broadcasted_iota(jnp.int32, (n_new, LANES), 0)
        kj = lax.broadcasted_iota(jnp.int32, (n_new, LANES), 1)
        sc = jnp.where(kj <= qi, sc, -jnp.inf)
        key_sc[:, NS * PPS * LANES:(NS * PPS + 1) * LANES] = _sortable(sc)
        n_tiles = NS * PPS + 1
        n_chunks = 3 if n_tiles % 3 == 0 else 1
        _topk_bias(key_sc, o_ref, topk, n_chunks, n_tiles // n_chunks * LANES)


def _sidx(page_table, iq_rows, iw_tile, cache_idx, idx_new, l, *, PPS, topk):
    Bs, n_pages = page_table.shape
    NS = n_pages // PPS
    n_new = iq_rows.shape[1] // H_IDX
    W = (n_pages + 1) * LANES

    def page_spec(p):
        return pl.BlockSpec((None, None, D_IDX, PAGE_SIZE),
                            lambda b, s, pt: (l, pt[b, s * PPS + p], 0, 0))

    grid_spec = pltpu.PrefetchScalarGridSpec(
        num_scalar_prefetch=1,
        grid=(Bs, NS),
        in_specs=[pl.BlockSpec((None, H_IDX * n_new, D_IDX), lambda b, s, pt: (b, 0, 0)),
                  pl.BlockSpec((None, H_IDX * n_new, LANES), lambda b, s, pt: (b, 0, 0))]
                 + [page_spec(p) for p in range(PPS)]
                 + [pl.BlockSpec((None, D_IDX, PAGE_SIZE), lambda b, s, pt: (b, 0, 0))],
        out_specs=pl.BlockSpec((None, n_new, W), lambda b, s, pt: (b, 0, 0)),
        scratch_shapes=[pltpu.VMEM((n_new, W), jnp.int32)],
    )
    return pl.pallas_call(
        functools.partial(_sidx_kernel, PPS=PPS, NS=NS, n_new=n_new, topk=topk),
        out_shape=jax.ShapeDtypeStruct((Bs, n_new, W), F32),
        grid_spec=grid_spec,
        compiler_params=_cparams(("parallel", "arbitrary")),
    )(page_table, iq_rows, iw_tile, *([cache_idx] * PPS), idx_new)


def _sattn_kernel(pt_ref, q_ref, *rest, PPS, NS, R, n_new, has_mask, diff):
    k_refs = rest[:PPS]
    v_refs = rest[PPS:2 * PPS]
    rest = rest[2 * PPS:]
    knew_ref, vnew_ref, far_ref, near_ref = rest[:4]
    rest = rest[4:]
    if has_mask:
        mask_ref, masknew_ref, spread_ref = rest[:3]
        rest = rest[3:]
    lam_ref, gain_ref, o_ref, m_sc, l_sc, acc_sc = rest
    s_id = pl.program_id(1)

    @pl.when(s_id == 0)
    def _():
        m_sc[...] = jnp.full(m_sc.shape, -jnp.inf, F32)
        l_sc[...] = jnp.zeros_like(l_sc)
        acc_sc[...] = jnp.zeros_like(acc_sc)

    q = q_ref[...].astype(BF16)
    PR = PAGE_SIZE * 4

    def expand(m):
        return jnp.concatenate([_dot(m, spread_ref[...])] * (R // n_new), axis=0)

    def pages(kps, vps, biases):
        ss = [_dot_nt(q, kp.astype(BF16)) + b for kp, b in zip(kps, biases)]
        top = ss[0]
        for s in ss[1:]:
            top = jnp.maximum(top, s)
        m_prev = m_sc[...]
        m_new = jnp.maximum(m_prev, jnp.max(top, axis=1, keepdims=True))
        a = jnp.exp2(m_prev - m_new)
        ps = [jnp.exp2(s - m_new) for s in ss]
        tot = ps[0]
        for p in ps[1:]:
            tot = tot + p
        pv = _dot(ps[0].astype(BF16), vps[0].astype(BF16))
        for p, vp in zip(ps[1:], vps[1:]):
            pv = pv + _dot(p.astype(BF16), vp.astype(BF16))
        l_sc[...] = a * l_sc[...] + jnp.sum(tot, axis=1, keepdims=True)
        acc_sc[...] = a * acc_sc[...] + pv
        m_sc[...] = m_new

    far = far_ref[...]
    biases = []
    for p in range(PPS):
        bias = far
        if p == PPS - 1:
            bias = jnp.where(s_id == NS - 1, near_ref[:, 0:PR], far)
        if has_mask:
            bias = bias + expand(mask_ref[:, p * LANES:(p + 1) * LANES])
        biases.append(bias)
    pages([r[...] for r in k_refs], [r[...] for r in v_refs], biases)

    @pl.when(s_id == NS - 1)
    def _():
        bias = near_ref[:, PR:2 * PR]
        if has_mask:
            bias = bias + expand(masknew_ref[...])
        pages([knew_ref[...]], [vnew_ref[...]], [bias])
        o_all = acc_sc[...] / l_sc[...]
        outs = []
        for h in range(4):
            if diff:
                o1 = o_all[(2 * h) * n_new:(2 * h + 1) * n_new]
                o2 = o_all[(2 * h + 1) * n_new:(2 * h + 2) * n_new]
                o = o1 - lam_ref[...] * o2
                ms = jnp.mean(o * o, axis=1, keepdims=True)
                outs.append(o * lax.rsqrt(ms + LN_EPS) * gain_ref[...])
            else:
                outs.append(o_all[h * n_new:(h + 1) * n_new])
        o_ref[...] = jnp.concatenate(outs, axis=1)


def _sattn(page_table, q_rows, cache_k, cache_v, knew, vnew, far, near, mask, lam_row, gain_row, l,
           *, PPS, diff):
    Bs, n_pages = page_table.shape
    NS = n_pages // PPS
    R = q_rows.shape[1]
    n_new = 8
    PR = PAGE_SIZE * 4
    has_mask = mask is not None

    def page_spec(p):
        return pl.BlockSpec((None, None, PR, LANES), lambda b, s, pt: (l, pt[b, s * PPS + p], 0, 0))

    per_b = lambda *shape: pl.BlockSpec((None,) + shape, lambda b, s, pt: (b,) + (0,) * len(shape))
    const = lambda *shape: pl.BlockSpec(shape, lambda b, s, pt: (0,) * len(shape))
    in_specs = ([per_b(R, LANES)] + [page_spec(p) for p in range(PPS)] * 2
                + [per_b(PR, LANES), per_b(PR, LANES), const(R, PR), const(R, 2 * PR)])
    args = [q_rows] + [cache_k] * PPS + [cache_v] * PPS + [knew, vnew, far, near]
    if has_mask:
        spread = (jnp.arange(PR, dtype=jnp.int32)[None, :] // 4
                  == jnp.arange(PAGE_SIZE, dtype=jnp.int32)[:, None]).astype(F32)
        in_specs += [pl.BlockSpec((None, n_new, PPS * LANES), lambda b, s, pt: (b, 0, s)),
                     pl.BlockSpec((None, n_new, LANES), lambda b, s, pt: (b, 0, n_pages)),
                     const(PAGE_SIZE, PR)]
        args += [mask, mask, spread]
    in_specs += [const(1, LANES), const(1, LANES)]
    args += [lam_row, gain_row]
    grid_spec = pltpu.PrefetchScalarGridSpec(
        num_scalar_prefetch=1,
        grid=(Bs, NS),
        in_specs=in_specs,
        out_specs=pl.BlockSpec((None, n_new, W_GROUP), lambda b, s, pt: (b, 0, 0)),
        scratch_shapes=[pltpu.VMEM((R, 1), F32), pltpu.VMEM((R, 1), F32), pltpu.VMEM((R, LANES), F32)],
    )
    return pl.pallas_call(
        functools.partial(_sattn_kernel, PPS=PPS, NS=NS, R=R, n_new=n_new,
                          has_mask=has_mask, diff=diff),
        out_shape=jax.ShapeDtypeStruct((Bs, n_new, W_GROUP), F32),
        grid_spec=grid_spec,
        compiler_params=_cparams(("parallel", "arbitrary")),
    )(page_table, *args)


def _rel_bucket(dist):
    n = jnp.maximum(dist, 0)
    nf = jnp.maximum(n, REL_MAX_EXACT).astype(F32)
    large = REL_MAX_EXACT + (jnp.log(nf / REL_MAX_EXACT) / math.log(REL_MAX_DIST / REL_MAX_EXACT)
                             * (REL_BUCKETS - REL_MAX_EXACT)).astype(jnp.int32)
    large = jnp.minimum(large, REL_BUCKETS - 1)
    return jnp.where(n < REL_MAX_EXACT, n, large)


def _rel_lookup(rel, bucket):
    out = jnp.zeros((rel.shape[1],) + bucket.shape, F32)
    for b in range(REL_BUCKETS):
        out = jnp.where(bucket[None] == b, rel[b].reshape((-1,) + (1,) * bucket.ndim), out)
    return out * LOG2E


def _prompt_bias_tables(rel, T):
    assert T >= REL_MAX_DIST
    r = jnp.arange(T, dtype=jnp.int32)[:, None]
    c = jnp.arange(T, dtype=jnp.int32)[None, :]
    d0 = r - c
    t0 = jnp.where((d0 >= 0)[None], _rel_lookup(rel, _rel_bucket(d0)), NEG_BIAS)
    t1 = _rel_lookup(rel, _rel_bucket(T + r - c))
    t2 = _rel_lookup(rel, _rel_bucket(jnp.full((T, T), 2 * T, jnp.int32)))
    return jnp.stack([t0, t1, t2], axis=1)


def _sample_bias_tables(rel, n_new, reps):
    H = rel.shape[1]
    t = jnp.arange(n_new, dtype=jnp.int32)[:, None]
    c = jnp.arange(PAGE_SIZE, dtype=jnp.int32)[None, :]
    last = _rel_lookup(rel, _rel_bucket(PAGE_SIZE + t - c))
    dn = t - c
    new = jnp.where(((dn >= 0) & (c < n_new))[None], _rel_lookup(rel, _rel_bucket(dn)), NEG_BIAS)
    far = _rel_lookup(rel, _rel_bucket(jnp.full((n_new, PAGE_SIZE), 2 * PAGE_SIZE, jnp.int32)))

    def rows(x):
        own = jnp.eye(H, dtype=bool)[:, None, None, :]
        y = jnp.where(own, x[..., None], NEG_BIAS).reshape(H, 1, n_new, PAGE_SIZE * H)
        return jnp.broadcast_to(y, (H, reps, n_new, PAGE_SIZE * H)).reshape(H * reps * n_new, PAGE_SIZE * H)

    return rows(far), jnp.concatenate([rows(last), rows(new)], axis=1)


def _s5_tables(a_re, a_im, log_dt, b_re, b_im, c_re, c_im, d_skip, w_glu):
    dt = jnp.exp(log_dt.astype(F32))[..., None]
    lam_re, lam_im = a_re.astype(F32), a_im.astype(F32)
    z_re, z_im = lam_re * dt, lam_im * dt
    mag = jnp.exp(z_re)
    e_re, e_im = mag * jnp.cos(z_im), mag * jnp.sin(z_im)
    den = lam_re * lam_re + lam_im * lam_im
    f_re = ((e_re - 1.0) * lam_re + e_im * lam_im) / den
    f_im = (e_im * lam_re - (e_re - 1.0) * lam_im) / den
    br, bi = b_re.astype(F32), b_im.astype(F32)
    bb_re = f_re[..., None] * br - f_im[..., None] * bi
    bb_im = f_re[..., None] * bi + f_im[..., None] * br
    eye = jnp.eye(G_B, dtype=F32)
    bd_in = lambda bb: jnp.einsum('lgpc,gh->lgchp', bb, eye).reshape(DEPTH, W_GROUP, NSTATE).astype(BF16)
    bd_out = lambda cc: jnp.einsum('lgcp,gh->lgphc', cc.astype(F32), eye).reshape(DEPTH, NSTATE, W_GROUP).astype(BF16)

    def powers(ts):
        tt = ts[None, :, None, None]
        pm = jnp.exp(z_re[:, None] * tt)
        return ((pm * jnp.cos(z_im[:, None] * tt)).reshape(DEPTH, -1, NSTATE),
                (pm * jnp.sin(z_im[:, None] * tt)).reshape(DEPTH, -1, NSTATE))

    n_steps = S5_GROUP.bit_length() - 1
    apr, api = powers(jnp.asarray([2.0 ** j for j in range(n_steps)], F32))
    live = (jnp.arange(S5_GROUP)[None, :] >= (2 ** jnp.arange(n_steps))[:, None]).astype(F32)
    apr = apr[:, :, None, :] * live[None, :, :, None]
    api = api[:, :, None, :] * live[None, :, :, None]
    pwr, pwi = powers(jnp.arange(1, S5_GROUP + 1, dtype=F32))
    return (bd_in(bb_re), bd_in(bb_im), apr, api, pwr, pwi, bd_out(c_re), bd_out(c_im),
            d_skip.astype(F32).reshape(DEPTH, 1, W_GROUP), w_glu.astype(BF16))


def _tail_weights(w_in):
    pad = jnp.zeros(w_in.shape[:2] + (N_TAIL - 1624,), w_in.dtype)
    return jnp.concatenate([w_in[..., 4168:5704], w_in[..., 4096:4168], w_in[..., 5704:5720], pad],
                           axis=-1).astype(BF16)


def _channel_mix(ys, x, l, W, tm):
    x1, x1b = _mm_ln(ys, W['w_out'], l, x, W['ln1_g'], W['ln1_b'], tm, W_GROUP)
    hid = _swiglu(x1b, W['ffn_w_gate'], W['ffn_w_up'], l, tm, 512)
    return _mm_ln([hid], W['ffn_w_down'], l, x1, W['ln2_g'], W['ln2_b'], tm, 512)


def _prompt_layer(x, xb, l, W, B, L):
    T_ATT = 512
    p8 = _proj_main(xb, W['w_in'], l, 1024)
    tail = _proj_tail(xb, W['w_tail'], l, 512)
    y_a = _flash_prompt(p8, (0, 1, 2), W['tab_a'], None, W['lam_row'][l], W['gain_a'][l],
                        B=B, L=L, T=T_ATT, groups=2, dk=DH_A, scale=DH_A ** -0.5, diff=True)
    zeros_h = jnp.zeros((B, 1, NSTATE), F32)
    y_b, h_re, h_im = _s5(p8, zeros_h, zeros_h, W['s5'], l, B=B, L=L, T=S5_BLOCK)
    mask = _idx_prompt(p8, tail, B=B, L=L, tq=128, topk=min(TOPK_MAX, L // 4))
    y_c = _flash_prompt(p8, (4, 5, 6), W['tab_c'], mask, W['lam_row'][l], W['gain_a'][l],
                        B=B, L=L, T=T_ATT, groups=1, dk=DH_C, scale=DH_C ** -0.5, diff=False)
    y_d, s_fin = _gla(tail, jnp.zeros((B, H_D, DK_D, DV_D), F32), W['gla_w_gate'], W['gla_b_gate'],
                      W['gla_norm'], l, B=B, L=L, TB=128, valid_len=128)
    x, xb = _channel_mix([y_a, y_b, y_c, y_d], x, l, W, 512)
    news = (p8[1].reshape(B, L, H_A, 128), p8[2].reshape(B, L, H_A, 128),
            p8[5].reshape(B, L, H_C, 128), p8[6].reshape(B, L, H_C, 128),
            tail[:, TAIL_MISC:TAIL_MISC + D_IDX].reshape(B, L, D_IDX),
            h_re.reshape(B, G_B, P_B), h_im.reshape(B, G_B, P_B), s_fin)
    return x, xb, news


def _sample_layer(x, xb, l, W, C, Bs, Ls):
    M = Bs * Ls
    PPS = 8
    p8 = _proj_main(xb, W['w_in'], l, M)
    tail = _proj_tail(xb, W['w_tail'], l, M)
    pt = C['page_table']

    def pad_new(z):
        z = z.reshape(Bs, Ls, z.shape[-1])
        return jnp.pad(z, ((0, 0), (0, PAGE_SIZE - Ls), (0, 0)))

    def pad_rows(z):
        z = z.reshape(Bs, Ls * 4, LANES)
        return jnp.pad(z, ((0, 0), (0, PAGE_SIZE * 4 - Ls * 4), (0, 0)))

    qa = p8[0].reshape(Bs, Ls, H_A, 2, DH_A) * (DH_A ** -0.5 * LOG2E)
    q_a = jnp.einsum('bqhwd,wv->bhwqvd', qa, jnp.eye(2, dtype=F32)).reshape(Bs, H_A * 2 * Ls, LANES)
    y_a = _sattn(pt, q_a, C['a_k'], C['a_v'], pad_rows(p8[1]), pad_rows(p8[2]), W['far_a'], W['near_a'],
                 None, W['lam_row'][l], W['gain_a'][l], l, PPS=PPS, diff=True)
    y_b, h_re, h_im = _s5(p8, C['b_re'][l].reshape(Bs, 1, NSTATE), C['b_im'][l].reshape(Bs, 1, NSTATE),
                          W['s5'], l, B=Bs, L=Ls, T=Ls)
    misc = tail[:, TAIL_MISC:]
    iq_rows = p8[7].reshape(Bs, Ls, H_IDX, D_IDX).transpose(0, 2, 1, 3).reshape(Bs, H_IDX * Ls, D_IDX)
    iw = misc[:, D_IDX:D_IDX + H_IDX].reshape(Bs, Ls, H_IDX).transpose(0, 2, 1).reshape(Bs, H_IDX * Ls, 1)
    iw_tile = jnp.broadcast_to(iw, (Bs, H_IDX * Ls, LANES))
    mask = _sidx(pt, iq_rows, iw_tile, C['c_idx_t'], pad_new(misc[:, :D_IDX]).transpose(0, 2, 1), l, PPS=PPS,
                 topk=min(TOPK_MAX, (pt.shape[1] * PAGE_SIZE + Ls) // 4))
    q_c = (p8[4].reshape(Bs, Ls, H_C, DH_C) * (DH_C ** -0.5 * LOG2E)).transpose(0, 2, 1, 3)
    y_c = _sattn(pt, q_c.reshape(Bs, H_C * Ls, LANES), C['c_k'], C['c_v'], pad_rows(p8[5]), pad_rows(p8[6]),
                 W['far_c'], W['near_c'], mask, W['lam_row'][l], W['gain_a'][l], l, PPS=PPS, diff=False)
    tail_pad = pad_new(tail).reshape(Bs * PAGE_SIZE, N_TAIL)
    y_d, s_fin = _gla(tail_pad, C['d'][l], W['gla_w_gate'], W['gla_b_gate'], W['gla_norm'], l,
                      B=Bs, L=PAGE_SIZE, TB=PAGE_SIZE, valid_len=Ls)
    y_d = y_d.reshape(Bs, PAGE_SIZE, W_GROUP)[:, :Ls].reshape(M, W_GROUP)
    x, xb = _channel_mix([y_a.reshape(M, W_GROUP), y_b, y_c.reshape(M, W_GROUP), y_d], x, l, W, M)
    news = (p8[1].reshape(Bs, Ls, H_A, 128), p8[2].reshape(Bs, Ls, H_A, 128),
            p8[5].reshape(Bs, Ls, H_C, 128), p8[6].reshape(Bs, Ls, H_C, 128),
            misc[:, :D_IDX].reshape(Bs, Ls, D_IDX),
            h_re.reshape(Bs, G_B, P_B), h_im.reshape(Bs, G_B, P_B), s_fin)
    return x, xb, news


def kernel(x_prompt, x_sample, cache_a_k, cache_a_v, cache_c_k, cache_c_v, cache_c_idx, state_b_re, state_b_im, state_d, page_table, rel_bias, w_in, w_out, lam_q1, lam_k1, lam_q2, lam_k2, a_subln, s5_a_re, s5_a_im, s5_log_dt, s5_b_re, s5_b_im, s5_c_re, s5_c_im, s5_d, s5_w_glu, gla_w_gate, gla_b_gate, gla_norm, ln1_g, ln1_b, ffn_w_gate, ffn_w_up, ffn_w_down, ln2_g, ln2_b):
    B, L, _ = x_prompt.shape
    Bs, Ls, _ = x_sample.shape
    n_pool = cache_a_k.shape[1]

    lam_init = jnp.asarray([0.8 - 0.6 * math.exp(-0.3 * l) for l in range(DEPTH)], F32)
    lam = (jnp.exp(jnp.sum(lam_q1.astype(F32) * lam_k1.astype(F32), axis=-1))
           - jnp.exp(jnp.sum(lam_q2.astype(F32) * lam_k2.astype(F32), axis=-1)) + lam_init)
    s5_args = (s5_a_re, s5_a_im, s5_log_dt, s5_b_re, s5_b_im, s5_c_re, s5_c_im, s5_d, s5_w_glu)
    rel = rel_bias.astype(F32)
    far_a, near_a = _sample_bias_tables(rel[:, :H_A], Ls, 2)
    far_c, near_c = _sample_bias_tables(rel[:, H_A:], Ls, 1)
    W = {
        'w_in': w_in, 'w_tail': _tail_weights(w_in), 'w_out': w_out.astype(BF16),
        'ffn_w_gate': ffn_w_gate, 'ffn_w_up': ffn_w_up, 'ffn_w_down': ffn_w_down.astype(BF16),
        'ln1_g': ln1_g.reshape(DEPTH, 1, D_MODEL), 'ln1_b': ln1_b.reshape(DEPTH, 1, D_MODEL),
        'ln2_g': ln2_g.reshape(DEPTH, 1, D_MODEL), 'ln2_b': ln2_b.reshape(DEPTH, 1, D_MODEL),
        'lam_row': jnp.broadcast_to(lam[:, None, None], (DEPTH, 1, LANES)),
        'gain_a': (a_subln.astype(F32) * (1.0 - lam_init)[:, None]).reshape(DEPTH, 1, LANES),
        'tab_a': _prompt_bias_tables(rel[:, :H_A], 512), 'tab_c': _prompt_bias_tables(rel[:, H_A:], 512),
        'far_a': far_a, 'near_a': near_a, 'far_c': far_c, 'near_c': near_c,
        's5': _s5_tables(*s5_args),
        'gla_w_gate': gla_w_gate, 'gla_b_gate': gla_b_gate.reshape(DEPTH, 1, H_D * DK_D),
        'gla_norm': gla_norm.reshape(DEPTH, 1, DV_D),
    }
    C = {
        'page_table': page_table,
        'a_k': cache_a_k.reshape(DEPTH, n_pool, PAGE_SIZE * H_A, LANES),
        'a_v': cache_a_v.reshape(DEPTH, n_pool, PAGE_SIZE * H_A, LANES),
        'c_k': cache_c_k.reshape(DEPTH, n_pool, PAGE_SIZE * H_C, LANES),
        'c_v': cache_c_v.reshape(DEPTH, n_pool, PAGE_SIZE * H_C, LANES),
        'c_idx_t': jnp.swapaxes(cache_c_idx, 2, 3),
        'b_re': state_b_re, 'b_im': state_b_im, 'd': state_d,
    }

    xp = x_prompt.reshape(B * L, D_MODEL)
    xs = x_sample.reshape(Bs * Ls, D_MODEL)
    xpb, xsb = xp, xs
    news_p, news_s = [], []
    for l in range(DEPTH):
        xp, xpb, n_p = _prompt_layer(xp, xpb, l, W, B, L)
        xs, xsb, n_s = _sample_layer(xs, xsb, l, W, C, Bs, Ls)
        news_p.append(n_p)
        news_s.append(n_s)
    stack = lambda news: [jnp.stack(z, axis=0) for z in zip(*news)]
    return (xp.reshape(B, L, D_MODEL), xs.reshape(Bs, Ls, D_MODEL), *stack(news_p), *stack(news_s))
```

```python
import functools
import math

import jax
import jax.numpy as jnp
from jax import lax
from jax.experimental import pallas as pl
from jax.experimental.pallas import tpu as pltpu

F32 = jnp.float32
BF16 = jnp.bfloat16

D_MODEL = 2048
DEPTH = 4
PAST_LEN = 16384
PAGE_SIZE = 128
W_GROUP = 512
H_A = 4
DH_A = 64
H_C = 4
DH_C = 128
H_IDX = 8
D_IDX = 64
TOPK_MAX = 256
G_B = 32
P_B = 64
S5_CH = 16
H_D = 4
DK_D = 64
DV_D = 128
GATE_RANK = 16
GATE_TAU = 16.0
GLA_CHUNK = 16
D_FF = 5632
REL_BUCKETS = 32
REL_MAX_EXACT = 16
REL_MAX_DIST = 128
DEEPNORM_ALPHA = (2 * DEPTH) ** 0.25
LN_EPS = 1e-5

N_MAIN = 4096
N_TAIL = 1664
TAIL_MISC = 1536
NSTATE = G_B * P_B

NEG_BIAS = -1e30
INT_MIN = -2 ** 31
KEY_NEG_INF = -2139095041

VMEM_LIMIT_BYTES = 56 * 1024 * 1024
LANES = 128
LOG2E = 1.4426950408889634
FLASH_ROW_BLOCK = 256
S5_GROUP = 8
S5_BLOCK = 256


def _cparams(sem):
    return pltpu.CompilerParams(dimension_semantics=sem, vmem_limit_bytes=VMEM_LIMIT_BYTES)


def _dot(a, b):
    return jnp.dot(a, b, preferred_element_type=F32)


def _dot_nt(a, b):
    return lax.dot_general(a, b, (((1,), (1,)), ((), ())), preferred_element_type=F32)


KV_SLABS = (1, 2, 5, 6)


def _proj_main_kernel(a_ref, w_ref, o_ref, *kv_refs, tm):
    j = pl.program_id(1)
    res = _dot_nt(a_ref[...].astype(BF16), w_ref[...].astype(BF16))
    o_ref[...] = res
    for slab, kv_ref in zip(KV_SLABS, kv_refs):
        @pl.when(j == slab)
        def _(kv_ref=kv_ref):
            for h in range(4):
                kv_ref[pl.ds(h, tm, stride=4), :] = res[:, h * LANES:(h + 1) * LANES]


def _proj_main(x, w_in_t, l, tm):
    M, K = x.shape
    tn = W_GROUP
    kv_shape = jax.ShapeDtypeStruct((M * 4, LANES), F32)
    kv_spec = pl.BlockSpec((tm * 4, LANES), lambda i, j: (i, 0))
    return pl.pallas_call(
        functools.partial(_proj_main_kernel, tm=tm),
        out_shape=(jax.ShapeDtypeStruct((N_MAIN // tn, M, tn), F32),) + (kv_shape,) * 4,
        grid=(M // tm, N_MAIN // tn),
        in_specs=[pl.BlockSpec((tm, K), lambda i, j: (i, 0)),
                  pl.BlockSpec((None, tn, K), lambda i, j: (l, j, 0))],
        out_specs=(pl.BlockSpec((None, tm, tn), lambda i, j: (j, i, 0)),) + (kv_spec,) * 4,
        compiler_params=_cparams(("parallel", "arbitrary")),
    )(x, w_in_t)


def _proj_tail_kernel(a_ref, w_ref, o_ref):
    o_ref[...] = _dot_nt(a_ref[...].astype(BF16), w_ref[...])


def _proj_tail(x, w_tail_t, l, tm):
    M, K = x.shape
    return pl.pallas_call(
        _proj_tail_kernel,
        out_shape=jax.ShapeDtypeStruct((M, N_TAIL), F32),
        grid=(M // tm,),
        in_specs=[pl.BlockSpec((tm, K), lambda i: (i, 0)),
                  pl.BlockSpec((None, N_TAIL, K), lambda i: (l, 0, 0))],
        out_specs=pl.BlockSpec((tm, N_TAIL), lambda i: (i, 0)),
        compiler_params=_cparams(("parallel",)),
    )(x, w_tail_t)


def _deepnorm_ln(x_ref, acc, g_ref, b_ref, o32_ref, o16_ref):
    z = DEEPNORM_ALPHA * x_ref[...] + acc
    mu = jnp.mean(z, axis=1, keepdims=True)
    zc = z - mu
    var = jnp.mean(zc * zc, axis=1, keepdims=True)
    y = zc * lax.rsqrt(var + LN_EPS) * g_ref[...] + b_ref[...]
    o32_ref[...] = y
    o16_ref[...] = y.astype(BF16)


def _mix_ln_kernel(*refs, n_a, tk):
    a_refs = refs[:n_a]
    w_ref, x_ref, g_ref, b_ref, o32_ref, o16_ref = refs[n_a:]
    acc = _dot(a_refs[0][...].astype(BF16), w_ref[0:tk, :])
    for k in range(1, n_a):
        acc = acc + _dot(a_refs[k][...].astype(BF16), w_ref[k * tk:(k + 1) * tk, :])
    _deepnorm_ln(x_ref, acc, g_ref, b_ref, o32_ref, o16_ref)


def _mix_ln(a_list, w16, l, x, g, b, tm):
    M, N = x.shape
    tk = a_list[0].shape[1]
    K = tk * len(a_list)
    row = pl.BlockSpec((None, 1, N), lambda i: (l, 0, 0))
    tile = pl.BlockSpec((tm, N), lambda i: (i, 0))
    return pl.pallas_call(
        functools.partial(_mix_ln_kernel, n_a=len(a_list), tk=tk),
        out_shape=(jax.ShapeDtypeStruct((M, N), F32), jax.ShapeDtypeStruct((M, N), BF16)),
        grid=(M // tm,),
        in_specs=[pl.BlockSpec((tm, tk), lambda i: (i, 0)) for _ in a_list]
                 + [pl.BlockSpec((None, K, N), lambda i: (l, 0, 0)), tile, row, row],
        out_specs=(tile, tile),
        compiler_params=_cparams(("parallel",)),
    )(*a_list, w16, x, g, b)


def _mm_ln_kernel(a_ref, w_ref, x_ref, g_ref, b_ref, o32_ref, o16_ref, acc_ref, *, n_k):
    k = pl.program_id(1)

    @pl.when(k == 0)
    def _():
        acc_ref[...] = jnp.zeros_like(acc_ref)

    acc_ref[...] += _dot(a_ref[...], w_ref[...])

    @pl.when(k == n_k - 1)
    def _():
        _deepnorm_ln(x_ref, acc_ref[...], g_ref, b_ref, o32_ref, o16_ref)


def _mm_ln(a16, w16, l, x, g, b, tm, tk):
    M, N = x.shape
    n_k = a16.shape[1] // tk
    row = pl.BlockSpec((None, 1, N), lambda i, k: (l, 0, 0))
    tile = pl.BlockSpec((tm, N), lambda i, k: (i, 0))
    return pl.pallas_call(
        functools.partial(_mm_ln_kernel, n_k=n_k),
        out_shape=(jax.ShapeDtypeStruct((M, N), F32), jax.ShapeDtypeStruct((M, N), BF16)),
        grid=(M // tm, n_k),
        in_specs=[pl.BlockSpec((tm, tk), lambda i, k: (i, k)),
                  pl.BlockSpec((None, tk, N), lambda i, k: (l, k, 0)), tile, row, row],
        out_specs=(tile, tile),
        scratch_shapes=[pltpu.VMEM((tm, N), F32)],
        compiler_params=_cparams(("parallel", "arbitrary")),
    )(a16, w16, x, g, b)


def _swiglu_kernel(x_ref, wg_ref, wu_ref, o_ref):
    a = x_ref[...].astype(BF16)
    gate = _dot(a, wg_ref[...].astype(BF16))
    up = _dot(a, wu_ref[...].astype(BF16))
    o_ref[...] = (gate * jax.nn.sigmoid(gate) * up).astype(o_ref.dtype)


def _swiglu(x16, wg, wu, l, tm, tn):
    M, K = x16.shape
    return pl.pallas_call(
        _swiglu_kernel,
        out_shape=jax.ShapeDtypeStruct((M, D_FF), BF16),
        grid=(D_FF // tn, M // tm),
        in_specs=[pl.BlockSpec((tm, K), lambda j, i: (i, 0)),
                  pl.BlockSpec((None, K, tn), lambda j, i: (l, 0, j)),
                  pl.BlockSpec((None, K, tn), lambda j, i: (l, 0, j))],
        out_specs=pl.BlockSpec((tm, tn), lambda j, i: (i, j)),
        compiler_params=_cparams(("parallel", "arbitrary")),
    )(x16, wg, wu)


def _flash_kernel(*refs, groups, dk, scale, has_mask, diff, T):
    if has_mask:
        q_ref, k_ref, v_ref, bias_ref, mask_ref, lam_ref, gain_ref, o_ref, m_sc, l_sc, acc_sc = refs
    else:
        q_ref, k_ref, v_ref, bias_ref, lam_ref, gain_ref, o_ref, m_sc, l_sc, acc_sc = refs
    i = pl.program_id(2)
    j = pl.program_id(3)
    RB = FLASH_ROW_BLOCK

    @pl.when(j == 0)
    def _():
        m_sc[...] = jnp.full(m_sc.shape, -jnp.inf, F32)
        l_sc[...] = jnp.zeros_like(l_sc)
        acc_sc[...] = jnp.zeros_like(acc_sc)

    def step(diagonal):
        q = (q_ref[...] * (scale * LOG2E)).astype(BF16)
        k = k_ref[...].astype(BF16)
        v_ext = jnp.concatenate([v_ref[...].astype(BF16), jnp.ones((T, LANES), BF16)], axis=1)
        for r in range(T // RB):
            rows = slice(r * RB, (r + 1) * RB)
            nc = (r + 1) * RB if diagonal else T
            bias = bias_ref[rows, 0:nc]
            if has_mask:
                bias = bias + mask_ref[rows, 0:nc].astype(F32)
            for g in range(groups):
                s = _dot_nt(q[rows, g * dk:(g + 1) * dk], k[0:nc, g * dk:(g + 1) * dk]) + bias
                m_prev = m_sc[g, rows, :]
                m_new = jnp.maximum(m_prev, jnp.max(s, axis=1, keepdims=True))
                p = jnp.exp2(s - jnp.concatenate([m_new] * (nc // LANES), axis=1))
                a = jnp.exp2(m_prev - m_new)
                pv = _dot(p.astype(BF16), v_ext[0:nc])
                acc_sc[g, rows, :] = a * acc_sc[g, rows, :] + pv[:, :LANES]
                l_sc[g, rows, :] = a * l_sc[g, rows, :] + pv[:, LANES:]
                m_sc[g, rows, :] = m_new

    @pl.when(j < i)
    def _():
        step(False)

    @pl.when(j == i)
    def _():
        step(True)
        if diff:
            o = acc_sc[0] / l_sc[0] - lam_ref[...] * (acc_sc[1] / l_sc[1])
            ms = jnp.mean(o * o, axis=1, keepdims=True)
            o = o * lax.rsqrt(ms + LN_EPS) * gain_ref[...]
        else:
            o = acc_sc[0] / l_sc[0]
        o_ref[...] = o


def _flash_prompt(p8, slabs, bias_tab, mask, lam_row, gain_row, *, B, L, T, groups, dk, scale, diff):
    nq = L // T
    H = 4
    has_mask = mask is not None
    sq, sk, sv = slabs
    qspec = pl.BlockSpec((None, T, LANES), lambda b, h, i, j: (sq, b * nq + i, h))
    kspec = pl.BlockSpec((None, T, LANES), lambda b, h, i, j: (sk, b * nq + jnp.minimum(j, i), h))
    vspec = pl.BlockSpec((None, T, LANES), lambda b, h, i, j: (sv, b * nq + jnp.minimum(j, i), h))
    bspec = pl.BlockSpec((None, None, T, T),
                         lambda b, h, i, j: (h, jnp.minimum(jnp.maximum(i - j, 0), 2), 0, 0))
    rowspec = pl.BlockSpec((1, LANES), lambda b, h, i, j: (0, 0))
    in_specs = [qspec, kspec, vspec, bspec]
    args = [p8, p8, p8, bias_tab]
    if has_mask:
        in_specs.append(pl.BlockSpec((T, T), lambda b, h, i, j: (b * nq + i, jnp.minimum(j, i))))
        args.append(mask)
    in_specs += [rowspec, rowspec]
    args += [lam_row, gain_row]
    return pl.pallas_call(
        functools.partial(_flash_kernel, groups=groups, dk=dk, scale=scale, has_mask=has_mask, diff=diff, T=T),
        out_shape=jax.ShapeDtypeStruct((B * L, H * LANES), F32),
        grid=(B, H, nq, nq),
        in_specs=in_specs,
        out_specs=pl.BlockSpec((T, LANES), lambda b, h, i, j: (b * nq + i, h)),
        scratch_shapes=[pltpu.VMEM((groups, T, LANES), F32), pltpu.VMEM((groups, T, LANES), F32),
                        pltpu.VMEM((groups, T, LANES), F32)],
        compiler_params=_cparams(("parallel", "parallel", "parallel", "arbitrary")),
    )(*args)


def _sortable(x):
    i = lax.bitcast_convert_type(x, jnp.int32)
    return jnp.where(i < 0, i ^ jnp.int32(0x7FFFFFFF), i)


def _topk_bias(key_ref, o_ref, topk, n_chunks, cw):
    R = key_ref.shape[0]
    kf = float(topk)
    nl = cw // LANES

    def chunk_at(c):
        return pl.ds(pl.multiple_of(c * cw, LANES), cw)

    def count(pred):
        def body(c, acc):
            hit = jnp.where(pred(key_ref[:, chunk_at(c)]), 1.0, 0.0)
            parts = [hit[:, t * LANES:(t + 1) * LANES] for t in range(nl)]
            while len(parts) > 1:
                parts = [a + b for a, b in zip(parts[::2], parts[1::2])] + parts[len(parts) & ~1:]
            return acc + parts[0]
        acc = lax.fori_loop(0, n_chunks, body, jnp.zeros((R, LANES), F32))
        return jnp.sum(acc, axis=1, keepdims=True)

    base = jnp.where(count(lambda k: k >= 0) >= kf, 0, INT_MIN).astype(jnp.int32)

    def bit_step(it, base):
        cand = base | jnp.left_shift(jnp.int32(1), 30 - it)
        return jnp.where(count(lambda k: k >= cand) >= kf, cand, base)

    base = lax.fori_loop(0, 31, bit_step, base)
    has_excess = jnp.max(count(lambda k: k >= base)) > kf

    @pl.when(jnp.logical_not(has_excess))
    def _():
        def write(c, carry):
            k = key_ref[:, chunk_at(c)]
            sel = (k >= base) & (k > KEY_NEG_INF)
            o_ref[:, chunk_at(c)] = jnp.where(sel, 0.0, NEG_BIAS).astype(o_ref.dtype)
            return carry
        lax.fori_loop(0, n_chunks, write, 0)

    @pl.when(has_excess)
    def _():
        need = kf - count(lambda k: k > base)
        before = (lax.broadcasted_iota(jnp.int32, (LANES, LANES), 0)
                  < lax.broadcasted_iota(jnp.int32, (LANES, LANES), 1))
        before = jnp.where(before, 1.0, 0.0)

        def lane_tile(c, seen):
            sl = pl.ds(pl.multiple_of(c * LANES, LANES), LANES)
            k = key_ref[:, sl]
            eq = k == base
            eqf = jnp.where(eq, 1.0, 0.0)
            rank = _dot(eqf, before) + seen
            sel = ((k > base) | (eq & (rank < need))) & (k > KEY_NEG_INF)
            o_ref[:, sl] = jnp.where(sel, 0.0, NEG_BIAS).astype(o_ref.dtype)
            return seen + jnp.sum(eqf, axis=1, keepdims=True)

        lax.fori_loop(0, n_chunks * nl, lane_tile, jnp.zeros((R, 1), F32))


def _topk_bias_t(key_ref, o_ref, topk, n_chunks, kc):
    Q = key_ref.shape[1]
    kf = float(topk)
    SUB = 8

    def rows_at(c):
        return pl.ds(pl.multiple_of(c * kc, kc), kc)

    def count(pred):
        def body(c, acc):
            hit = jnp.where(pred(key_ref[rows_at(c), :]), 1.0, 0.0).reshape(kc // SUB, SUB, Q)
            n = kc // SUB
            while n > 1:
                n //= 2
                hit = hit[:n] + hit[n:2 * n]
            return acc + hit[0]
        acc = lax.fori_loop(0, n_chunks, body, jnp.zeros((SUB, Q), F32))
        return jnp.sum(acc, axis=0, keepdims=True)

    base = jnp.where(count(lambda k: k >= 0) >= kf, 0, INT_MIN).astype(jnp.int32)

    def bit_step(it, base):
        cand = base | jnp.left_shift(jnp.int32(1), 30 - it)
        return jnp.where(count(lambda k: k >= cand) >= kf, cand, base)

    base = lax.fori_loop(0, 31, bit_step, base)
    has_excess = jnp.max(count(lambda k: k >= base)) > kf

    def store(c, sel):
        bias_t = jnp.where(sel, 0.0, NEG_BIAS)
        o_ref[:, rows_at(c)] = jnp.transpose(bias_t).astype(o_ref.dtype)

    @pl.when(jnp.logical_not(has_excess))
    def _():
        def write(c, carry):
            k = key_ref[rows_at(c), :]
            store(c, (k >= base) & (k > KEY_NEG_INF))
            return carry
        lax.fori_loop(0, n_chunks, write, 0)

    @pl.when(has_excess)
    def _():
        need = kf - count(lambda k: k > base)
        earlier = (lax.broadcasted_iota(jnp.int32, (kc, kc), 1)
                   < lax.broadcasted_iota(jnp.int32, (kc, kc), 0))
        earlier = jnp.where(earlier, 1.0, 0.0)

        def write(c, seen):
            k = key_ref[rows_at(c), :]
            eq = k == base
            eqf = jnp.where(eq, 1.0, 0.0)
            rank = _dot(earlier, eqf) + seen
            store(c, ((k > base) | (eq & (rank < need))) & (k > KEY_NEG_INF))
            return seen + jnp.sum(eqf, axis=0, keepdims=True)

        lax.fori_loop(0, n_chunks, write, jnp.zeros((1, Q), F32))


def _idx_prompt_kernel(iq_ref, ik_ref, iwt_ref, o_ref, key_sc, *, tq, kc, topk):
    i = pl.program_id(1)
    iq = iq_ref[...]
    lhs = jnp.concatenate([iq[:, h * D_IDX:(h + 1) * D_IDX] for h in range(H_IDX)], axis=0).astype(BF16)
    iwt = iwt_ref[...]
    qpos = i * tq + lax.broadcasted_iota(jnp.int32, (1, tq), 1)
    n_need = (i * tq + tq + kc - 1) // kc
    o_ref[...] = jnp.full(o_ref.shape, NEG_BIAS, o_ref.dtype)

    def score_chunk(c, carry):
        off = pl.multiple_of(c * kc, kc)
        ik = ik_ref[pl.ds(off, kc), 0:D_IDX].astype(BF16)
        s = _dot_nt(ik, lhs)
        score = jnp.zeros((kc, tq), F32)
        for h in range(H_IDX):
            score = score + jnp.maximum(s[:, h * tq:(h + 1) * tq], 0.0) * iwt[h:h + 1, :]
        kpos = off + lax.broadcasted_iota(jnp.int32, (kc, 1), 0)
        score = jnp.where(kpos <= qpos, score, -jnp.inf)
        key_sc[pl.ds(off, kc), :] = _sortable(score)
        return carry

    lax.fori_loop(0, n_need, score_chunk, 0)
    _topk_bias_t(key_sc, o_ref, topk, n_need, kc)


def _idx_prompt(p8, tail, *, B, L, tq, topk):
    nq = L // tq
    nmisc = TAIL_MISC // LANES
    return pl.pallas_call(
        functools.partial(_idx_prompt_kernel, tq=tq, kc=min(L, 512), topk=topk),
        out_shape=jax.ShapeDtypeStruct((B * L, L), BF16),
        grid=(B, nq),
        in_specs=[pl.BlockSpec((None, tq, W_GROUP), lambda b, i: (7, b * nq + i, 0)),
                  pl.BlockSpec((L, LANES), lambda b, i: (b, nmisc)),
                  pl.BlockSpec((H_IDX, tq), lambda b, i: (0, b * nq + i))],
        out_specs=pl.BlockSpec((tq, L), lambda b, i: (b * nq + i, 0)),
        scratch_shapes=[pltpu.VMEM((L, tq), jnp.int32)],
        compiler_params=_cparams(("parallel", "arbitrary")),
    )(p8, tail, jnp.transpose(tail[:, TAIL_MISC + D_IDX:TAIL_MISC + D_IDX + H_IDX]))


def _gelu_tanh(x):
    return 0.5 * x * (1.0 + jnp.tanh(math.sqrt(2.0 / math.pi) * (x + 0.044715 * (x * x * x))))


def _s5_kernel(u_ref, h0r_ref, h0i_ref, bre_ref, bim_ref, apr_ref, api_ref, pwr_ref, pwi_ref,
               cre_ref, cim_ref, d_ref, wglu_ref, y_ref, hr_ref, hi_ref, cr_sc, ci_sc, hsr_sc, hsi_sc, *, T):
    t = pl.program_id(1)

    @pl.when(t == 0)
    def _():
        cr_sc[...] = h0r_ref[...]
        ci_sc[...] = h0i_ref[...]

    u = u_ref[...]
    ub = u.astype(BF16)
    G = T // S5_GROUP
    hr = _dot(ub, bre_ref[...]).reshape(G, S5_GROUP, NSTATE)
    hi = _dot(ub, bim_ref[...]).reshape(G, S5_GROUP, NSTATE)
    for j in range(S5_GROUP.bit_length() - 1):
        sr = pltpu.roll(hr, 1 << j, 1)
        si = pltpu.roll(hi, 1 << j, 1)
        ar = apr_ref[j]
        ai = api_ref[j]
        hr, hi = hr + (ar * sr - ai * si), hi + (ar * si + ai * sr)
    cr = cr_sc[...]
    ci = ci_sc[...]
    pr = pwr_ref[...]
    pi_ = pwi_ref[...]
    for g in range(G):
        rows = slice(g * S5_GROUP, (g + 1) * S5_GROUP)
        gr = hr[g] + (pr * cr - pi_ * ci)
        gi = hi[g] + (pr * ci + pi_ * cr)
        hsr_sc[rows, :] = gr
        hsi_sc[rows, :] = gi
        cr = gr[S5_GROUP - 1:S5_GROUP, :]
        ci = gi[S5_GROUP - 1:S5_GROUP, :]
    cr_sc[...] = cr
    ci_sc[...] = ci
    hr_ref[...] = cr
    hi_ref[...] = ci
    y = (_dot(hsr_sc[...].astype(BF16), cre_ref[...]) - _dot(hsi_sc[...].astype(BF16), cim_ref[...])
         + d_ref[...] * u)
    y = _gelu_tanh(y)
    gu = _dot(y.astype(BF16), wglu_ref[...])
    y_ref[...] = gu[:, :W_GROUP] * jax.nn.sigmoid(gu[:, W_GROUP:])


def _s5(p8, h0r, h0i, tabs, l, *, B, L, T):
    nt = L // T
    bre, bim, apr, api, pwr, pwi, cre, cim, dsk, wglu = tabs
    lay = lambda *shape: pl.BlockSpec((None,) + shape, lambda b, t: (l,) + (0,) * len(shape))
    st = pl.BlockSpec((None, 1, NSTATE), lambda b, t: (b, 0, 0))
    return pl.pallas_call(
        functools.partial(_s5_kernel, T=T),
        out_shape=(jax.ShapeDtypeStruct((B * L, W_GROUP), F32),
                   jax.ShapeDtypeStruct((B, 1, NSTATE), F32), jax.ShapeDtypeStruct((B, 1, NSTATE), F32)),
        grid=(B, nt),
        in_specs=[pl.BlockSpec((None, T, W_GROUP), lambda b, t: (3, b * nt + t, 0)), st, st,
                  lay(W_GROUP, NSTATE), lay(W_GROUP, NSTATE), lay(3, S5_GROUP, NSTATE), lay(3, S5_GROUP, NSTATE),
                  lay(S5_GROUP, NSTATE), lay(S5_GROUP, NSTATE), lay(NSTATE, W_GROUP), lay(NSTATE, W_GROUP),
                  lay(1, W_GROUP), lay(W_GROUP, 2 * W_GROUP)],
        out_specs=(pl.BlockSpec((T, W_GROUP), lambda b, t: (b * nt + t, 0)), st, st),
        scratch_shapes=[pltpu.VMEM((1, NSTATE), F32), pltpu.VMEM((1, NSTATE), F32),
                        pltpu.VMEM((T, NSTATE), F32), pltpu.VMEM((T, NSTATE), F32)],
        compiler_params=_cparams(("parallel", "arbitrary")),
    )(p8, h0r, h0i, bre, bim, apr, api, pwr, pwi, cre, cim, dsk, wglu)


def _log_sigmoid(x):
    return jnp.minimum(x, 0.0) - jnp.log1p(jnp.exp(-jnp.abs(x)))


def _gla_kernel(q_ref, k_ref, v_ref, r_ref, misc_ref, wg_ref, bg_ref, ng_ref, s0_ref,
                y_ref, sfin_ref, s_sc, *, TB, CH, nt, valid_len):
    t = pl.program_id(1)
    HK = H_D * DK_D
    HV = H_D * DV_D

    @pl.when(t == 0)
    def _():
        s_sc[...] = jnp.zeros_like(s_sc)
        for h in range(H_D):
            s_sc[h * DK_D:(h + 1) * DK_D, h * DV_D:(h + 1) * DV_D] = s0_ref[h]

    row = lax.broadcasted_iota(jnp.int32, (TB, 1), 0)
    glow = misc_ref[:, D_IDX + H_IDX:D_IDX + H_IDX + GATE_RANK]
    logit = _dot(glow.astype(BF16), wg_ref[...].astype(BF16)) + bg_ref[...]
    la = _log_sigmoid(logit) * (1.0 / GATE_TAU)
    k = k_ref[...]
    if valid_len < TB:
        la = jnp.where(row < valid_len, la, 0.0)
        k = jnp.where(row < valid_len, k, 0.0)
    ri = lax.broadcasted_iota(jnp.int32, (TB, TB), 0)
    ci = lax.broadcasted_iota(jnp.int32, (TB, TB), 1)
    same = (ri // CH) == (ci // CH)
    lower = same & (ci <= ri)
    tri = jnp.where(lower, 1.0, 0.0).astype(BF16)
    blk = jnp.where(same, 1.0, 0.0).astype(BF16)
    la_hi = la.astype(BF16)
    la_lo = (la - la_hi.astype(F32)).astype(BF16)
    bc = _dot(tri, la_hi) + _dot(tri, la_lo)
    bl = _dot(blk, la_hi) + _dot(blk, la_lo)
    q = q_ref[...] * (DK_D ** -0.5)
    v = v_ref[...]
    vb = v.astype(BF16)
    qt = (q * jnp.exp(bc)).astype(BF16)
    kc = (k * jnp.exp(-bc)).astype(BF16)
    kh_t = jnp.transpose(k * jnp.exp(bl - bc))
    dec_t = jnp.transpose(jnp.exp(bl))

    o_parts = []
    for h in range(H_D):
        att = _dot_nt(qt[:, h * DK_D:(h + 1) * DK_D], kc[:, h * DK_D:(h + 1) * DK_D])
        att = jnp.where(lower, att, 0.0)
        o_parts.append(_dot(att.astype(BF16), vb[:, h * DV_D:(h + 1) * DV_D]))
    o_intra = jnp.concatenate(o_parts, axis=1)

    head_blk = (lax.broadcasted_iota(jnp.int32, (HK, HV), 0) // DK_D
                == lax.broadcasted_iota(jnp.int32, (HK, HV), 1) // DV_D)
    col = lax.broadcasted_iota(jnp.int32, (1, TB), 1)
    S = s_sc[...]
    o_rows = []
    for c in range(TB // CH):
        o_rows.append(_dot(qt[c * CH:(c + 1) * CH, :], S.astype(BF16)))
        in_chunk = (col >= c * CH) & (col < (c + 1) * CH)
        kv = _dot(jnp.where(in_chunk, kh_t, 0.0).astype(BF16), vb)
        S = dec_t[:, c * CH:c * CH + 1] * S + jnp.where(head_blk, kv, 0.0)
    s_sc[...] = S
    o = o_intra + jnp.concatenate(o_rows, axis=0)

    outs = []
    for h in range(H_D):
        oh = o[:, h * DV_D:(h + 1) * DV_D]
        ms = jnp.mean(oh * oh, axis=1, keepdims=True)
        outs.append(oh * lax.rsqrt(ms + LN_EPS) * ng_ref[...])
    r = r_ref[...]
    y_ref[...] = jnp.concatenate(outs, axis=1) * (r * jax.nn.sigmoid(r))

    @pl.when(t == nt - 1)
    def _():
        for h in range(H_D):
            sfin_ref[h] = S[h * DK_D:(h + 1) * DK_D, h * DV_D:(h + 1) * DV_D]


def _gla(tail, s0, w_gate, b_gate, norm_g, l, *, B, L, TB, valid_len):
    nt = L // TB
    lay = lambda *shape: pl.BlockSpec((None,) + shape, lambda b, t: (l,) + (0,) * len(shape))
    sspec = pl.BlockSpec((None, H_D, DK_D, DV_D), lambda b, t: (b, 0, 0, 0))
    return pl.pallas_call(
        functools.partial(_gla_kernel, TB=TB, CH=GLA_CHUNK, nt=nt, valid_len=valid_len),
        out_shape=(jax.ShapeDtypeStruct((B * L, W_GROUP), F32),
                   jax.ShapeDtypeStruct((B, H_D, DK_D, DV_D), F32)),
        grid=(B, nt),
        in_specs=[pl.BlockSpec((TB, 256), lambda b, t: (b * nt + t, 0)),
                  pl.BlockSpec((TB, 256), lambda b, t: (b * nt + t, 1)),
                  pl.BlockSpec((TB, 512), lambda b, t: (b * nt + t, 1)),
                  pl.BlockSpec((TB, 512), lambda b, t: (b * nt + t, 2)),
                  pl.BlockSpec((TB, LANES), lambda b, t: (b * nt + t, TAIL_MISC // LANES)),
                  lay(GATE_RANK, H_D * DK_D), lay(1, H_D * DK_D), lay(1, DV_D), sspec],
        out_specs=(pl.BlockSpec((TB, W_GROUP), lambda b, t: (b * nt + t, 0)), sspec),
        scratch_shapes=[pltpu.VMEM((H_D * DK_D, H_D * DV_D), F32)],
        compiler_params=_cparams(("parallel", "arbitrary")),
    )(tail, tail, tail, tail, tail, w_gate, b_gate, norm_g, s0)


def _sidx_kernel(pt_ref, iq_ref, iw_ref, *rest, PPS, NS, n_new, topk):
    page_refs = rest[:PPS]
    new_ref, o_ref, key_sc = rest[PPS:]
    s_id = pl.program_id(1)
    iq = iq_ref[...].astype(BF16)
    iw = iw_ref[...]

    def scores(keys_t):
        hs = jnp.maximum(_dot(iq, keys_t.astype(BF16)), 0.0) * iw
        sc = jnp.zeros((n_new, LANES), F32)
        for h in range(H_IDX):
            sc = sc + hs[h * n_new:(h + 1) * n_new]
        return sc

    for p in range(PPS):
        off = pl.multiple_of((s_id * PPS + p) * LANES, LANES)
        key_sc[:, pl.ds(off, LANES)] = _sortable(scores(page_refs[p][...]))

    @pl.when(s_id == NS - 1)
    def _():
        sc = scores(new_ref[...])
        qi = lax.broadcasted_iota(jnp.int32, (n_new, LANES), 0)
        kj = lax.broadcasted_iota(jnp.int32, (n_new, LANES), 1)
        sc = jnp.where(kj <= qi, sc, -jnp.inf)
        key_sc[:, NS * PPS * LANES:(NS * PPS + 1) * LANES] = _sortable(sc)
        n_tiles = NS * PPS + 1
        n_chunks = 3 if n_tiles % 3 == 0 else 1
        _topk_bias(key_sc, o_ref, topk, n_chunks, n_tiles // n_chunks * LANES)


def _sidx(page_table, iq_rows, iw_tile, cache_idx, idx_new, l, *, PPS, topk):
    Bs, n_pages = page_table.shape
    NS = n_pages // PPS
    n_new = iq_rows.shape[1] // H_IDX
    W = (n_pages + 1) * LANES

    def page_spec(p):
        return pl.BlockSpec((None, None, D_IDX, PAGE_SIZE),
                            lambda b, s, pt: (l, pt[b, s * PPS + p], 0, 0))

    grid_spec = pltpu.PrefetchScalarGridSpec(
        num_scalar_prefetch=1,
        grid=(Bs, NS),
        in_specs=[pl.BlockSpec((None, H_IDX * n_new, D_IDX), lambda b, s, pt: (b, 0, 0)),
                  pl.BlockSpec((None, H_IDX * n_new, LANES), lambda b, s, pt: (b, 0, 0))]
                 + [page_spec(p) for p in range(PPS)]
                 + [pl.BlockSpec((None, D_IDX, PAGE_SIZE), lambda b, s, pt: (b, 0, 0))],
        out_specs=pl.BlockSpec((None, n_new, W), lambda b, s, pt: (b, 0, 0)),
        scratch_shapes=[pltpu.VMEM((n_new, W), jnp.int32)],
    )
    return pl.pallas_call(
        functools.partial(_sidx_kernel, PPS=PPS, NS=NS, n_new=n_new, topk=topk),
        out_shape=jax.ShapeDtypeStruct((Bs, n_new, W), F32),
        grid_spec=grid_spec,
        compiler_params=_cparams(("parallel", "arbitrary")),
    )(page_table, iq_rows, iw_tile, *([cache_idx] * PPS), idx_new)


def _sattn_kernel(pt_ref, q_ref, *rest, PPS, NS, R, n_new, has_mask, diff):
    k_refs = rest[:PPS]
    v_refs = rest[PPS:2 * PPS]
    rest = rest[2 * PPS:]
    knew_ref, vnew_ref, far_ref, near_ref = rest[:4]
    rest = rest[4:]
    if has_mask:
        mask_ref, masknew_ref, spread_ref = rest[:3]
        rest = rest[3:]
    lam_ref, gain_ref, o_ref, m_sc, l_sc, acc_sc = rest
    s_id = pl.program_id(1)

    @pl.when(s_id == 0)
    def _():
        m_sc[...] = jnp.full(m_sc.shape, -jnp.inf, F32)
        l_sc[...] = jnp.zeros_like(l_sc)
        acc_sc[...] = jnp.zeros_like(acc_sc)

    q = q_ref[...].astype(BF16)
    PR = PAGE_SIZE * 4

    def expand(m):
        return jnp.concatenate([_dot(m, spread_ref[...])] * (R // n_new), axis=0)

    def pages(kps, vps, biases):
        ss = [_dot_nt(q, kp.astype(BF16)) + b for kp, b in zip(kps, biases)]
        top = ss[0]
        for s in ss[1:]:
            top = jnp.maximum(top, s)
        m_prev = m_sc[...]
        m_new = jnp.maximum(m_prev, jnp.max(top, axis=1, keepdims=True))
        a = jnp.exp2(m_prev - m_new)
        ps = [jnp.exp2(s - m_new) for s in ss]
        tot = ps[0]
        for p in ps[1:]:
            tot = tot + p
        pv = _dot(ps[0].astype(BF16), vps[0].astype(BF16))
        for p, vp in zip(ps[1:], vps[1:]):
            pv = pv + _dot(p.astype(BF16), vp.astype(BF16))
        l_sc[...] = a * l_sc[...] + jnp.sum(tot, axis=1, keepdims=True)
        acc_sc[...] = a * acc_sc[...] + pv
        m_sc[...] = m_new

    far = far_ref[...]
    biases = []
    for p in range(PPS):
        bias = far
        if p == PPS - 1:
            bias = jnp.where(s_id == NS - 1, near_ref[:, 0:PR], far)
        if has_mask:
            bias = bias + expand(mask_ref[:, p * LANES:(p + 1) * LANES])
        biases.append(bias)
    pages([r[...] for r in k_refs], [r[...] for r in v_refs], biases)

    @pl.when(s_id == NS - 1)
    def _():
        bias = near_ref[:, PR:2 * PR]
        if has_mask:
            bias = bias + expand(masknew_ref[...])
        pages([knew_ref[...]], [vnew_ref[...]], [bias])
        o_all = acc_sc[...] / l_sc[...]
        outs = []
        for h in range(4):
            if diff:
                o1 = o_all[(2 * h) * n_new:(2 * h + 1) * n_new]
                o2 = o_all[(2 * h + 1) * n_new:(2 * h + 2) * n_new]
                o = o1 - lam_ref[...] * o2
                ms = jnp.mean(o * o, axis=1, keepdims=True)
                outs.append(o * lax.rsqrt(ms + LN_EPS) * gain_ref[...])
            else:
                outs.append(o_all[h * n_new:(h + 1) * n_new])
        o_ref[...] = jnp.concatenate(outs, axis=1)


def _sattn(page_table, q_rows, cache_k, cache_v, knew, vnew, far, near, mask, lam_row, gain_row, l,
           *, PPS, diff):
    Bs, n_pages = page_table.shape
    NS = n_pages // PPS
    R = q_rows.shape[1]
    n_new = 8
    PR = PAGE_SIZE * 4
    has_mask = mask is not None

    def page_spec(p):
        return pl.BlockSpec((None, None, PR, LANES), lambda b, s, pt: (l, pt[b, s * PPS + p], 0, 0))

    per_b = lambda *shape: pl.BlockSpec((None,) + shape, lambda b, s, pt: (b,) + (0,) * len(shape))
    const = lambda *shape: pl.BlockSpec(shape, lambda b, s, pt: (0,) * len(shape))
    in_specs = ([per_b(R, LANES)] + [page_spec(p) for p in range(PPS)] * 2
                + [per_b(PR, LANES), per_b(PR, LANES), const(R, PR), const(R, 2 * PR)])
    args = [q_rows] + [cache_k] * PPS + [cache_v] * PPS + [knew, vnew, far, near]
    if has_mask:
        spread = (jnp.arange(PR, dtype=jnp.int32)[None, :] // 4
                  == jnp.arange(PAGE_SIZE, dtype=jnp.int32)[:, None]).astype(F32)
        in_specs += [pl.BlockSpec((None, n_new, PPS * LANES), lambda b, s, pt: (b, 0, s)),
                     pl.BlockSpec((None, n_new, LANES), lambda b, s, pt: (b, 0, n_pages)),
                     const(PAGE_SIZE, PR)]
        args += [mask, mask, spread]
    in_specs += [const(1, LANES), const(1, LANES)]
    args += [lam_row, gain_row]
    grid_spec = pltpu.PrefetchScalarGridSpec(
        num_scalar_prefetch=1,
        grid=(Bs, NS),
        in_specs=in_specs,
        out_specs=pl.BlockSpec((None, n_new, W_GROUP), lambda b, s, pt: (b, 0, 0)),
        scratch_shapes=[pltpu.VMEM((R, 1), F32), pltpu.VMEM((R, 1), F32), pltpu.VMEM((R, LANES), F32)],
    )
    return pl.pallas_call(
        functools.partial(_sattn_kernel, PPS=PPS, NS=NS, R=R, n_new=n_new,
                          has_mask=has_mask, diff=diff),
        out_shape=jax.ShapeDtypeStruct((Bs, n_new, W_GROUP), F32),
        grid_spec=grid_spec,
        compiler_params=_cparams(("parallel", "arbitrary")),
    )(page_table, *args)


def _rel_bucket(dist):
    n = jnp.maximum(dist, 0)
    nf = jnp.maximum(n, REL_MAX_EXACT).astype(F32)
    large = REL_MAX_EXACT + (jnp.log(nf / REL_MAX_EXACT) / math.log(REL_MAX_DIST / REL_MAX_EXACT)
                             * (REL_BUCKETS - REL_MAX_EXACT)).astype(jnp.int32)
    large = jnp.minimum(large, REL_BUCKETS - 1)
    return jnp.where(n < REL_MAX_EXACT, n, large)


def _rel_lookup(rel, bucket):
    out = jnp.zeros((rel.shape[1],) + bucket.shape, F32)
    for b in range(REL_BUCKETS):
        out = jnp.where(bucket[None] == b, rel[b].reshape((-1,) + (1,) * bucket.ndim), out)
    return out * LOG2E


def _prompt_bias_tables(rel, T):
    assert T >= REL_MAX_DIST
    r = jnp.arange(T, dtype=jnp.int32)[:, None]
    c = jnp.arange(T, dtype=jnp.int32)[None, :]
    d0 = r - c
    t0 = jnp.where((d0 >= 0)[None], _rel_lookup(rel, _rel_bucket(d0)), NEG_BIAS)
    t1 = _rel_lookup(rel, _rel_bucket(T + r - c))
    t2 = _rel_lookup(rel, _rel_bucket(jnp.full((T, T), 2 * T, jnp.int32)))
    return jnp.stack([t0, t1, t2], axis=1)


def _sample_bias_tables(rel, n_new, reps):
    H = rel.shape[1]
    t = jnp.arange(n_new, dtype=jnp.int32)[:, None]
    c = jnp.arange(PAGE_SIZE, dtype=jnp.int32)[None, :]
    last = _rel_lookup(rel, _rel_bucket(PAGE_SIZE + t - c))
    dn = t - c
    new = jnp.where(((dn >= 0) & (c < n_new))[None], _rel_lookup(rel, _rel_bucket(dn)), NEG_BIAS)
    far = _rel_lookup(rel, _rel_bucket(jnp.full((n_new, PAGE_SIZE), 2 * PAGE_SIZE, jnp.int32)))

    def rows(x):
        own = jnp.eye(H, dtype=bool)[:, None, None, :]
        y = jnp.where(own, x[..., None], NEG_BIAS).reshape(H, 1, n_new, PAGE_SIZE * H)
        return jnp.broadcast_to(y, (H, reps, n_new, PAGE_SIZE * H)).reshape(H * reps * n_new, PAGE_SIZE * H)

    return rows(far), jnp.concatenate([rows(last), rows(new)], axis=1)


def _s5_tables(a_re, a_im, log_dt, b_re, b_im, c_re, c_im, d_skip, w_glu):
    dt = jnp.exp(log_dt.astype(F32))[..., None]
    lam_re, lam_im = a_re.astype(F32), a_im.astype(F32)
    z_re, z_im = lam_re * dt, lam_im * dt
    mag = jnp.exp(z_re)
    e_re, e_im = mag * jnp.cos(z_im), mag * jnp.sin(z_im)
    den = lam_re * lam_re + lam_im * lam_im
    f_re = ((e_re - 1.0) * lam_re + e_im * lam_im) / den
    f_im = (e_im * lam_re - (e_re - 1.0) * lam_im) / den
    br, bi = b_re.astype(F32), b_im.astype(F32)
    bb_re = f_re[..., None] * br - f_im[..., None] * bi
    bb_im = f_re[..., None] * bi + f_im[..., None] * br
    eye = jnp.eye(G_B, dtype=F32)
    bd_in = lambda bb: jnp.einsum('lgpc,gh->lgchp', bb, eye).reshape(DEPTH, W_GROUP, NSTATE).astype(BF16)
    bd_out = lambda cc: jnp.einsum('lgcp,gh->lgphc', cc.astype(F32), eye).reshape(DEPTH, NSTATE, W_GROUP).astype(BF16)

    def powers(ts):
        tt = ts[None, :, None, None]
        pm = jnp.exp(z_re[:, None] * tt)
        return ((pm * jnp.cos(z_im[:, None] * tt)).reshape(DEPTH, -1, NSTATE),
                (pm * jnp.sin(z_im[:, None] * tt)).reshape(DEPTH, -1, NSTATE))

    n_steps = S5_GROUP.bit_length() - 1
    apr, api = powers(jnp.asarray([2.0 ** j for j in range(n_steps)], F32))
    live = (jnp.arange(S5_GROUP)[None, :] >= (2 ** jnp.arange(n_steps))[:, None]).astype(F32)
    apr = apr[:, :, None, :] * live[None, :, :, None]
    api = api[:, :, None, :] * live[None, :, :, None]
    pwr, pwi = powers(jnp.arange(1, S5_GROUP + 1, dtype=F32))
    return (bd_in(bb_re), bd_in(bb_im), apr, api, pwr, pwi, bd_out(c_re), bd_out(c_im),
            d_skip.astype(F32).reshape(DEPTH, 1, W_GROUP), w_glu.astype(BF16))


def _tail_weights(w_in_t):
    pad = jnp.zeros((w_in_t.shape[0], N_TAIL - 1624, w_in_t.shape[2]), w_in_t.dtype)
    return jnp.concatenate([w_in_t[:, 4168:5704], w_in_t[:, 4096:4168], w_in_t[:, 5704:5720], pad],
                           axis=1).astype(BF16)


def _channel_mix(ys, x, l, W, tm):
    x1, x1b = _mix_ln(ys, W['w_out'], l, x, W['ln1_g'], W['ln1_b'], tm)
    hid = _swiglu(x1b, W['ffn_w_gate'], W['ffn_w_up'], l, tm, 512)
    return _mm_ln(hid, W['ffn_w_down'], l, x1, W['ln2_g'], W['ln2_b'], tm, D_FF // 4)


def _prompt_layer(x, xb, l, W, B, L):
    T_ATT = 512
    p8, a_k, a_v, c_k, c_v = _proj_main(xb, W['w_in_t'], l, 1024)
    tail = _proj_tail(xb, W['w_tail_t'], l, 512)
    y_a = _flash_prompt(p8, (0, 1, 2), W['tab_a'], None, W['lam_row'][l], W['gain_a'][l],
                        B=B, L=L, T=T_ATT, groups=2, dk=DH_A, scale=DH_A ** -0.5, diff=True)
    zeros_h = jnp.zeros((B, 1, NSTATE), F32)
    y_b, h_re, h_im = _s5(p8, zeros_h, zeros_h, W['s5'], l, B=B, L=L, T=S5_BLOCK)
    mask = _idx_prompt(p8, tail, B=B, L=L, tq=128, topk=min(TOPK_MAX, L // 4))
    y_c = _flash_prompt(p8, (4, 5, 6), W['tab_c'], mask, W['lam_row'][l], W['gain_a'][l],
                        B=B, L=L, T=T_ATT, groups=1, dk=DH_C, scale=DH_C ** -0.5, diff=False)
    y_d, s_fin = _gla(tail, jnp.zeros((B, H_D, DK_D, DV_D), F32), W['gla_w_gate'], W['gla_b_gate'],
                      W['gla_norm'], l, B=B, L=L, TB=128, valid_len=128)
    x, xb = _channel_mix([y_a, y_b, y_c, y_d], x, l, W, 512)
    news = (a_k.reshape(B, L, H_A, 128), a_v.reshape(B, L, H_A, 128),
            c_k.reshape(B, L, H_C, 128), c_v.reshape(B, L, H_C, 128),
            tail[:, TAIL_MISC:TAIL_MISC + D_IDX].reshape(B, L, D_IDX),
            h_re.reshape(B, G_B, P_B), h_im.reshape(B, G_B, P_B), s_fin)
    return x, xb, news


def _sample_layer(x, xb, l, W, C, Bs, Ls):
    M = Bs * Ls
    PPS = 8
    p8, a_k, a_v, c_k, c_v = _proj_main(xb, W['w_in_t'], l, M)
    tail = _proj_tail(xb, W['w_tail_t'], l, M)
    pt = C['page_table']

    def pad_new(z):
        z = z.reshape(Bs, Ls, z.shape[-1])
        return jnp.pad(z, ((0, 0), (0, PAGE_SIZE - Ls), (0, 0)))

    def pad_rows(z):
        z = z.reshape(Bs, Ls * 4, LANES)
        return jnp.pad(z, ((0, 0), (0, PAGE_SIZE * 4 - Ls * 4), (0, 0)))

    qa = p8[0].reshape(Bs, Ls, H_A, 2, DH_A) * (DH_A ** -0.5 * LOG2E)
    q_a = jnp.einsum('bqhwd,wv->bhwqvd', qa, jnp.eye(2, dtype=F32)).reshape(Bs, H_A * 2 * Ls, LANES)
    y_a = _sattn(pt, q_a, C['a_k'], C['a_v'], pad_rows(a_k), pad_rows(a_v), W['far_a'], W['near_a'],
                 None, W['lam_row'][l], W['gain_a'][l], l, PPS=PPS, diff=True)
    y_b, h_re, h_im = _s5(p8, C['b_re'][l].reshape(Bs, 1, NSTATE), C['b_im'][l].reshape(Bs, 1, NSTATE),
                          W['s5'], l, B=Bs, L=Ls, T=Ls)
    misc = tail[:, TAIL_MISC:]
    iq_rows = p8[7].reshape(Bs, Ls, H_IDX, D_IDX).transpose(0, 2, 1, 3).reshape(Bs, H_IDX * Ls, D_IDX)
    iw = misc[:, D_IDX:D_IDX + H_IDX].reshape(Bs, Ls, H_IDX).transpose(0, 2, 1).reshape(Bs, H_IDX * Ls, 1)
    iw_tile = jnp.broadcast_to(iw, (Bs, H_IDX * Ls, LANES))
    mask = _sidx(pt, iq_rows, iw_tile, C['c_idx_t'], pad_new(misc[:, :D_IDX]).transpose(0, 2, 1), l, PPS=PPS,
                 topk=min(TOPK_MAX, (pt.shape[1] * PAGE_SIZE + Ls) // 4))
    q_c = (p8[4].reshape(Bs, Ls, H_C, DH_C) * (DH_C ** -0.5 * LOG2E)).transpose(0, 2, 1, 3)
    y_c = _sattn(pt, q_c.reshape(Bs, H_C * Ls, LANES), C['c_k'], C['c_v'], pad_rows(c_k), pad_rows(c_v),
                 W['far_c'], W['near_c'], mask, W['lam_row'][l], W['gain_a'][l], l, PPS=PPS, diff=False)
    tail_pad = pad_new(tail).reshape(Bs * PAGE_SIZE, N_TAIL)
    y_d, s_fin = _gla(tail_pad, C['d'][l], W['gla_w_gate'], W['gla_b_gate'], W['gla_norm'], l,
                      B=Bs, L=PAGE_SIZE, TB=PAGE_SIZE, valid_len=Ls)
    y_d = y_d.reshape(Bs, PAGE_SIZE, W_GROUP)[:, :Ls].reshape(M, W_GROUP)
    x, xb = _channel_mix([y_a.reshape(M, W_GROUP), y_b, y_c.reshape(M, W_GROUP), y_d], x, l, W, M)
    news = (a_k.reshape(Bs, Ls, H_A, 128), a_v.reshape(Bs, Ls, H_A, 128),
            c_k.reshape(Bs, Ls, H_C, 128), c_v.reshape(Bs, Ls, H_C, 128),
            misc[:, :D_IDX].reshape(Bs, Ls, D_IDX),
            h_re.reshape(Bs, G_B, P_B), h_im.reshape(Bs, G_B, P_B), s_fin)
    return x, xb, news


def kernel(x_prompt, x_sample, cache_a_k, cache_a_v, cache_c_k, cache_c_v, cache_c_idx, state_b_re, state_b_im, state_d, page_table, rel_bias, w_in, w_out, lam_q1, lam_k1, lam_q2, lam_k2, a_subln, s5_a_re, s5_a_im, s5_log_dt, s5_b_re, s5_b_im, s5_c_re, s5_c_im, s5_d, s5_w_glu, gla_w_gate, gla_b_gate, gla_norm, ln1_g, ln1_b, ffn_w_gate, ffn_w_up, ffn_w_down, ln2_g, ln2_b):
    B, L, _ = x_prompt.shape
    Bs, Ls, _ = x_sample.shape
    n_pool = cache_a_k.shape[1]

    lam_init = jnp.asarray([0.8 - 0.6 * math.exp(-0.3 * l) for l in range(DEPTH)], F32)
    lam = (jnp.exp(jnp.sum(lam_q1.astype(F32) * lam_k1.astype(F32), axis=-1))
           - jnp.exp(jnp.sum(lam_q2.astype(F32) * lam_k2.astype(F32), axis=-1)) + lam_init)
    s5_args = (s5_a_re, s5_a_im, s5_log_dt, s5_b_re, s5_b_im, s5_c_re, s5_c_im, s5_d, s5_w_glu)
    rel = rel_bias.astype(F32)
    w_in_t = jnp.swapaxes(w_in, 1, 2)
    far_a, near_a = _sample_bias_tables(rel[:, :H_A], Ls, 2)
    far_c, near_c = _sample_bias_tables(rel[:, H_A:], Ls, 1)
    W = {
        'w_in_t': w_in_t, 'w_tail_t': _tail_weights(w_in_t), 'w_out': w_out.astype(BF16),
        'ffn_w_gate': ffn_w_gate, 'ffn_w_up': ffn_w_up, 'ffn_w_down': ffn_w_down.astype(BF16),
        'ln1_g': ln1_g.reshape(DEPTH, 1, D_MODEL), 'ln1_b': ln1_b.reshape(DEPTH, 1, D_MODEL),
        'ln2_g': ln2_g.reshape(DEPTH, 1, D_MODEL), 'ln2_b': ln2_b.reshape(DEPTH, 1, D_MODEL),
        'lam_row': jnp.broadcast_to(lam[:, None, None], (DEPTH, 1, LANES)),
        'gain_a': (a_subln.astype(F32) * (1.0 - lam_init)[:, None]).reshape(DEPTH, 1, LANES),
        'tab_a': _prompt_bias_tables(rel[:, :H_A], 512), 'tab_c': _prompt_bias_tables(rel[:, H_A:], 512),
        'far_a': far_a, 'near_a': near_a, 'far_c': far_c, 'near_c': near_c,
        's5': _s5_tables(*s5_args),
        'gla_w_gate': gla_w_gate, 'gla_b_gate': gla_b_gate.reshape(DEPTH, 1, H_D * DK_D),
        'gla_norm': gla_norm.reshape(DEPTH, 1, DV_D),
    }
    C = {
        'page_table': page_table,
        'a_k': cache_a_k.reshape(DEPTH, n_pool, PAGE_SIZE * H_A, LANES),
        'a_v': cache_a_v.reshape(DEPTH, n_pool, PAGE_SIZE * H_A, LANES),
        'c_k': cache_c_k.reshape(DEPTH, n_pool, PAGE_SIZE * H_C, LANES),
        'c_v': cache_c_v.reshape(DEPTH, n_pool, PAGE_SIZE * H_C, LANES),
        'c_idx_t': jnp.swapaxes(cache_c_idx, 2, 3),
        'b_re': state_b_re, 'b_im': state_b_im, 'd': state_d,
    }

    xp = x_prompt.reshape(B * L, D_MODEL)
    xs = x_sample.reshape(Bs * Ls, D_MODEL)
    xpb, xsb = xp, xs
    news_p, news_s = [], []
    for l in range(DEPTH):
        xp, xpb, n_p = _prompt_layer(xp, xpb, l, W, B, L)
        xs, xsb, n_s = _sample_layer(xs, xsb, l, W, C, Bs, Ls)
        news_p.append(n_p)
        news_s.append(n_s)
    stack = lambda news: [jnp.stack(z, axis=0) for z in zip(*news)]
    return (xp.reshape(B, L, D_MODEL), xs.reshape(Bs, Ls, D_MODEL), *stack(news_p), *stack(news_s))
```

```python
import functools
import math

import jax
import jax.numpy as jnp
from jax import lax
from jax.experimental import pallas as pl
from jax.experimental.pallas import tpu as pltpu

F32 = jnp.float32
BF16 = jnp.bfloat16

D_MODEL = 2048
DEPTH = 4
PAST_LEN = 16384
PAGE_SIZE = 128
W_GROUP = 512
H_A = 4
DH_A = 64
H_C = 4
DH_C = 128
H_IDX = 8
D_IDX = 64
TOPK_MAX = 256
G_B = 32
P_B = 64
S5_CH = 16
H_D = 4
DK_D = 64
DV_D = 128
GATE_RANK = 16
GATE_TAU = 16.0
GLA_CHUNK = 16
D_FF = 5632
REL_BUCKETS = 32
REL_MAX_EXACT = 16
REL_MAX_DIST = 128
DEEPNORM_ALPHA = (2 * DEPTH) ** 0.25
LN_EPS = 1e-5

N_MAIN = 4096
N_TAIL = 1664
TAIL_MISC = 1536
NSTATE = G_B * P_B

NEG_BIAS = -1e30
INT_MIN = -2 ** 31
KEY_NEG_INF = -2139095041

VMEM_LIMIT_BYTES = 56 * 1024 * 1024
LANES = 128
LOG2E = 1.4426950408889634
FLASH_ROW_BLOCK = 256
S5_GROUP = 8
S5_BLOCK = 256


def _cparams(sem):
    return pltpu.CompilerParams(dimension_semantics=sem, vmem_limit_bytes=VMEM_LIMIT_BYTES)


def _dot(a, b):
    return jnp.dot(a, b, preferred_element_type=F32)


def _dot_nt(a, b):
    return lax.dot_general(a, b, (((1,), (1,)), ((), ())), preferred_element_type=F32)


KV_SLABS = (1, 2, 5, 6)


def _proj_main_kernel(a_ref, w_ref, o_ref, *kv_refs, tm):
    j = pl.program_id(1)
    res = _dot_nt(a_ref[...].astype(BF16), w_ref[...].astype(BF16))
    o_ref[...] = res
    for slab, kv_ref in zip(KV_SLABS, kv_refs):
        @pl.when(j == slab)
        def _(kv_ref=kv_ref):
            for h in range(4):
                kv_ref[pl.ds(h, tm, stride=4), :] = res[:, h * LANES:(h + 1) * LANES]


def _proj_main(x, w_in_t, l, tm):
    M, K = x.shape
    tn = W_GROUP
    kv_shape = jax.ShapeDtypeStruct((M * 4, LANES), F32)
    kv_spec = pl.BlockSpec((tm * 4, LANES), lambda i, j: (i, 0))
    return pl.pallas_call(
        functools.partial(_proj_main_kernel, tm=tm),
        out_shape=(jax.ShapeDtypeStruct((N_MAIN // tn, M, tn), F32),) + (kv_shape,) * 4,
        grid=(M // tm, N_MAIN // tn),
        in_specs=[pl.BlockSpec((tm, K), lambda i, j: (i, 0)),
                  pl.BlockSpec((None, tn, K), lambda i, j: (l, j, 0))],
        out_specs=(pl.BlockSpec((None, tm, tn), lambda i, j: (j, i, 0)),) + (kv_spec,) * 4,
        compiler_params=_cparams(("parallel", "arbitrary")),
    )(x, w_in_t)


def _proj_tail_kernel(a_ref, w_ref, o_ref):
    o_ref[...] = _dot_nt(a_ref[...].astype(BF16), w_ref[...])


def _proj_tail(x, w_tail_t, l, tm):
    M, K = x.shape
    return pl.pallas_call(
        _proj_tail_kernel,
        out_shape=jax.ShapeDtypeStruct((M, N_TAIL), F32),
        grid=(M // tm,),
        in_specs=[pl.BlockSpec((tm, K), lambda i: (i, 0)),
                  pl.BlockSpec((None, N_TAIL, K), lambda i: (l, 0, 0))],
        out_specs=pl.BlockSpec((tm, N_TAIL), lambda i: (i, 0)),
        compiler_params=_cparams(("parallel",)),
    )(x, w_tail_t)


def _deepnorm_ln(x_ref, acc, g_ref, b_ref, o32_ref, o16_ref):
    z = DEEPNORM_ALPHA * x_ref[...] + acc
    mu = jnp.mean(z, axis=1, keepdims=True)
    zc = z - mu
    var = jnp.mean(zc * zc, axis=1, keepdims=True)
    y = zc * lax.rsqrt(var + LN_EPS) * g_ref[...] + b_ref[...]
    o32_ref[...] = y
    o16_ref[...] = y.astype(BF16)


def _mix_ln_kernel(*refs, n_a, tk):
    a_refs = refs[:n_a]
    w_ref, x_ref, g_ref, b_ref, o32_ref, o16_ref = refs[n_a:]
    acc = _dot(a_refs[0][...].astype(BF16), w_ref[0:tk, :])
    for k in range(1, n_a):
        acc = acc + _dot(a_refs[k][...].astype(BF16), w_ref[k * tk:(k + 1) * tk, :])
    _deepnorm_ln(x_ref, acc, g_ref, b_ref, o32_ref, o16_ref)


def _mix_ln(a_list, w16, l, x, g, b, tm):
    M, N = x.shape
    tk = a_list[0].shape[1]
    K = tk * len(a_list)
    row = pl.BlockSpec((None, 1, N), lambda i: (l, 0, 0))
    tile = pl.BlockSpec((tm, N), lambda i: (i, 0))
    return pl.pallas_call(
        functools.partial(_mix_ln_kernel, n_a=len(a_list), tk=tk),
        out_shape=(jax.ShapeDtypeStruct((M, N), F32), jax.ShapeDtypeStruct((M, N), BF16)),
        grid=(M // tm,),
        in_specs=[pl.BlockSpec((tm, tk), lambda i: (i, 0)) for _ in a_list]
                 + [pl.BlockSpec((None, K, N), lambda i: (l, 0, 0)), tile, row, row],
        out_specs=(tile, tile),
        compiler_params=_cparams(("parallel",)),
    )(*a_list, w16, x, g, b)


def _mm_ln_kernel(a_ref, w_ref, x_ref, g_ref, b_ref, o32_ref, o16_ref, acc_ref, *, n_k):
    k = pl.program_id(1)

    @pl.when(k == 0)
    def _():
        acc_ref[...] = jnp.zeros_like(acc_ref)

    acc_ref[...] += _dot(a_ref[...], w_ref[...])

    @pl.when(k == n_k - 1)
    def _():
        _deepnorm_ln(x_ref, acc_ref[...], g_ref, b_ref, o32_ref, o16_ref)


def _mm_ln(a16, w16, l, x, g, b, tm, tk):
    M, N = x.shape
    n_k = a16.shape[1] // tk
    row = pl.BlockSpec((None, 1, N), lambda i, k: (l, 0, 0))
    tile = pl.BlockSpec((tm, N), lambda i, k: (i, 0))
    return pl.pallas_call(
        functools.partial(_mm_ln_kernel, n_k=n_k),
        out_shape=(jax.ShapeDtypeStruct((M, N), F32), jax.ShapeDtypeStruct((M, N), BF16)),
        grid=(M // tm, n_k),
        in_specs=[pl.BlockSpec((tm, tk), lambda i, k: (i, k)),
                  pl.BlockSpec((None, tk, N), lambda i, k: (l, k, 0)), tile, row, row],
        out_specs=(tile, tile),
        scratch_shapes=[pltpu.VMEM((tm, N), F32)],
        compiler_params=_cparams(("parallel", "arbitrary")),
    )(a16, w16, x, g, b)


def _swiglu_kernel(x_ref, wg_ref, wu_ref, o_ref):
    a = x_ref[...].astype(BF16)
    gate = _dot(a, wg_ref[...].astype(BF16))
    up = _dot(a, wu_ref[...].astype(BF16))
    o_ref[...] = (gate * jax.nn.sigmoid(gate) * up).astype(o_ref.dtype)


def _swiglu(x16, wg, wu, l, tm, tn):
    M, K = x16.shape
    return pl.pallas_call(
        _swiglu_kernel,
        out_shape=jax.ShapeDtypeStruct((M, D_FF), BF16),
        grid=(D_FF // tn, M // tm),
        in_specs=[pl.BlockSpec((tm, K), lambda j, i: (i, 0)),
                  pl.BlockSpec((None, K, tn), lambda j, i: (l, 0, j)),
                  pl.BlockSpec((None, K, tn), lambda j, i: (l, 0, j))],
        out_specs=pl.BlockSpec((tm, tn), lambda j, i: (i, j)),
        compiler_params=_cparams(("parallel", "arbitrary")),
    )(x16, wg, wu)


def _flash_kernel(*refs, groups, dk, scale, has_mask, diff, T):
    qi_ref, kj_ref = refs[:2]
    if has_mask:
        q_ref, k_ref, v_ref, bias_ref, mask_ref, lam_ref, gain_ref, o_ref, m_sc, l_sc, acc_sc = refs[2:]
    else:
        q_ref, k_ref, v_ref, bias_ref, lam_ref, gain_ref, o_ref, m_sc, l_sc, acc_sc = refs[2:]
    i = qi_ref[pl.program_id(2)]
    j = kj_ref[pl.program_id(2)]
    RB = FLASH_ROW_BLOCK

    @pl.when(j == 0)
    def _():
        m_sc[...] = jnp.full(m_sc.shape, -jnp.inf, F32)
        l_sc[...] = jnp.zeros_like(l_sc)
        acc_sc[...] = jnp.zeros_like(acc_sc)

    def step(diagonal):
        q = (q_ref[...] * (scale * LOG2E)).astype(BF16)
        k = k_ref[...].astype(BF16)
        v_ext = jnp.concatenate([v_ref[...].astype(BF16), jnp.ones((T, LANES), BF16)], axis=1)
        for r in range(T // RB):
            rows = slice(r * RB, (r + 1) * RB)
            nc = (r + 1) * RB if diagonal else T
            bias = bias_ref[rows, 0:nc]
            if has_mask:
                bias = bias + mask_ref[rows, 0:nc].astype(F32)
            for g in range(groups):
                s = _dot_nt(q[rows, g * dk:(g + 1) * dk], k[0:nc, g * dk:(g + 1) * dk]) + bias
                m_prev = m_sc[g, rows, :]
                m_new = jnp.maximum(m_prev, jnp.max(s, axis=1, keepdims=True))
                p = jnp.exp2(s - jnp.concatenate([m_new] * (nc // LANES), axis=1))
                a = jnp.exp2(m_prev - m_new)
                pv = _dot(p.astype(BF16), v_ext[0:nc])
                acc_sc[g, rows, :] = a * acc_sc[g, rows, :] + pv[:, :LANES]
                l_sc[g, rows, :] = a * l_sc[g, rows, :] + pv[:, LANES:]
                m_sc[g, rows, :] = m_new

    @pl.when(j < i)
    def _():
        step(False)

    @pl.when(j == i)
    def _():
        step(True)
        if diff:
            o = acc_sc[0] / l_sc[0] - lam_ref[...] * (acc_sc[1] / l_sc[1])
            ms = jnp.mean(o * o, axis=1, keepdims=True)
            o = o * lax.rsqrt(ms + LN_EPS) * gain_ref[...]
        else:
            o = acc_sc[0] / l_sc[0]
        o_ref[...] = o


def _flash_prompt(p8, slabs, bias_tab, mask, lam_row, gain_row, *, B, L, T, groups, dk, scale, diff):
    nq = L // T
    H = 4
    has_mask = mask is not None
    sq, sk, sv = slabs
    pairs = [(i, j) for i in range(nq) for j in range(i + 1)]
    qi = jnp.asarray([p[0] for p in pairs], jnp.int32)
    kj = jnp.asarray([p[1] for p in pairs], jnp.int32)
    qspec = pl.BlockSpec((None, T, LANES), lambda b, h, t, qi, kj: (sq, b * nq + qi[t], h))
    kspec = pl.BlockSpec((None, T, LANES), lambda b, h, t, qi, kj: (sk, b * nq + kj[t], h))
    vspec = pl.BlockSpec((None, T, LANES), lambda b, h, t, qi, kj: (sv, b * nq + kj[t], h))
    bspec = pl.BlockSpec((None, None, T, T),
                         lambda b, h, t, qi, kj: (h, jnp.minimum(qi[t] - kj[t], 2), 0, 0))
    rowspec = pl.BlockSpec((1, LANES), lambda b, h, t, qi, kj: (0, 0))
    in_specs = [qspec, kspec, vspec, bspec]
    args = [p8, p8, p8, bias_tab]
    if has_mask:
        in_specs.append(pl.BlockSpec((T, T), lambda b, h, t, qi, kj: (b * nq + qi[t], kj[t])))
        args.append(mask)
    in_specs += [rowspec, rowspec]
    args += [lam_row, gain_row]
    grid_spec = pltpu.PrefetchScalarGridSpec(
        num_scalar_prefetch=2,
        grid=(B, H, len(pairs)),
        in_specs=in_specs,
        out_specs=pl.BlockSpec((T, LANES), lambda b, h, t, qi, kj: (b * nq + qi[t], h)),
        scratch_shapes=[pltpu.VMEM((groups, T, LANES), F32), pltpu.VMEM((groups, T, LANES), F32),
                        pltpu.VMEM((groups, T, LANES), F32)],
    )
    return pl.pallas_call(
        functools.partial(_flash_kernel, groups=groups, dk=dk, scale=scale, has_mask=has_mask, diff=diff, T=T),
        out_shape=jax.ShapeDtypeStruct((B * L, H * LANES), F32),
        grid_spec=grid_spec,
        compiler_params=_cparams(("parallel", "parallel", "arbitrary")),
    )(qi, kj, *args)


def _sortable(x):
    i = lax.bitcast_convert_type(x, jnp.int32)
    return jnp.where(i < 0, i ^ jnp.int32(0x7FFFFFFF), i)


def _topk_bias(key_ref, o_ref, topk, n_chunks, cw):
    R = key_ref.shape[0]
    kf = float(topk)
    nl = cw // LANES

    def chunk_at(c):
        return pl.ds(pl.multiple_of(c * cw, LANES), cw)

    def count(pred):
        def body(c, acc):
            hit = jnp.where(pred(key_ref[:, chunk_at(c)]), 1.0, 0.0)
            parts = [hit[:, t * LANES:(t + 1) * LANES] for t in range(nl)]
            while len(parts) > 1:
                parts = [a + b for a, b in zip(parts[::2], parts[1::2])] + parts[len(parts) & ~1:]
            return acc + parts[0]
        acc = lax.fori_loop(0, n_chunks, body, jnp.zeros((R, LANES), F32))
        return jnp.sum(acc, axis=1, keepdims=True)

    base = jnp.where(count(lambda k: k >= 0) >= kf, 0, INT_MIN).astype(jnp.int32)

    def bit_step(it, base):
        cand = base | jnp.left_shift(jnp.int32(1), 30 - it)
        return jnp.where(count(lambda k: k >= cand) >= kf, cand, base)

    base = lax.fori_loop(0, 31, bit_step, base)
    has_excess = jnp.max(count(lambda k: k >= base)) > kf

    @pl.when(jnp.logical_not(has_excess))
    def _():
        def write(c, carry):
            k = key_ref[:, chunk_at(c)]
            sel = (k >= base) & (k > KEY_NEG_INF)
            o_ref[:, chunk_at(c)] = jnp.where(sel, 0.0, NEG_BIAS).astype(o_ref.dtype)
            return carry
        lax.fori_loop(0, n_chunks, write, 0)

    @pl.when(has_excess)
    def _():
        need = kf - count(lambda k: k > base)
        before = (lax.broadcasted_iota(jnp.int32, (LANES, LANES), 0)
                  < lax.broadcasted_iota(jnp.int32, (LANES, LANES), 1))
        before = jnp.where(before, 1.0, 0.0)

        def lane_tile(c, seen):
            sl = pl.ds(pl.multiple_of(c * LANES, LANES), LANES)
            k = key_ref[:, sl]
            eq = k == base
            eqf = jnp.where(eq, 1.0, 0.0)
            rank = _dot(eqf, before) + seen
            sel = ((k > base) | (eq & (rank < need))) & (k > KEY_NEG_INF)
            o_ref[:, sl] = jnp.where(sel, 0.0, NEG_BIAS).astype(o_ref.dtype)
            return seen + jnp.sum(eqf, axis=1, keepdims=True)

        lax.fori_loop(0, n_chunks * nl, lane_tile, jnp.zeros((R, 1), F32))


def _topk_bias_t(key_ref, o_ref, topk, n_chunks, kc):
    Q = key_ref.shape[1]
    kf = float(topk)
    SUB = 8

    def rows_at(c):
        return pl.ds(pl.multiple_of(c * kc, kc), kc)

    def count(pred):
        def body(c, acc):
            hit = jnp.where(pred(key_ref[rows_at(c), :]), 1.0, 0.0).reshape(kc // SUB, SUB, Q)
            n = kc // SUB
            while n > 1:
                n //= 2
                hit = hit[:n] + hit[n:2 * n]
            return acc + hit[0]
        acc = lax.fori_loop(0, n_chunks, body, jnp.zeros((SUB, Q), F32))
        return jnp.sum(acc, axis=0, keepdims=True)

    base = jnp.where(count(lambda k: k >= 0) >= kf, 0, INT_MIN).astype(jnp.int32)

    def bit_step(it, base):
        cand = base | jnp.left_shift(jnp.int32(1), 30 - it)
        return jnp.where(count(lambda k: k >= cand) >= kf, cand, base)

    base = lax.fori_loop(0, 31, bit_step, base)
    has_excess = jnp.max(count(lambda k: k >= base)) > kf

    def store(c, sel):
        bias_t = jnp.where(sel, 0.0, NEG_BIAS).astype(o_ref.dtype)
        o_ref[:, rows_at(c)] = jnp.transpose(bias_t)

    @pl.when(jnp.logical_not(has_excess))
    def _():
        def write(c, carry):
            k = key_ref[rows_at(c), :]
            store(c, (k >= base) & (k > KEY_NEG_INF))
            return carry
        lax.fori_loop(0, n_chunks, write, 0)

    @pl.when(has_excess)
    def _():
        need = kf - count(lambda k: k > base)
        earlier = (lax.broadcasted_iota(jnp.int32, (kc, kc), 1)
                   < lax.broadcasted_iota(jnp.int32, (kc, kc), 0))
        earlier = jnp.where(earlier, 1.0, 0.0)

        def write(c, seen):
            k = key_ref[rows_at(c), :]
            eq = k == base
            eqf = jnp.where(eq, 1.0, 0.0)
            rank = _dot(earlier, eqf) + seen
            store(c, ((k > base) | (eq & (rank < need))) & (k > KEY_NEG_INF))
            return seen + jnp.sum(eqf, axis=0, keepdims=True)

        lax.fori_loop(0, n_chunks, write, jnp.zeros((1, Q), F32))


def _idx_prompt_kernel(iq_ref, ik_ref, iwt_ref, o_ref, key_sc, *, tq, kc, topk):
    i = pl.program_id(1)
    iq = iq_ref[...]
    lhs = jnp.concatenate([iq[:, h * D_IDX:(h + 1) * D_IDX] for h in range(H_IDX)], axis=0).astype(BF16)
    iwt = iwt_ref[...]
    qpos = i * tq + lax.broadcasted_iota(jnp.int32, (1, tq), 1)
    n_need = (i * tq + tq + kc - 1) // kc
    o_ref[...] = jnp.full(o_ref.shape, NEG_BIAS, o_ref.dtype)

    def score_chunk(c, carry):
        off = pl.multiple_of(c * kc, kc)
        ik = ik_ref[pl.ds(off, kc), 0:D_IDX].astype(BF16)
        s = _dot_nt(ik, lhs)
        score = jnp.zeros((kc, tq), F32)
        for h in range(H_IDX):
            score = score + jnp.maximum(s[:, h * tq:(h + 1) * tq], 0.0) * iwt[h:h + 1, :]
        kpos = off + lax.broadcasted_iota(jnp.int32, (kc, 1), 0)
        score = jnp.where(kpos <= qpos, score, -jnp.inf)
        key_sc[pl.ds(off, kc), :] = _sortable(score)
        return carry

    lax.fori_loop(0, n_need, score_chunk, 0)
    _topk_bias_t(key_sc, o_ref, topk, n_need, kc)


def _idx_prompt(p8, tail, *, B, L, tq, topk):
    nq = L // tq
    nmisc = TAIL_MISC // LANES
    return pl.pallas_call(
        functools.partial(_idx_prompt_kernel, tq=tq, kc=min(L, 512), topk=topk),
        out_shape=jax.ShapeDtypeStruct((B * L, L), BF16),
        grid=(B, nq),
        in_specs=[pl.BlockSpec((None, tq, W_GROUP), lambda b, i: (7, b * nq + i, 0)),
                  pl.BlockSpec((L, LANES), lambda b, i: (b, nmisc)),
                  pl.BlockSpec((H_IDX, tq), lambda b, i: (0, b * nq + i))],
        out_specs=pl.BlockSpec((tq, L), lambda b, i: (b * nq + i, 0)),
        scratch_shapes=[pltpu.VMEM((L, tq), jnp.int32)],
        compiler_params=_cparams(("parallel", "arbitrary")),
    )(p8, tail, jnp.transpose(tail[:, TAIL_MISC + D_IDX:TAIL_MISC + D_IDX + H_IDX]))


def _gelu_tanh(x):
    return 0.5 * x * (1.0 + jnp.tanh(math.sqrt(2.0 / math.pi) * (x + 0.044715 * (x * x * x))))


def _s5_kernel(u_ref, h0r_ref, h0i_ref, bre_ref, bim_ref, apr_ref, api_ref, pwr_ref, pwi_ref,
               cre_ref, cim_ref, d_ref, wglu_ref, y_ref, hr_ref, hi_ref, cr_sc, ci_sc, hsr_sc, hsi_sc, *, T):
    t = pl.program_id(1)

    @pl.when(t == 0)
    def _():
        cr_sc[...] = h0r_ref[...]
        ci_sc[...] = h0i_ref[...]

    u = u_ref[...]
    ub = u.astype(BF16)
    G = T // S5_GROUP
    hr = _dot(ub, bre_ref[...]).reshape(G, S5_GROUP, NSTATE)
    hi = _dot(ub, bim_ref[...]).reshape(G, S5_GROUP, NSTATE)
    for j in range(S5_GROUP.bit_length() - 1):
        sr = pltpu.roll(hr, 1 << j, 1)
        si = pltpu.roll(hi, 1 << j, 1)
        ar = apr_ref[j]
        ai = api_ref[j]
        hr, hi = hr + (ar * sr - ai * si), hi + (ar * si + ai * sr)
    cr = cr_sc[...]
    ci = ci_sc[...]
    pr = pwr_ref[...]
    pi_ = pwi_ref[...]
    for g in range(G):
        rows = slice(g * S5_GROUP, (g + 1) * S5_GROUP)
        gr = hr[g] + (pr * cr - pi_ * ci)
        gi = hi[g] + (pr * ci + pi_ * cr)
        hsr_sc[rows, :] = gr
        hsi_sc[rows, :] = gi
        cr = gr[S5_GROUP - 1:S5_GROUP, :]
        ci = gi[S5_GROUP - 1:S5_GROUP, :]
    cr_sc[...] = cr
    ci_sc[...] = ci
    hr_ref[...] = cr
    hi_ref[...] = ci
    y = (_dot(hsr_sc[...].astype(BF16), cre_ref[...]) - _dot(hsi_sc[...].astype(BF16), cim_ref[...])
         + d_ref[...] * u)
    y = _gelu_tanh(y)
    gu = _dot(y.astype(BF16), wglu_ref[...])
    y_ref[...] = gu[:, :W_GROUP] * jax.nn.sigmoid(gu[:, W_GROUP:])


def _s5(p8, h0r, h0i, tabs, l, *, B, L, T):
    nt = L // T
    bre, bim, apr, api, pwr, pwi, cre, cim, dsk, wglu = tabs
    lay = lambda *shape: pl.BlockSpec((None,) + shape, lambda b, t: (l,) + (0,) * len(shape))
    st = pl.BlockSpec((None, 1, NSTATE), lambda b, t: (b, 0, 0))
    return pl.pallas_call(
        functools.partial(_s5_kernel, T=T),
        out_shape=(jax.ShapeDtypeStruct((B * L, W_GROUP), F32),
                   jax.ShapeDtypeStruct((B, 1, NSTATE), F32), jax.ShapeDtypeStruct((B, 1, NSTATE), F32)),
        grid=(B, nt),
        in_specs=[pl.BlockSpec((None, T, W_GROUP), lambda b, t: (3, b * nt + t, 0)), st, st,
                  lay(W_GROUP, NSTATE), lay(W_GROUP, NSTATE), lay(3, S5_GROUP, NSTATE), lay(3, S5_GROUP, NSTATE),
                  lay(S5_GROUP, NSTATE), lay(S5_GROUP, NSTATE), lay(NSTATE, W_GROUP), lay(NSTATE, W_GROUP),
                  lay(1, W_GROUP), lay(W_GROUP, 2 * W_GROUP)],
        out_specs=(pl.BlockSpec((T, W_GROUP), lambda b, t: (b * nt + t, 0)), st, st),
        scratch_shapes=[pltpu.VMEM((1, NSTATE), F32), pltpu.VMEM((1, NSTATE), F32),
                        pltpu.VMEM((T, NSTATE), F32), pltpu.VMEM((T, NSTATE), F32)],
        compiler_params=_cparams(("parallel", "arbitrary")),
    )(p8, h0r, h0i, bre, bim, apr, api, pwr, pwi, cre, cim, dsk, wglu)


def _log_sigmoid(x):
    return jnp.minimum(x, 0.0) - jnp.log1p(jnp.exp(-jnp.abs(x)))


def _gla_kernel(q_ref, k_ref, v_ref, r_ref, misc_ref, wg_ref, bg_ref, ng_ref, s0_ref,
                y_ref, sfin_ref, s_sc, *, TB, CH, nt, valid_len):
    t = pl.program_id(1)
    HK = H_D * DK_D
    HV = H_D * DV_D

    @pl.when(t == 0)
    def _():
        s_sc[...] = jnp.zeros_like(s_sc)
        for h in range(H_D):
            s_sc[h * DK_D:(h + 1) * DK_D, h * DV_D:(h + 1) * DV_D] = s0_ref[h]

    row = lax.broadcasted_iota(jnp.int32, (TB, 1), 0)
    glow = misc_ref[:, D_IDX + H_IDX:D_IDX + H_IDX + GATE_RANK]
    logit = _dot(glow.astype(BF16), wg_ref[...].astype(BF16)) + bg_ref[...]
    la = _log_sigmoid(logit) * (1.0 / GATE_TAU)
    k = k_ref[...]
    if valid_len < TB:
        la = jnp.where(row < valid_len, la, 0.0)
        k = jnp.where(row < valid_len, k, 0.0)
    ri = lax.broadcasted_iota(jnp.int32, (TB, TB), 0)
    ci = lax.broadcasted_iota(jnp.int32, (TB, TB), 1)
    same = (ri // CH) == (ci // CH)
    lower = same & (ci <= ri)
    tri = jnp.where(lower, 1.0, 0.0).astype(BF16)
    blk = jnp.where(same, 1.0, 0.0).astype(BF16)
    la_hi = la.astype(BF16)
    la_lo = (la - la_hi.astype(F32)).astype(BF16)
    bc = _dot(tri, la_hi) + _dot(tri, la_lo)
    bl = _dot(blk, la_hi) + _dot(blk, la_lo)
    q = q_ref[...] * (DK_D ** -0.5)
    v = v_ref[...]
    vb = v.astype(BF16)
    qt = (q * jnp.exp(bc)).astype(BF16)
    kc = (k * jnp.exp(-bc)).astype(BF16)
    kh_t = jnp.transpose(k * jnp.exp(bl - bc))
    dec_t = jnp.transpose(jnp.exp(bl))

    o_parts = []
    for h in range(H_D):
        att = _dot_nt(qt[:, h * DK_D:(h + 1) * DK_D], kc[:, h * DK_D:(h + 1) * DK_D])
        att = jnp.where(lower, att, 0.0)
        o_parts.append(_dot(att.astype(BF16), vb[:, h * DV_D:(h + 1) * DV_D]))
    o_intra = jnp.concatenate(o_parts, axis=1)

    head_blk = (lax.broadcasted_iota(jnp.int32, (HK, HV), 0) // DK_D
                == lax.broadcasted_iota(jnp.int32, (HK, HV), 1) // DV_D)
    col = lax.broadcasted_iota(jnp.int32, (1, TB), 1)
    S = s_sc[...]
    o_rows = []
    for c in range(TB // CH):
        o_rows.append(_dot(qt[c * CH:(c + 1) * CH, :], S.astype(BF16)))
        in_chunk = (col >= c * CH) & (col < (c + 1) * CH)
        kv = _dot(jnp.where(in_chunk, kh_t, 0.0).astype(BF16), vb)
        S = dec_t[:, c * CH:c * CH + 1] * S + jnp.where(head_blk, kv, 0.0)
    s_sc[...] = S
    o = o_intra + jnp.concatenate(o_rows, axis=0)

    outs = []
    for h in range(H_D):
        oh = o[:, h * DV_D:(h + 1) * DV_D]
        ms = jnp.mean(oh * oh, axis=1, keepdims=True)
        outs.append(oh * lax.rsqrt(ms + LN_EPS) * ng_ref[...])
    r = r_ref[...]
    y_ref[...] = jnp.concatenate(outs, axis=1) * (r * jax.nn.sigmoid(r))

    @pl.when(t == nt - 1)
    def _():
        for h in range(H_D):
            sfin_ref[h] = S[h * DK_D:(h + 1) * DK_D, h * DV_D:(h + 1) * DV_D]


def _gla(tail, s0, w_gate, b_gate, norm_g, l, *, B, L, TB, valid_len):
    nt = L // TB
    lay = lambda *shape: pl.BlockSpec((None,) + shape, lambda b, t: (l,) + (0,) * len(shape))
    sspec = pl.BlockSpec((None, H_D, DK_D, DV_D), lambda b, t: (b, 0, 0, 0))
    return pl.pallas_call(
        functools.partial(_gla_kernel, TB=TB, CH=GLA_CHUNK, nt=nt, valid_len=valid_len),
        out_shape=(jax.ShapeDtypeStruct((B * L, W_GROUP), F32),
                   jax.ShapeDtypeStruct((B, H_D, DK_D, DV_D), F32)),
        grid=(B, nt),
        in_specs=[pl.BlockSpec((TB, 256), lambda b, t: (b * nt + t, 0)),
                  pl.BlockSpec((TB, 256), lambda b, t: (b * nt + t, 1)),
                  pl.BlockSpec((TB, 512), lambda b, t: (b * nt + t, 1)),
                  pl.BlockSpec((TB, 512), lambda b, t: (b * nt + t, 2)),
                  pl.BlockSpec((TB, LANES), lambda b, t: (b * nt + t, TAIL_MISC // LANES)),
                  lay(GATE_RANK, H_D * DK_D), lay(1, H_D * DK_D), lay(1, DV_D), sspec],
        out_specs=(pl.BlockSpec((TB, W_GROUP), lambda b, t: (b * nt + t, 0)), sspec),
        scratch_shapes=[pltpu.VMEM((H_D * DK_D, H_D * DV_D), F32)],
        compiler_params=_cparams(("parallel", "arbitrary")),
    )(tail, tail, tail, tail, tail, w_gate, b_gate, norm_g, s0)


def _sidx_kernel(pt_ref, iq_ref, iw_ref, *rest, PPS, NS, n_new, topk):
    page_refs = rest[:PPS]
    new_ref, o_ref, key_sc = rest[PPS:]
    s_id = pl.program_id(1)
    iq = iq_ref[...].astype(BF16)
    iw = iw_ref[...]

    def scores(keys_t):
        hs = jnp.maximum(_dot(iq, keys_t.astype(BF16)), 0.0) * iw
        sc = jnp.zeros((n_new, LANES), F32)
        for h in range(H_IDX):
            sc = sc + hs[h * n_new:(h + 1) * n_new]
        return sc

    for p in range(PPS):
        off = pl.multiple_of((s_id * PPS + p) * LANES, LANES)
        key_sc[:, pl.ds(off, LANES)] = _sortable(scores(page_refs[p][...]))

    @pl.when(s_id == NS - 1)
    def _():
        sc = scores(new_ref[...])
        qi = lax.broadcasted_iota(jnp.int32, (n_new, LANES), 0)
        kj = lax.broadcasted_iota(jnp.int32, (n_new, LANES), 1)
        sc = jnp.where(kj <= qi, sc, -jnp.inf)
        key_sc[:, NS * PPS * LANES:(NS * PPS + 1) * LANES] = _sortable(sc)
        n_tiles = NS * PPS + 1
        n_chunks = 3 if n_tiles % 3 == 0 else 1
        _topk_bias(key_sc, o_ref, topk, n_chunks, n_tiles // n_chunks * LANES)


def _sidx(page_table, iq_rows, iw_tile, cache_idx, idx_new, l, *, PPS, topk):
    Bs, n_pages = page_table.shape
    NS = n_pages // PPS
    n_new = iq_rows.shape[1] // H_IDX
    W = (n_pages + 1) * LANES

    def page_spec(p):
        return pl.BlockSpec((None, None, D_IDX, PAGE_SIZE),
                            lambda b, s, pt: (l, pt[b, s * PPS + p], 0, 0))

    grid_spec = pltpu.PrefetchScalarGridSpec(
        num_scalar_prefetch=1,
        grid=(Bs, NS),
        in_specs=[pl.BlockSpec((None, H_IDX * n_new, D_IDX), lambda b, s, pt: (b, 0, 0)),
                  pl.BlockSpec((None, H_IDX * n_new, LANES), lambda b, s, pt: (b, 0, 0))]
                 + [page_spec(p) for p in range(PPS)]
                 + [pl.BlockSpec((None, D_IDX, PAGE_SIZE), lambda b, s, pt: (b, 0, 0))],
        out_specs=pl.BlockSpec((None, n_new, W), lambda b, s, pt: (b, 0, 0)),
        scratch_shapes=[pltpu.VMEM((n_new, W), jnp.int32)],
    )
    return pl.pallas_call(
        functools.partial(_sidx_kernel, PPS=PPS, NS=NS, n_new=n_new, topk=topk),
        out_shape=jax.ShapeDtypeStruct((Bs, n_new, W), F32),
        grid_spec=grid_spec,
        compiler_params=_cparams(("parallel", "arbitrary")),
    )(page_table, iq_rows, iw_tile, *([cache_idx] * PPS), idx_new)


def _sattn_kernel(pt_ref, q_ref, *rest, PPS, NS, R, n_new, has_mask, diff):
    k_refs = rest[:PPS]
    v_refs = rest[PPS:2 * PPS]
    rest = rest[2 * PPS:]
    knew_ref, vnew_ref, far_ref, near_ref = rest[:4]
    rest = rest[4:]
    if has_mask:
        mask_ref, masknew_ref, spread_ref = rest[:3]
        rest = rest[3:]
    lam_ref, gain_ref, o_ref, m_sc, l_sc, acc_sc = rest
    s_id = pl.program_id(1)

    @pl.when(s_id == 0)
    def _():
        m_sc[...] = jnp.full(m_sc.shape, -jnp.inf, F32)
        l_sc[...] = jnp.zeros_like(l_sc)
        acc_sc[...] = jnp.zeros_like(acc_sc)

    q = q_ref[...].astype(BF16)
    PR = PAGE_SIZE * 4

    def expand(m):
        return jnp.concatenate([_dot(m, spread_ref[...])] * (R // n_new), axis=0)

    def pages(kps, vps, biases):
        ss = [_dot_nt(q, kp.astype(BF16)) + b for kp, b in zip(kps, biases)]
        top = ss[0]
        for s in ss[1:]:
            top = jnp.maximum(top, s)
        m_prev = m_sc[...]
        m_new = jnp.maximum(m_prev, jnp.max(top, axis=1, keepdims=True))
        a = jnp.exp2(m_prev - m_new)
        ps = [jnp.exp2(s - m_new) for s in ss]
        tot = ps[0]
        for p in ps[1:]:
            tot = tot + p
        pv = _dot(ps[0].astype(BF16), vps[0].astype(BF16))
        for p, vp in zip(ps[1:], vps[1:]):
            pv = pv + _dot(p.astype(BF16), vp.astype(BF16))
        l_sc[...] = a * l_sc[...] + jnp.sum(tot, axis=1, keepdims=True)
        acc_sc[...] = a * acc_sc[...] + pv
        m_sc[...] = m_new

    far = far_ref[...]
    biases = []
    for p in range(PPS):
        bias = far
        if p == PPS - 1:
            bias = jnp.where(s_id == NS - 1, near_ref[:, 0:PR], far)
        if has_mask:
            bias = bias + expand(mask_ref[:, p * LANES:(p + 1) * LANES])
        biases.append(bias)
    pages([r[...] for r in k_refs], [r[...] for r in v_refs], biases)

    @pl.when(s_id == NS - 1)
    def _():
        bias = near_ref[:, PR:2 * PR]
        if has_mask:
            bias = bias + expand(masknew_ref[...])
        pages([knew_ref[...]], [vnew_ref[...]], [bias])
        o_all = acc_sc[...] / l_sc[...]
        outs = []
        for h in range(4):
            if diff:
                o1 = o_all[(2 * h) * n_new:(2 * h + 1) * n_new]
                o2 = o_all[(2 * h + 1) * n_new:(2 * h + 2) * n_new]
                o = o1 - lam_ref[...] * o2
                ms = jnp.mean(o * o, axis=1, keepdims=True)
                outs.append(o * lax.rsqrt(ms + LN_EPS) * gain_ref[...])
            else:
                outs.append(o_all[h * n_new:(h + 1) * n_new])
        o_ref[...] = jnp.concatenate(outs, axis=1)


def _sattn(page_table, q_rows, cache_k, cache_v, knew, vnew, far, near, mask, lam_row, gain_row, l,
           *, PPS, diff):
    Bs, n_pages = page_table.shape
    NS = n_pages // PPS
    R = q_rows.shape[1]
    n_new = 8
    PR = PAGE_SIZE * 4
    has_mask = mask is not None

    def page_spec(p):
        return pl.BlockSpec((None, None, PR, LANES), lambda b, s, pt: (l, pt[b, s * PPS + p], 0, 0))

    per_b = lambda *shape: pl.BlockSpec((None,) + shape, lambda b, s, pt: (b,) + (0,) * len(shape))
    const = lambda *shape: pl.BlockSpec(shape, lambda b, s, pt: (0,) * len(shape))
    in_specs = ([per_b(R, LANES)] + [page_spec(p) for p in range(PPS)] * 2
                + [per_b(PR, LANES), per_b(PR, LANES), const(R, PR), const(R, 2 * PR)])
    args = [q_rows] + [cache_k] * PPS + [cache_v] * PPS + [knew, vnew, far, near]
    if has_mask:
        spread = (jnp.arange(PR, dtype=jnp.int32)[None, :] // 4
                  == jnp.arange(PAGE_SIZE, dtype=jnp.int32)[:, None]).astype(F32)
        in_specs += [pl.BlockSpec((None, n_new, PPS * LANES), lambda b, s, pt: (b, 0, s)),
                     pl.BlockSpec((None, n_new, LANES), lambda b, s, pt: (b, 0, n_pages)),
                     const(PAGE_SIZE, PR)]
        args += [mask, mask, spread]
    in_specs += [const(1, LANES), const(1, LANES)]
    args += [lam_row, gain_row]
    grid_spec = pltpu.PrefetchScalarGridSpec(
        num_scalar_prefetch=1,
        grid=(Bs, NS),
        in_specs=in_specs,
        out_specs=pl.BlockSpec((None, n_new, W_GROUP), lambda b, s, pt: (b, 0, 0)),
        scratch_shapes=[pltpu.VMEM((R, 1), F32), pltpu.VMEM((R, 1), F32), pltpu.VMEM((R, LANES), F32)],
    )
    return pl.pallas_call(
        functools.partial(_sattn_kernel, PPS=PPS, NS=NS, R=R, n_new=n_new,
                          has_mask=has_mask, diff=diff),
        out_shape=jax.ShapeDtypeStruct((Bs, n_new, W_GROUP), F32),
        grid_spec=grid_spec,
        compiler_params=_cparams(("parallel", "arbitrary")),
    )(page_table, *args)


def _rel_bucket(dist):
    n = jnp.maximum(dist, 0)
    nf = jnp.maximum(n, REL_MAX_EXACT).astype(F32)
    large = REL_MAX_EXACT + (jnp.log(nf / REL_MAX_EXACT) / math.log(REL_MAX_DIST / REL_MAX_EXACT)
                             * (REL_BUCKETS - REL_MAX_EXACT)).astype(jnp.int32)
    large = jnp.minimum(large, REL_BUCKETS - 1)
    return jnp.where(n < REL_MAX_EXACT, n, large)


def _rel_lookup(rel, bucket):
    out = jnp.zeros((rel.shape[1],) + bucket.shape, F32)
    for b in range(REL_BUCKETS):
        out = jnp.where(bucket[None] == b, rel[b].reshape((-1,) + (1,) * bucket.ndim), out)
    return out * LOG2E


def _prompt_bias_tables(rel, T):
    assert T >= REL_MAX_DIST
    r = jnp.arange(T, dtype=jnp.int32)[:, None]
    c = jnp.arange(T, dtype=jnp.int32)[None, :]
    d0 = r - c
    t0 = jnp.where((d0 >= 0)[None], _rel_lookup(rel, _rel_bucket(d0)), NEG_BIAS)
    t1 = _rel_lookup(rel, _rel_bucket(T + r - c))
    t2 = _rel_lookup(rel, _rel_bucket(jnp.full((T, T), 2 * T, jnp.int32)))
    return jnp.stack([t0, t1, t2], axis=1)


def _sample_bias_tables(rel, n_new, reps):
    H = rel.shape[1]
    t = jnp.arange(n_new, dtype=jnp.int32)[:, None]
    c = jnp.arange(PAGE_SIZE, dtype=jnp.int32)[None, :]
    last = _rel_lookup(rel, _rel_bucket(PAGE_SIZE + t - c))
    dn = t - c
    new = jnp.where(((dn >= 0) & (c < n_new))[None], _rel_lookup(rel, _rel_bucket(dn)), NEG_BIAS)
    far = _rel_lookup(rel, _rel_bucket(jnp.full((n_new, PAGE_SIZE), 2 * PAGE_SIZE, jnp.int32)))

    def rows(x):
        own = jnp.eye(H, dtype=bool)[:, None, None, :]
        y = jnp.where(own, x[..., None], NEG_BIAS).reshape(H, 1, n_new, PAGE_SIZE * H)
        return jnp.broadcast_to(y, (H, reps, n_new, PAGE_SIZE * H)).reshape(H * reps * n_new, PAGE_SIZE * H)

    return rows(far), jnp.concatenate([rows(last), rows(new)], axis=1)


def _s5_tables(a_re, a_im, log_dt, b_re, b_im, c_re, c_im, d_skip, w_glu):
    dt = jnp.exp(log_dt.astype(F32))[..., None]
    lam_re, lam_im = a_re.astype(F32), a_im.astype(F32)
    z_re, z_im = lam_re * dt, lam_im * dt
    mag = jnp.exp(z_re)
    e_re, e_im = mag * jnp.cos(z_im), mag * jnp.sin(z_im)
    den = lam_re * lam_re + lam_im * lam_im
    f_re = ((e_re - 1.0) * lam_re + e_im * lam_im) / den
    f_im = (e_im * lam_re - (e_re - 1.0) * lam_im) / den
    br, bi = b_re.astype(F32), b_im.astype(F32)
    bb_re = f_re[..., None] * br - f_im[..., None] * bi
    bb_im = f_re[..., None] * bi + f_im[..., None] * br
    eye = jnp.eye(G_B, dtype=F32)
    bd_in = lambda bb: jnp.einsum('lgpc,gh->lgchp', bb, eye).reshape(DEPTH, W_GROUP, NSTATE).astype(BF16)
    bd_out = lambda cc: jnp.einsum('lgcp,gh->lgphc', cc.astype(F32), eye).reshape(DEPTH, NSTATE, W_GROUP).astype(BF16)

    def powers(ts):
        tt = ts[None, :, None, None]
        pm = jnp.exp(z_re[:, None] * tt)
        return ((pm * jnp.cos(z_im[:, None] * tt)).reshape(DEPTH, -1, NSTATE),
                (pm * jnp.sin(z_im[:, None] * tt)).reshape(DEPTH, -1, NSTATE))

    n_steps = S5_GROUP.bit_length() - 1
    apr, api = powers(jnp.asarray([2.0 ** j for j in range(n_steps)], F32))
    live = (jnp.arange(S5_GROUP)[None, :] >= (2 ** jnp.arange(n_steps))[:, None]).astype(F32)
    apr = apr[:, :, None, :] * live[None, :, :, None]
    api = api[:, :, None, :] * live[None, :, :, None]
    pwr, pwi = powers(jnp.arange(1, S5_GROUP + 1, dtype=F32))
    return (bd_in(bb_re), bd_in(bb_im), apr, api, pwr, pwi, bd_out(c_re), bd_out(c_im),
            d_skip.astype(F32).reshape(DEPTH, 1, W_GROUP), w_glu.astype(BF16))


def _tail_weights(w_in_t):
    pad = jnp.zeros((w_in_t.shape[0], N_TAIL - 1624, w_in_t.shape[2]), w_in_t.dtype)
    return jnp.concatenate([w_in_t[:, 4168:5704], w_in_t[:, 4096:4168], w_in_t[:, 5704:5720], pad],
                           axis=1).astype(BF16)


def _channel_mix(ys, x, l, W, tm):
    x1, x1b = _mix_ln(ys, W['w_out'], l, x, W['ln1_g'], W['ln1_b'], tm)
    hid = _swiglu(x1b, W['ffn_w_gate'], W['ffn_w_up'], l, tm, 512)
    return _mm_ln(hid, W['ffn_w_down'], l, x1, W['ln2_g'], W['ln2_b'], tm, D_FF // 4)


def _prompt_layer(x, xb, l, W, B, L):
    T_ATT = 512
    p8, a_k, a_v, c_k, c_v = _proj_main(xb, W['w_in_t'], l, 1024)
    tail = _proj_tail(xb, W['w_tail_t'], l, 512)
    y_a = _flash_prompt(p8, (0, 1, 2), W['tab_a'], None, W['lam_row'][l], W['gain_a'][l],
                        B=B, L=L, T=T_ATT, groups=2, dk=DH_A, scale=DH_A ** -0.5, diff=True)
    zeros_h = jnp.zeros((B, 1, NSTATE), F32)
    y_b, h_re, h_im = _s5(p8, zeros_h, zeros_h, W['s5'], l, B=B, L=L, T=S5_BLOCK)
    mask = _idx_prompt(p8, tail, B=B, L=L, tq=128, topk=min(TOPK_MAX, L // 4))
    y_c = _flash_prompt(p8, (4, 5, 6), W['tab_c'], mask, W['lam_row'][l], W['gain_a'][l],
                        B=B, L=L, T=T_ATT, groups=1, dk=DH_C, scale=DH_C ** -0.5, diff=False)
    y_d, s_fin = _gla(tail, jnp.zeros((B, H_D, DK_D, DV_D), F32), W['gla_w_gate'], W['gla_b_gate'],
                      W['gla_norm'], l, B=B, L=L, TB=128, valid_len=128)
    x, xb = _channel_mix([y_a, y_b, y_c, y_d], x, l, W, 512)
    news = (a_k.reshape(B, L, H_A, 128), a_v.reshape(B, L, H_A, 128),
            c_k.reshape(B, L, H_C, 128), c_v.reshape(B, L, H_C, 128),
            tail[:, TAIL_MISC:TAIL_MISC + D_IDX].reshape(B, L, D_IDX),
            h_re.reshape(B, G_B, P_B), h_im.reshape(B, G_B, P_B), s_fin)
    return x, xb, news


def _sample_layer(x, xb, l, W, C, Bs, Ls):
    M = Bs * Ls
    PPS = 8
    p8, a_k, a_v, c_k, c_v = _proj_main(xb, W['w_in_t'], l, M)
    tail = _proj_tail(xb, W['w_tail_t'], l, M)
    pt = C['page_table']

    def pad_new(z):
        z = z.reshape(Bs, Ls, z.shape[-1])
        return jnp.pad(z, ((0, 0), (0, PAGE_SIZE - Ls), (0, 0)))

    def pad_rows(z):
        z = z.reshape(Bs, Ls * 4, LANES)
        return jnp.pad(z, ((0, 0), (0, PAGE_SIZE * 4 - Ls * 4), (0, 0)))

    qa = p8[0].reshape(Bs, Ls, H_A, 2, DH_A) * (DH_A ** -0.5 * LOG2E)
    q_a = jnp.einsum('bqhwd,wv->bhwqvd', qa, jnp.eye(2, dtype=F32)).reshape(Bs, H_A * 2 * Ls, LANES)
    y_a = _sattn(pt, q_a, C['a_k'], C['a_v'], pad_rows(a_k), pad_rows(a_v), W['far_a'], W['near_a'],
                 None, W['lam_row'][l], W['gain_a'][l], l, PPS=PPS, diff=True)
    y_b, h_re, h_im = _s5(p8, C['b_re'][l].reshape(Bs, 1, NSTATE), C['b_im'][l].reshape(Bs, 1, NSTATE),
                          W['s5'], l, B=Bs, L=Ls, T=Ls)
    misc = tail[:, TAIL_MISC:]
    iq_rows = p8[7].reshape(Bs, Ls, H_IDX, D_IDX).transpose(0, 2, 1, 3).reshape(Bs, H_IDX * Ls, D_IDX)
    iw = misc[:, D_IDX:D_IDX + H_IDX].reshape(Bs, Ls, H_IDX).transpose(0, 2, 1).reshape(Bs, H_IDX * Ls, 1)
    iw_tile = jnp.broadcast_to(iw, (Bs, H_IDX * Ls, LANES))
    mask = _sidx(pt, iq_rows, iw_tile, C['c_idx_t'], pad_new(misc[:, :D_IDX]).transpose(0, 2, 1), l, PPS=PPS,
                 topk=min(TOPK_MAX, (pt.shape[1] * PAGE_SIZE + Ls) // 4))
    q_c = (p8[4].reshape(Bs, Ls, H_C, DH_C) * (DH_C ** -0.5 * LOG2E)).transpose(0, 2, 1, 3)
    y_c = _sattn(pt, q_c.reshape(Bs, H_C * Ls, LANES), C['c_k'], C['c_v'], pad_rows(c_k), pad_rows(c_v),
                 W['far_c'], W['near_c'], mask, W['lam_row'][l], W['gain_a'][l], l, PPS=PPS, diff=False)
    tail_pad = pad_new(tail).reshape(Bs * PAGE_SIZE, N_TAIL)
    y_d, s_fin = _gla(tail_pad, C['d'][l], W['gla_w_gate'], W['gla_b_gate'], W['gla_norm'], l,
                      B=Bs, L=PAGE_SIZE, TB=PAGE_SIZE, valid_len=Ls)
    y_d = y_d.reshape(Bs, PAGE_SIZE, W_GROUP)[:, :Ls].reshape(M, W_GROUP)
    x, xb = _channel_mix([y_a.reshape(M, W_GROUP), y_b, y_c.reshape(M, W_GROUP), y_d], x, l, W, M)
    news = (a_k.reshape(Bs, Ls, H_A, 128), a_v.reshape(Bs, Ls, H_A, 128),
            c_k.reshape(Bs, Ls, H_C, 128), c_v.reshape(Bs, Ls, H_C, 128),
            misc[:, :D_IDX].reshape(Bs, Ls, D_IDX),
            h_re.reshape(Bs, G_B, P_B), h_im.reshape(Bs, G_B, P_B), s_fin)
    return x, xb, news


def kernel(x_prompt, x_sample, cache_a_k, cache_a_v, cache_c_k, cache_c_v, cache_c_idx, state_b_re, state_b_im, state_d, page_table, rel_bias, w_in, w_out, lam_q1, lam_k1, lam_q2, lam_k2, a_subln, s5_a_re, s5_a_im, s5_log_dt, s5_b_re, s5_b_im, s5_c_re, s5_c_im, s5_d, s5_w_glu, gla_w_gate, gla_b_gate, gla_norm, ln1_g, ln1_b, ffn_w_gate, ffn_w_up, ffn_w_down, ln2_g, ln2_b):
    B, L, _ = x_prompt.shape
    Bs, Ls, _ = x_sample.shape
    n_pool = cache_a_k.shape[1]

    lam_init = jnp.asarray([0.8 - 0.6 * math.exp(-0.3 * l) for l in range(DEPTH)], F32)
    lam = (jnp.exp(jnp.sum(lam_q1.astype(F32) * lam_k1.astype(F32), axis=-1))
           - jnp.exp(jnp.sum(lam_q2.astype(F32) * lam_k2.astype(F32), axis=-1)) + lam_init)
    s5_args = (s5_a_re, s5_a_im, s5_log_dt, s5_b_re, s5_b_im, s5_c_re, s5_c_im, s5_d, s5_w_glu)
    rel = rel_bias.astype(F32)
    w_in_t = jnp.swapaxes(w_in, 1, 2).astype(BF16)
    far_a, near_a = _sample_bias_tables(rel[:, :H_A], Ls, 2)
    far_c, near_c = _sample_bias_tables(rel[:, H_A:], Ls, 1)
    W = {
        'w_in_t': w_in_t, 'w_tail_t': _tail_weights(w_in_t), 'w_out': w_out.astype(BF16),
        'ffn_w_gate': ffn_w_gate, 'ffn_w_up': ffn_w_up, 'ffn_w_down': ffn_w_down.astype(BF16),
        'ln1_g': ln1_g.reshape(DEPTH, 1, D_MODEL), 'ln1_b': ln1_b.reshape(DEPTH, 1, D_MODEL),
        'ln2_g': ln2_g.reshape(DEPTH, 1, D_MODEL), 'ln2_b': ln2_b.reshape(DEPTH, 1, D_MODEL),
        'lam_row': jnp.broadcast_to(lam[:, None, None], (DEPTH, 1, LANES)),
        'gain_a': (a_subln.astype(F32) * (1.0 - lam_init)[:, None]).reshape(DEPTH, 1, LANES),
        'tab_a': _prompt_bias_tables(rel[:, :H_A], 512), 'tab_c': _prompt_bias_tables(rel[:, H_A:], 512),
        'far_a': far_a, 'near_a': near_a, 'far_c': far_c, 'near_c': near_c,
        's5': _s5_tables(*s5_args),
        'gla_w_gate': gla_w_gate, 'gla_b_gate': gla_b_gate.reshape(DEPTH, 1, H_D * DK_D),
        'gla_norm': gla_norm.reshape(DEPTH, 1, DV_D),
    }
    C = {
        'page_table': page_table,
        'a_k': cache_a_k.reshape(DEPTH, n_pool, PAGE_SIZE * H_A, LANES),
        'a_v': cache_a_v.reshape(DEPTH, n_pool, PAGE_SIZE * H_A, LANES),
        'c_k': cache_c_k.reshape(DEPTH, n_pool, PAGE_SIZE * H_C, LANES),
        'c_v': cache_c_v.reshape(DEPTH, n_pool, PAGE_SIZE * H_C, LANES),
        'c_idx_t': jnp.swapaxes(cache_c_idx, 2, 3),
        'b_re': state_b_re, 'b_im': state_b_im, 'd': state_d,
    }

    xp = x_prompt.reshape(B * L, D_MODEL)
    xs = x_sample.reshape(Bs * Ls, D_MODEL)
    xpb, xsb = xp.astype(BF16), xs.astype(BF16)
    news_p, news_s = [], []
    for l in range(DEPTH):
        xp, xpb, n_p = _prompt_layer(xp, xpb, l, W, B, L)
        xs, xsb, n_s = _sample_layer(xs, xsb, l, W, C, Bs, Ls)
        news_p.append(n_p)
        news_s.append(n_s)
    stack = lambda news: [jnp.stack(z, axis=0) for z in zip(*news)]
    return (xp.reshape(B, L, D_MODEL), xs.reshape(Bs, Ls, D_MODEL), *stack(news_p), *stack(news_s))
```

```python
import functools
import math

import jax
import jax.numpy as jnp
from jax import lax
from jax.experimental import pallas as pl
from jax.experimental.pallas import tpu as pltpu

F32 = jnp.float32
BF16 = jnp.bfloat16

D_MODEL = 2048
DEPTH = 4
PAST_LEN = 16384
PAGE_SIZE = 128
W_GROUP = 512
H_A = 4
DH_A = 64
H_C = 4
DH_C = 128
H_IDX = 8
D_IDX = 64
TOPK_MAX = 256
G_B = 32
P_B = 64
S5_CH = 16
H_D = 4
DK_D = 64
DV_D = 128
GATE_RANK = 16
GATE_TAU = 16.0
GLA_CHUNK = 16
D_FF = 5632
REL_BUCKETS = 32
REL_MAX_EXACT = 16
REL_MAX_DIST = 128
DEEPNORM_ALPHA = (2 * DEPTH) ** 0.25
LN_EPS = 1e-5

N_MAIN = 4096
N_TAIL = 1664
TAIL_MISC = 1536
NSTATE = G_B * P_B

NEG_BIAS = -1e30
INT_MIN = -2 ** 31
KEY_NEG_INF = -2139095041

VMEM_LIMIT_BYTES = 56 * 1024 * 1024
LANES = 128
LOG2E = 1.4426950408889634
FLASH_ROW_BLOCK = 256
S5_GROUP = 8
S5_BLOCK = 256
PROJ_ROWS = 1024
ROW_TILE = 512
SWIGLU_ROWS = 1024
ATT_TILE = 512
IDX_QUERIES = 128
GLA_BLOCK = 128
SAMPLE_PAGES = 8


def _cparams(sem):
    return pltpu.CompilerParams(dimension_semantics=sem, vmem_limit_bytes=VMEM_LIMIT_BYTES)


def _dot(a, b):
    return jnp.dot(a, b, preferred_element_type=F32)


def _dot_nt(a, b):
    return lax.dot_general(a, b, (((1,), (1,)), ((), ())), preferred_element_type=F32)


KV_SLABS = (1, 2, 5, 6)


def _proj_main_kernel(a_ref, w_ref, o_ref, *kv_refs, tm):
    j = pl.program_id(1)
    res = _dot_nt(a_ref[...].astype(BF16), w_ref[...].astype(BF16))
    o_ref[...] = res
    for slab, kv_ref in zip(KV_SLABS, kv_refs):
        @pl.when(j == slab)
        def _(kv_ref=kv_ref):
            for h in range(4):
                kv_ref[pl.ds(h, tm, stride=4), :] = res[:, h * LANES:(h + 1) * LANES]


def _proj_main(x, w_in_t, l, tm):
    M, K = x.shape
    tn = W_GROUP
    kv_shape = jax.ShapeDtypeStruct((M * 4, LANES), F32)
    kv_spec = pl.BlockSpec((tm * 4, LANES), lambda i, j: (i, 0))
    return pl.pallas_call(
        functools.partial(_proj_main_kernel, tm=tm),
        out_shape=(jax.ShapeDtypeStruct((N_MAIN // tn, M, tn), F32),) + (kv_shape,) * 4,
        grid=(M // tm, N_MAIN // tn),
        in_specs=[pl.BlockSpec((tm, K), lambda i, j: (i, 0)),
                  pl.BlockSpec((None, tn, K), lambda i, j: (l, j, 0))],
        out_specs=(pl.BlockSpec((None, tm, tn), lambda i, j: (j, i, 0)),) + (kv_spec,) * 4,
        compiler_params=_cparams(("parallel", "arbitrary")),
    )(x, w_in_t)


def _proj_tail_kernel(a_ref, w_ref, o_ref):
    o_ref[...] = _dot_nt(a_ref[...].astype(BF16), w_ref[...])


def _proj_tail(x, w_tail_t, l, tm):
    M, K = x.shape
    return pl.pallas_call(
        _proj_tail_kernel,
        out_shape=jax.ShapeDtypeStruct((M, N_TAIL), F32),
        grid=(M // tm,),
        in_specs=[pl.BlockSpec((tm, K), lambda i: (i, 0)),
                  pl.BlockSpec((None, N_TAIL, K), lambda i: (l, 0, 0))],
        out_specs=pl.BlockSpec((tm, N_TAIL), lambda i: (i, 0)),
        compiler_params=_cparams(("parallel",)),
    )(x, w_tail_t)


def _deepnorm_ln(x_ref, acc, g_ref, b_ref, o32_ref, o16_ref):
    z = DEEPNORM_ALPHA * x_ref[...] + acc
    mu = jnp.mean(z, axis=1, keepdims=True)
    zc = z - mu
    var = jnp.mean(zc * zc, axis=1, keepdims=True)
    y = zc * lax.rsqrt(var + LN_EPS) * g_ref[...] + b_ref[...]
    o32_ref[...] = y
    o16_ref[...] = y.astype(BF16)


def _mix_ln_kernel(*refs, n_a, tk):
    a_refs = refs[:n_a]
    w_ref, x_ref, g_ref, b_ref, o32_ref, o16_ref = refs[n_a:]
    acc = _dot(a_refs[0][...].astype(BF16), w_ref[0:tk, :])
    for k in range(1, n_a):
        acc = acc + _dot(a_refs[k][...].astype(BF16), w_ref[k * tk:(k + 1) * tk, :])
    _deepnorm_ln(x_ref, acc, g_ref, b_ref, o32_ref, o16_ref)


def _mix_ln(a_list, w16, l, x, g, b, tm):
    M, N = x.shape
    tk = a_list[0].shape[1]
    K = tk * len(a_list)
    row = pl.BlockSpec((None, 1, N), lambda i: (l, 0, 0))
    tile = pl.BlockSpec((tm, N), lambda i: (i, 0))
    return pl.pallas_call(
        functools.partial(_mix_ln_kernel, n_a=len(a_list), tk=tk),
        out_shape=(jax.ShapeDtypeStruct((M, N), F32), jax.ShapeDtypeStruct((M, N), BF16)),
        grid=(M // tm,),
        in_specs=[pl.BlockSpec((tm, tk), lambda i: (i, 0)) for _ in a_list]
                 + [pl.BlockSpec((None, K, N), lambda i: (l, 0, 0)), tile, row, row],
        out_specs=(tile, tile),
        compiler_params=_cparams(("parallel",)),
    )(*a_list, w16, x, g, b)


def _mm_ln_kernel(a_ref, w_ref, x_ref, g_ref, b_ref, o32_ref, o16_ref, acc_ref, *, n_k):
    k = pl.program_id(1)

    @pl.when(k == 0)
    def _():
        acc_ref[...] = _dot(a_ref[...], w_ref[...])

    @pl.when(k > 0)
    def _():
        acc_ref[...] += _dot(a_ref[...], w_ref[...])

    @pl.when(k == n_k - 1)
    def _():
        _deepnorm_ln(x_ref, acc_ref[...], g_ref, b_ref, o32_ref, o16_ref)


def _mm_ln(a16, w16, l, x, g, b, tm, tk):
    M, N = x.shape
    n_k = a16.shape[1] // tk
    row = pl.BlockSpec((None, 1, N), lambda i, k: (l, 0, 0))
    tile = pl.BlockSpec((tm, N), lambda i, k: (i, 0))
    return pl.pallas_call(
        functools.partial(_mm_ln_kernel, n_k=n_k),
        out_shape=(jax.ShapeDtypeStruct((M, N), F32), jax.ShapeDtypeStruct((M, N), BF16)),
        grid=(M // tm, n_k),
        in_specs=[pl.BlockSpec((tm, tk), lambda i, k: (i, k)),
                  pl.BlockSpec((None, tk, N), lambda i, k: (l, k, 0)), tile, row, row],
        out_specs=(tile, tile),
        scratch_shapes=[pltpu.VMEM((tm, N), F32)],
        compiler_params=_cparams(("parallel", "arbitrary")),
    )(a16, w16, x, g, b)


def _swiglu_kernel(x_ref, wg_ref, wu_ref, o_ref):
    a = x_ref[...].astype(BF16)
    gate = _dot(a, wg_ref[...].astype(BF16))
    up = _dot(a, wu_ref[...].astype(BF16))
    o_ref[...] = (gate * jax.nn.sigmoid(gate) * up).astype(o_ref.dtype)


def _swiglu(x16, wg, wu, l, tm, tn):
    M, K = x16.shape
    return pl.pallas_call(
        _swiglu_kernel,
        out_shape=jax.ShapeDtypeStruct((M, D_FF), BF16),
        grid=(D_FF // tn, M // tm),
        in_specs=[pl.BlockSpec((tm, K), lambda j, i: (i, 0)),
                  pl.BlockSpec((None, K, tn), lambda j, i: (l, 0, j)),
                  pl.BlockSpec((None, K, tn), lambda j, i: (l, 0, j))],
        out_specs=pl.BlockSpec((tm, tn), lambda j, i: (i, j)),
        compiler_params=_cparams(("parallel", "arbitrary")),
    )(x16, wg, wu)


def _flash_kernel(*refs, groups, dk, scale, has_mask, diff, T):
    qi_ref, kj_ref = refs[:2]
    if has_mask:
        q_ref, k_ref, v_ref, bias_ref, mask_ref, lam_ref, gain_ref, o_ref, m_sc, l_sc, acc_sc = refs[2:]
    else:
        q_ref, k_ref, v_ref, bias_ref, lam_ref, gain_ref, o_ref, m_sc, l_sc, acc_sc = refs[2:]
    i = qi_ref[pl.program_id(2)]
    j = kj_ref[pl.program_id(2)]
    RB = FLASH_ROW_BLOCK

    @pl.when(j == 0)
    def _():
        m_sc[...] = jnp.full(m_sc.shape, -jnp.inf, F32)
        l_sc[...] = jnp.zeros_like(l_sc)
        acc_sc[...] = jnp.zeros_like(acc_sc)

    def step(diagonal):
        q = (q_ref[...] * (scale * LOG2E)).astype(BF16)
        k = k_ref[...].astype(BF16)
        v_ext = jnp.concatenate([v_ref[...].astype(BF16), jnp.ones((T, LANES), BF16)], axis=1)
        for r in range(T // RB):
            rows = slice(r * RB, (r + 1) * RB)
            nc = (r + 1) * RB if diagonal else T
            bias = bias_ref[rows, 0:nc]
            if has_mask:
                bias = bias + mask_ref[rows, 0:nc].astype(F32)
            for g in range(groups):
                s = _dot_nt(q[rows, g * dk:(g + 1) * dk], k[0:nc, g * dk:(g + 1) * dk]) + bias
                m_prev = m_sc[g, rows, :]
                m_new = jnp.maximum(m_prev, jnp.max(s, axis=1, keepdims=True))
                p = jnp.exp2(s - jnp.concatenate([m_new] * (nc // LANES), axis=1))
                a = jnp.exp2(m_prev - m_new)
                pv = _dot(p.astype(BF16), v_ext[0:nc])
                acc_sc[g, rows, :] = a * acc_sc[g, rows, :] + pv[:, :LANES]
                l_sc[g, rows, :] = a * l_sc[g, rows, :] + pv[:, LANES:]
                m_sc[g, rows, :] = m_new

    @pl.when(j < i)
    def _():
        step(False)

    @pl.when(j == i)
    def _():
        step(True)
        if diff:
            o = acc_sc[0] / l_sc[0] - lam_ref[...] * (acc_sc[1] / l_sc[1])
            ms = jnp.mean(o * o, axis=1, keepdims=True)
            o = o * lax.rsqrt(ms + LN_EPS) * gain_ref[...]
        else:
            o = acc_sc[0] / l_sc[0]
        o_ref[...] = o


def _flash_prompt(p8, slabs, bias_tab, mask, lam_row, gain_row, *, B, L, T, groups, dk, scale, diff):
    nq = L // T
    H = 4
    has_mask = mask is not None
    sq, sk, sv = slabs
    pairs = [(i, j) for i in range(nq) for j in range(i + 1)]
    qi = jnp.asarray([p[0] for p in pairs], jnp.int32)
    kj = jnp.asarray([p[1] for p in pairs], jnp.int32)
    qspec = pl.BlockSpec((None, T, LANES), lambda b, h, t, qi, kj: (sq, b * nq + qi[t], h))
    kspec = pl.BlockSpec((None, T, LANES), lambda b, h, t, qi, kj: (sk, b * nq + kj[t], h))
    vspec = pl.BlockSpec((None, T, LANES), lambda b, h, t, qi, kj: (sv, b * nq + kj[t], h))
    bspec = pl.BlockSpec((None, None, T, T),
                         lambda b, h, t, qi, kj: (h, jnp.minimum(qi[t] - kj[t], 2), 0, 0))
    rowspec = pl.BlockSpec((1, LANES), lambda b, h, t, qi, kj: (0, 0))
    in_specs = [qspec, kspec, vspec, bspec]
    args = [p8, p8, p8, bias_tab]
    if has_mask:
        in_specs.append(pl.BlockSpec((T, T), lambda b, h, t, qi, kj: (b * nq + qi[t], kj[t])))
        args.append(mask)
    in_specs += [rowspec, rowspec]
    args += [lam_row, gain_row]
    grid_spec = pltpu.PrefetchScalarGridSpec(
        num_scalar_prefetch=2,
        grid=(B, H, len(pairs)),
        in_specs=in_specs,
        out_specs=pl.BlockSpec((T, LANES), lambda b, h, t, qi, kj: (b * nq + qi[t], h)),
        scratch_shapes=[pltpu.VMEM((groups, T, LANES), F32), pltpu.VMEM((groups, T, LANES), F32),
                        pltpu.VMEM((groups, T, LANES), F32)],
    )
    return pl.pallas_call(
        functools.partial(_flash_kernel, groups=groups, dk=dk, scale=scale, has_mask=has_mask, diff=diff, T=T),
        out_shape=jax.ShapeDtypeStruct((B * L, H * LANES), F32),
        grid_spec=grid_spec,
        compiler_params=_cparams(("parallel", "parallel", "arbitrary")),
    )(qi, kj, *args)


def _sortable(x):
    i = lax.bitcast_convert_type(x, jnp.int32)
    return jnp.where(i < 0, i ^ jnp.int32(0x7FFFFFFF), i)


def _topk_bias(key_ref, o_ref, topk, n_chunks, cw):
    R = key_ref.shape[0]
    kf = float(topk)
    nl = cw // LANES

    def chunk_at(c):
        return pl.ds(pl.multiple_of(c * cw, LANES), cw)

    def count(pred):
        def body(c, acc):
            hit = jnp.where(pred(key_ref[:, chunk_at(c)]), 1.0, 0.0)
            parts = [hit[:, t * LANES:(t + 1) * LANES] for t in range(nl)]
            while len(parts) > 1:
                parts = [a + b for a, b in zip(parts[::2], parts[1::2])] + parts[len(parts) & ~1:]
            return acc + parts[0]
        acc = lax.fori_loop(0, n_chunks, body, jnp.zeros((R, LANES), F32))
        return jnp.sum(acc, axis=1, keepdims=True)

    base = jnp.where(count(lambda k: k >= 0) >= kf, 0, INT_MIN).astype(jnp.int32)

    def bit_step(it, base):
        cand = base | jnp.left_shift(jnp.int32(1), 30 - it)
        return jnp.where(count(lambda k: k >= cand) >= kf, cand, base)

    base = lax.fori_loop(0, 31, bit_step, base)
    has_excess = jnp.max(count(lambda k: k >= base)) > kf

    @pl.when(jnp.logical_not(has_excess))
    def _():
        def write(c, carry):
            k = key_ref[:, chunk_at(c)]
            sel = (k >= base) & (k > KEY_NEG_INF)
            o_ref[:, chunk_at(c)] = jnp.where(sel, 0.0, NEG_BIAS).astype(o_ref.dtype)
            return carry
        lax.fori_loop(0, n_chunks, write, 0)

    @pl.when(has_excess)
    def _():
        need = kf - count(lambda k: k > base)
        before = (lax.broadcasted_iota(jnp.int32, (LANES, LANES), 0)
                  < lax.broadcasted_iota(jnp.int32, (LANES, LANES), 1))
        before = jnp.where(before, 1.0, 0.0)

        def lane_tile(c, seen):
            sl = pl.ds(pl.multiple_of(c * LANES, LANES), LANES)
            k = key_ref[:, sl]
            eq = k == base
            eqf = jnp.where(eq, 1.0, 0.0)
            rank = _dot(eqf, before) + seen
            sel = ((k > base) | (eq & (rank < need))) & (k > KEY_NEG_INF)
            o_ref[:, sl] = jnp.where(sel, 0.0, NEG_BIAS).astype(o_ref.dtype)
            return seen + jnp.sum(eqf, axis=1, keepdims=True)

        lax.fori_loop(0, n_chunks * nl, lane_tile, jnp.zeros((R, 1), F32))


def _topk_bias_t(key_ref, o_ref, topk, n_chunks, kc):
    Q = key_ref.shape[1]
    kf = float(topk)
    SUB = 8

    def rows_at(c):
        return pl.ds(pl.multiple_of(c * kc, kc), kc)

    def count(pred):
        def body(c, acc):
            hit = jnp.where(pred(key_ref[rows_at(c), :]), 1.0, 0.0).reshape(kc // SUB, SUB, Q)
            n = kc // SUB
            while n > 1:
                n //= 2
                hit = hit[:n] + hit[n:2 * n]
            return acc + hit[0]
        acc = lax.fori_loop(0, n_chunks, body, jnp.zeros((SUB, Q), F32))
        return jnp.sum(acc, axis=0, keepdims=True)

    base = jnp.where(count(lambda k: k >= 0) >= kf, 0, INT_MIN).astype(jnp.int32)

    def bit_step(it, base):
        cand = base | jnp.left_shift(jnp.int32(1), 30 - it)
        return jnp.where(count(lambda k: k >= cand) >= kf, cand, base)

    base = lax.fori_loop(0, 31, bit_step, base)
    has_excess = jnp.max(count(lambda k: k >= base)) > kf

    def store(c, sel):
        bias_t = jnp.where(sel, 0.0, NEG_BIAS).astype(o_ref.dtype)
        o_ref[:, rows_at(c)] = jnp.transpose(bias_t)

    @pl.when(jnp.logical_not(has_excess))
    def _():
        def write(c, carry):
            k = key_ref[rows_at(c), :]
            store(c, (k >= base) & (k > KEY_NEG_INF))
            return carry
        lax.fori_loop(0, n_chunks, write, 0)

    @pl.when(has_excess)
    def _():
        need = kf - count(lambda k: k > base)
        earlier = (lax.broadcasted_iota(jnp.int32, (kc, kc), 1)
                   < lax.broadcasted_iota(jnp.int32, (kc, kc), 0))
        earlier = jnp.where(earlier, 1.0, 0.0)

        def write(c, seen):
            k = key_ref[rows_at(c), :]
            eq = k == base
            eqf = jnp.where(eq, 1.0, 0.0)
            rank = _dot(earlier, eqf) + seen
            store(c, ((k > base) | (eq & (rank < need))) & (k > KEY_NEG_INF))
            return seen + jnp.sum(eqf, axis=0, keepdims=True)

        lax.fori_loop(0, n_chunks, write, jnp.zeros((1, Q), F32))


def _idx_prompt_kernel(iq_ref, ik_ref, iwt_ref, o_ref, key_sc, *, tq, kc, topk):
    i = pl.program_id(1)
    iq = iq_ref[...]
    lhs = jnp.concatenate([iq[:, h * D_IDX:(h + 1) * D_IDX] for h in range(H_IDX)], axis=0).astype(BF16)
    iwt = iwt_ref[...]
    qpos = i * tq + lax.broadcasted_iota(jnp.int32, (1, tq), 1)
    n_need = (i * tq + tq + kc - 1) // kc
    o_ref[...] = jnp.full(o_ref.shape, NEG_BIAS, o_ref.dtype)

    def score_chunk(c, carry):
        off = pl.multiple_of(c * kc, kc)
        ik = ik_ref[pl.ds(off, kc), 0:D_IDX].astype(BF16)
        s = _dot_nt(ik, lhs)
        score = jnp.zeros((kc, tq), F32)
        for h in range(H_IDX):
            score = score + jnp.maximum(s[:, h * tq:(h + 1) * tq], 0.0) * iwt[h:h + 1, :]
        kpos = off + lax.broadcasted_iota(jnp.int32, (kc, 1), 0)
        score = jnp.where(kpos <= qpos, score, -jnp.inf)
        key_sc[pl.ds(off, kc), :] = _sortable(score)
        return carry

    lax.fori_loop(0, n_need, score_chunk, 0)
    _topk_bias_t(key_sc, o_ref, topk, n_need, kc)


def _idx_prompt(p8, tail, *, B, L, tq, topk):
    nq = L // tq
    nmisc = TAIL_MISC // LANES
    return pl.pallas_call(
        functools.partial(_idx_prompt_kernel, tq=tq, kc=min(L, 512), topk=topk),
        out_shape=jax.ShapeDtypeStruct((B * L, L), BF16),
        grid=(B, nq),
        in_specs=[pl.BlockSpec((None, tq, W_GROUP), lambda b, i: (7, b * nq + i, 0)),
                  pl.BlockSpec((L, LANES), lambda b, i: (b, nmisc)),
                  pl.BlockSpec((H_IDX, tq), lambda b, i: (0, b * nq + i))],
        out_specs=pl.BlockSpec((tq, L), lambda b, i: (b * nq + i, 0)),
        scratch_shapes=[pltpu.VMEM((L, tq), jnp.int32)],
        compiler_params=_cparams(("parallel", "arbitrary")),
    )(p8, tail, jnp.transpose(tail[:, TAIL_MISC + D_IDX:TAIL_MISC + D_IDX + H_IDX]))


def _gelu_tanh(x):
    return 0.5 * x * (1.0 + jnp.tanh(math.sqrt(2.0 / math.pi) * (x + 0.044715 * (x * x * x))))


def _s5_kernel(u_ref, h0r_ref, h0i_ref, bre_ref, bim_ref, apr_ref, api_ref, pwr_ref, pwi_ref,
               cre_ref, cim_ref, d_ref, wglu_ref, y_ref, hr_ref, hi_ref, cr_sc, ci_sc, hsr_sc, hsi_sc, *, T):
    t = pl.program_id(1)

    @pl.when(t == 0)
    def _():
        cr_sc[...] = h0r_ref[...]
        ci_sc[...] = h0i_ref[...]

    u = u_ref[...]
    ub = u.astype(BF16)
    G = T // S5_GROUP
    hr = _dot(ub, bre_ref[...]).reshape(G, S5_GROUP, NSTATE)
    hi = _dot(ub, bim_ref[...]).reshape(G, S5_GROUP, NSTATE)
    for j in range(S5_GROUP.bit_length() - 1):
        sr = pltpu.roll(hr, 1 << j, 1)
        si = pltpu.roll(hi, 1 << j, 1)
        ar = apr_ref[j]
        ai = api_ref[j]
        hr, hi = hr + (ar * sr - ai * si), hi + (ar * si + ai * sr)
    cr = cr_sc[...]
    ci = ci_sc[...]
    pr = pwr_ref[...]
    pi_ = pwi_ref[...]
    for g in range(G):
        rows = slice(g * S5_GROUP, (g + 1) * S5_GROUP)
        gr = hr[g] + (pr * cr - pi_ * ci)
        gi = hi[g] + (pr * ci + pi_ * cr)
        hsr_sc[rows, :] = gr
        hsi_sc[rows, :] = gi
        cr = gr[S5_GROUP - 1:S5_GROUP, :]
        ci = gi[S5_GROUP - 1:S5_GROUP, :]
    cr_sc[...] = cr
    ci_sc[...] = ci
    hr_ref[...] = cr
    hi_ref[...] = ci
    y = (_dot(hsr_sc[...].astype(BF16), cre_ref[...]) - _dot(hsi_sc[...].astype(BF16), cim_ref[...])
         + d_ref[...] * u)
    y = _gelu_tanh(y)
    gu = _dot(y.astype(BF16), wglu_ref[...])
    y_ref[...] = gu[:, :W_GROUP] * jax.nn.sigmoid(gu[:, W_GROUP:])


def _s5(p8, h0r, h0i, tabs, l, *, B, L, T):
    nt = L // T
    bre, bim, apr, api, pwr, pwi, cre, cim, dsk, wglu = tabs
    lay = lambda *shape: pl.BlockSpec((None,) + shape, lambda b, t: (l,) + (0,) * len(shape))
    st = pl.BlockSpec((None, 1, NSTATE), lambda b, t: (b, 0, 0))
    return pl.pallas_call(
        functools.partial(_s5_kernel, T=T),
        out_shape=(jax.ShapeDtypeStruct((B * L, W_GROUP), F32),
                   jax.ShapeDtypeStruct((B, 1, NSTATE), F32), jax.ShapeDtypeStruct((B, 1, NSTATE), F32)),
        grid=(B, nt),
        in_specs=[pl.BlockSpec((None, T, W_GROUP), lambda b, t: (3, b * nt + t, 0)), st, st,
                  lay(W_GROUP, NSTATE), lay(W_GROUP, NSTATE), lay(3, S5_GROUP, NSTATE), lay(3, S5_GROUP, NSTATE),
                  lay(S5_GROUP, NSTATE), lay(S5_GROUP, NSTATE), lay(NSTATE, W_GROUP), lay(NSTATE, W_GROUP),
                  lay(1, W_GROUP), lay(W_GROUP, 2 * W_GROUP)],
        out_specs=(pl.BlockSpec((T, W_GROUP), lambda b, t: (b * nt + t, 0)), st, st),
        scratch_shapes=[pltpu.VMEM((1, NSTATE), F32), pltpu.VMEM((1, NSTATE), F32),
                        pltpu.VMEM((T, NSTATE), F32), pltpu.VMEM((T, NSTATE), F32)],
        compiler_params=_cparams(("parallel", "arbitrary")),
    )(p8, h0r, h0i, bre, bim, apr, api, pwr, pwi, cre, cim, dsk, wglu)


def _log_sigmoid(x):
    return jnp.minimum(x, 0.0) - jnp.log1p(jnp.exp(-jnp.abs(x)))


def _gla_kernel(q_ref, k_ref, v_ref, r_ref, misc_ref, wg_ref, bg_ref, ng_ref, s0_ref,
                y_ref, sfin_ref, s_sc, *, TB, CH, nt, valid_len):
    t = pl.program_id(1)
    HK = H_D * DK_D
    HV = H_D * DV_D

    @pl.when(t == 0)
    def _():
        s_sc[...] = jnp.zeros_like(s_sc)
        for h in range(H_D):
            s_sc[h * DK_D:(h + 1) * DK_D, h * DV_D:(h + 1) * DV_D] = s0_ref[h]

    row = lax.broadcasted_iota(jnp.int32, (TB, 1), 0)
    glow = misc_ref[:, D_IDX + H_IDX:D_IDX + H_IDX + GATE_RANK]
    logit = _dot(glow.astype(BF16), wg_ref[...].astype(BF16)) + bg_ref[...]
    la = _log_sigmoid(logit) * (1.0 / GATE_TAU)
    k = k_ref[...]
    if valid_len < TB:
        la = jnp.where(row < valid_len, la, 0.0)
        k = jnp.where(row < valid_len, k, 0.0)
    ri = lax.broadcasted_iota(jnp.int32, (TB, TB), 0)
    ci = lax.broadcasted_iota(jnp.int32, (TB, TB), 1)
    same = (ri // CH) == (ci // CH)
    lower = same & (ci <= ri)
    tri = jnp.where(lower, 1.0, 0.0).astype(BF16)
    blk = jnp.where(same, 1.0, 0.0).astype(BF16)
    la_hi = la.astype(BF16)
    la_lo = (la - la_hi.astype(F32)).astype(BF16)
    bc = _dot(tri, la_hi) + _dot(tri, la_lo)
    bl = _dot(blk, la_hi) + _dot(blk, la_lo)
    q = q_ref[...] * (DK_D ** -0.5)
    v = v_ref[...]
    vb = v.astype(BF16)
    qt = (q * jnp.exp(bc)).astype(BF16)
    kc = (k * jnp.exp(-bc)).astype(BF16)
    kh_t = jnp.transpose(k * jnp.exp(bl - bc))
    dec_t = jnp.transpose(jnp.exp(bl))

    o_parts = []
    for h in range(H_D):
        att = _dot_nt(qt[:, h * DK_D:(h + 1) * DK_D], kc[:, h * DK_D:(h + 1) * DK_D])
        att = jnp.where(lower, att, 0.0)
        o_parts.append(_dot(att.astype(BF16), vb[:, h * DV_D:(h + 1) * DV_D]))
    o_intra = jnp.concatenate(o_parts, axis=1)

    head_blk = (lax.broadcasted_iota(jnp.int32, (HK, HV), 0) // DK_D
                == lax.broadcasted_iota(jnp.int32, (HK, HV), 1) // DV_D)
    col = lax.broadcasted_iota(jnp.int32, (1, TB), 1)
    S = s_sc[...]
    o_rows = []
    for c in range(TB // CH):
        o_rows.append(_dot(qt[c * CH:(c + 1) * CH, :], S.astype(BF16)))
        in_chunk = (col >= c * CH) & (col < (c + 1) * CH)
        kv = _dot(jnp.where(in_chunk, kh_t, 0.0).astype(BF16), vb)
        S = dec_t[:, c * CH:c * CH + 1] * S + jnp.where(head_blk, kv, 0.0)
    s_sc[...] = S
    o = o_intra + jnp.concatenate(o_rows, axis=0)

    outs = []
    for h in range(H_D):
        oh = o[:, h * DV_D:(h + 1) * DV_D]
        ms = jnp.mean(oh * oh, axis=1, keepdims=True)
        outs.append(oh * lax.rsqrt(ms + LN_EPS) * ng_ref[...])
    r = r_ref[...]
    y_ref[...] = jnp.concatenate(outs, axis=1) * (r * jax.nn.sigmoid(r))

    @pl.when(t == nt - 1)
    def _():
        for h in range(H_D):
            sfin_ref[h] = S[h * DK_D:(h + 1) * DK_D, h * DV_D:(h + 1) * DV_D]


def _gla(tail, s0, w_gate, b_gate, norm_g, l, *, B, L, TB, valid_len):
    nt = L // TB
    lay = lambda *shape: pl.BlockSpec((None,) + shape, lambda b, t: (l,) + (0,) * len(shape))
    sspec = pl.BlockSpec((None, H_D, DK_D, DV_D), lambda b, t: (b, 0, 0, 0))
    return pl.pallas_call(
        functools.partial(_gla_kernel, TB=TB, CH=GLA_CHUNK, nt=nt, valid_len=valid_len),
        out_shape=(jax.ShapeDtypeStruct((B * L, W_GROUP), F32),
                   jax.ShapeDtypeStruct((B, H_D, DK_D, DV_D), F32)),
        grid=(B, nt),
        in_specs=[pl.BlockSpec((TB, 256), lambda b, t: (b * nt + t, 0)),
                  pl.BlockSpec((TB, 256), lambda b, t: (b * nt + t, 1)),
                  pl.BlockSpec((TB, 512), lambda b, t: (b * nt + t, 1)),
                  pl.BlockSpec((TB, 512), lambda b, t: (b * nt + t, 2)),
                  pl.BlockSpec((TB, LANES), lambda b, t: (b * nt + t, TAIL_MISC // LANES)),
                  lay(GATE_RANK, H_D * DK_D), lay(1, H_D * DK_D), lay(1, DV_D), sspec],
        out_specs=(pl.BlockSpec((TB, W_GROUP), lambda b, t: (b * nt + t, 0)), sspec),
        scratch_shapes=[pltpu.VMEM((H_D * DK_D, H_D * DV_D), F32)],
        compiler_params=_cparams(("parallel", "arbitrary")),
    )(tail, tail, tail, tail, tail, w_gate, b_gate, norm_g, s0)


def _sidx_kernel(pt_ref, iq_ref, iw_ref, *rest, PPS, NS, n_new, topk):
    page_refs = rest[:PPS]
    new_ref, o_ref, key_sc = rest[PPS:]
    s_id = pl.program_id(1)
    iq = iq_ref[...].astype(BF16)
    iw = iw_ref[...]

    def scores(keys_t):
        hs = jnp.maximum(_dot(iq, keys_t.astype(BF16)), 0.0) * iw
        sc = jnp.zeros((n_new, LANES), F32)
        for h in range(H_IDX):
            sc = sc + hs[h * n_new:(h + 1) * n_new]
        return sc

    for p in range(PPS):
        off = pl.multiple_of((s_id * PPS + p) * LANES, LANES)
        key_sc[:, pl.ds(off, LANES)] = _sortable(scores(page_refs[p][...]))

    @pl.when(s_id == NS - 1)
    def _():
        sc = scores(new_ref[...])
        qi = lax.broadcasted_iota(jnp.int32, (n_new, LANES), 0)
        kj = lax.broadcasted_iota(jnp.int32, (n_new, LANES), 1)
        sc = jnp.where(kj <= qi, sc, -jnp.inf)
        key_sc[:, NS * PPS * LANES:(NS * PPS + 1) * LANES] = _sortable(sc)
        n_tiles = NS * PPS + 1
        n_chunks = 3 if n_tiles % 3 == 0 else 1
        _topk_bias(key_sc, o_ref, topk, n_chunks, n_tiles // n_chunks * LANES)


def _sidx(page_table, iq_rows, iw_tile, cache_idx, idx_new, l, *, PPS, topk):
    Bs, n_pages = page_table.shape
    NS = n_pages // PPS
    n_new = iq_rows.shape[1] // H_IDX
    W = (n_pages + 1) * LANES

    def page_spec(p):
        return pl.BlockSpec((None, None, D_IDX, PAGE_SIZE),
                            lambda b, s, pt: (l, pt[b, s * PPS + p], 0, 0))

    grid_spec = pltpu.PrefetchScalarGridSpec(
        num_scalar_prefetch=1,
        grid=(Bs, NS),
        in_specs=[pl.BlockSpec((None, H_IDX * n_new, D_IDX), lambda b, s, pt: (b, 0, 0)),
                  pl.BlockSpec((None, H_IDX * n_new, LANES), lambda b, s, pt: (b, 0, 0))]
                 + [page_spec(p) for p in range(PPS)]
                 + [pl.BlockSpec((None, D_IDX, PAGE_SIZE), lambda b, s, pt: (b, 0, 0))],
        out_specs=pl.BlockSpec((None, n_new, W), lambda b, s, pt: (b, 0, 0)),
        scratch_shapes=[pltpu.VMEM((n_new, W), jnp.int32)],
    )
    return pl.pallas_call(
        functools.partial(_sidx_kernel, PPS=PPS, NS=NS, n_new=n_new, topk=topk),
        out_shape=jax.ShapeDtypeStruct((Bs, n_new, W), F32),
        grid_spec=grid_spec,
        compiler_params=_cparams(("parallel", "arbitrary")),
    )(page_table, iq_rows, iw_tile, *([cache_idx] * PPS), idx_new)


def _sattn_kernel(pt_ref, q_ref, *rest, PPS, NS, R, n_new, has_mask, diff):
    k_refs = rest[:PPS]
    v_refs = rest[PPS:2 * PPS]
    rest = rest[2 * PPS:]
    knew_ref, vnew_ref, far_ref, near_ref = rest[:4]
    rest = rest[4:]
    if has_mask:
        mask_ref, masknew_ref, spread_ref = rest[:3]
        rest = rest[3:]
    lam_ref, gain_ref, o_ref, m_sc, l_sc, acc_sc = rest
    s_id = pl.program_id(1)

    @pl.when(s_id == 0)
    def _():
        m_sc[...] = jnp.full(m_sc.shape, -jnp.inf, F32)
        l_sc[...] = jnp.zeros_like(l_sc)
        acc_sc[...] = jnp.zeros_like(acc_sc)

    q = q_ref[...].astype(BF16)
    PR = PAGE_SIZE * 4

    def expand(m):
        return jnp.concatenate([_dot(m, spread_ref[...])] * (R // n_new), axis=0)

    def pages(kps, vps, biases):
        ss = [_dot_nt(q, kp.astype(BF16)) + b for kp, b in zip(kps, biases)]
        top = ss[0]
        for s in ss[1:]:
            top = jnp.maximum(top, s)
        m_prev = m_sc[...]
        m_new = jnp.maximum(m_prev, jnp.max(top, axis=1, keepdims=True))
        a = jnp.exp2(m_prev - m_new)
        ps = [jnp.exp2(s - m_new) for s in ss]
        tot = ps[0]
        for p in ps[1:]:
            tot = tot + p
        pv = _dot(ps[0].astype(BF16), vps[0].astype(BF16))
        for p, vp in zip(ps[1:], vps[1:]):
            pv = pv + _dot(p.astype(BF16), vp.astype(BF16))
        l_sc[...] = a * l_sc[...] + jnp.sum(tot, axis=1, keepdims=True)
        acc_sc[...] = a * acc_sc[...] + pv
        m_sc[...] = m_new

    far = far_ref[...]
    biases = []
    for p in range(PPS):
        bias = far
        if p == PPS - 1:
            bias = jnp.where(s_id == NS - 1, near_ref[:, 0:PR], far)
        if has_mask:
            bias = bias + expand(mask_ref[:, p * LANES:(p + 1) * LANES])
        biases.append(bias)
    pages([r[...] for r in k_refs], [r[...] for r in v_refs], biases)

    @pl.when(s_id == NS - 1)
    def _():
        bias = near_ref[:, PR:2 * PR]
        if has_mask:
            bias = bias + expand(masknew_ref[...])
        pages([knew_ref[...]], [vnew_ref[...]], [bias])
        o_all = acc_sc[...] / l_sc[...]
        outs = []
        for h in range(4):
            if diff:
                o1 = o_all[(2 * h) * n_new:(2 * h + 1) * n_new]
                o2 = o_all[(2 * h + 1) * n_new:(2 * h + 2) * n_new]
                o = o1 - lam_ref[...] * o2
                ms = jnp.mean(o * o, axis=1, keepdims=True)
                outs.append(o * lax.rsqrt(ms + LN_EPS) * gain_ref[...])
            else:
                outs.append(o_all[h * n_new:(h + 1) * n_new])
        o_ref[...] = jnp.concatenate(outs, axis=1)


def _sattn(page_table, q_rows, cache_k, cache_v, knew, vnew, far, near, mask, lam_row, gain_row, l,
           *, PPS, diff):
    Bs, n_pages = page_table.shape
    NS = n_pages // PPS
    R = q_rows.shape[1]
    n_new = 8
    PR = PAGE_SIZE * 4
    has_mask = mask is not None

    def page_spec(p):
        return pl.BlockSpec((None, None, PR, LANES), lambda b, s, pt: (l, pt[b, s * PPS + p], 0, 0))

    per_b = lambda *shape: pl.BlockSpec((None,) + shape, lambda b, s, pt: (b,) + (0,) * len(shape))
    const = lambda *shape: pl.BlockSpec(shape, lambda b, s, pt: (0,) * len(shape))
    in_specs = ([per_b(R, LANES)] + [page_spec(p) for p in range(PPS)] * 2
                + [per_b(PR, LANES), per_b(PR, LANES), const(R, PR), const(R, 2 * PR)])
    args = [q_rows] + [cache_k] * PPS + [cache_v] * PPS + [knew, vnew, far, near]
    if has_mask:
        spread = (jnp.arange(PR, dtype=jnp.int32)[None, :] // 4
                  == jnp.arange(PAGE_SIZE, dtype=jnp.int32)[:, None]).astype(F32)
        in_specs += [pl.BlockSpec((None, n_new, PPS * LANES), lambda b, s, pt: (b, 0, s)),
                     pl.BlockSpec((None, n_new, LANES), lambda b, s, pt: (b, 0, n_pages)),
                     const(PAGE_SIZE, PR)]
        args += [mask, mask, spread]
    in_specs += [const(1, LANES), const(1, LANES)]
    args += [lam_row, gain_row]
    grid_spec = pltpu.PrefetchScalarGridSpec(
        num_scalar_prefetch=1,
        grid=(Bs, NS),
        in_specs=in_specs,
        out_specs=pl.BlockSpec((None, n_new, W_GROUP), lambda b, s, pt: (b, 0, 0)),
        scratch_shapes=[pltpu.VMEM((R, 1), F32), pltpu.VMEM((R, 1), F32), pltpu.VMEM((R, LANES), F32)],
    )
    return pl.pallas_call(
        functools.partial(_sattn_kernel, PPS=PPS, NS=NS, R=R, n_new=n_new,
                          has_mask=has_mask, diff=diff),
        out_shape=jax.ShapeDtypeStruct((Bs, n_new, W_GROUP), F32),
        grid_spec=grid_spec,
        compiler_params=_cparams(("parallel", "arbitrary")),
    )(page_table, *args)


def _rel_bucket(dist):
    n = jnp.maximum(dist, 0)
    nf = jnp.maximum(n, REL_MAX_EXACT).astype(F32)
    large = REL_MAX_EXACT + (jnp.log(nf / REL_MAX_EXACT) / math.log(REL_MAX_DIST / REL_MAX_EXACT)
                             * (REL_BUCKETS - REL_MAX_EXACT)).astype(jnp.int32)
    large = jnp.minimum(large, REL_BUCKETS - 1)
    return jnp.where(n < REL_MAX_EXACT, n, large)


def _rel_lookup(rel, bucket):
    out = jnp.zeros((rel.shape[1],) + bucket.shape, F32)
    for b in range(REL_BUCKETS):
        out = jnp.where(bucket[None] == b, rel[b].reshape((-1,) + (1,) * bucket.ndim), out)
    return out * LOG2E


def _prompt_bias_tables(rel, T):
    assert T >= REL_MAX_DIST
    r = jnp.arange(T, dtype=jnp.int32)[:, None]
    c = jnp.arange(T, dtype=jnp.int32)[None, :]
    d0 = r - c
    t0 = jnp.where((d0 >= 0)[None], _rel_lookup(rel, _rel_bucket(d0)), NEG_BIAS)
    t1 = _rel_lookup(rel, _rel_bucket(T + r - c))
    t2 = _rel_lookup(rel, _rel_bucket(jnp.full((T, T), 2 * T, jnp.int32)))
    return jnp.stack([t0, t1, t2], axis=1)


def _sample_bias_tables(rel, n_new, reps):
    H = rel.shape[1]
    t = jnp.arange(n_new, dtype=jnp.int32)[:, None]
    c = jnp.arange(PAGE_SIZE, dtype=jnp.int32)[None, :]
    last = _rel_lookup(rel, _rel_bucket(PAGE_SIZE + t - c))
    dn = t - c
    new = jnp.where(((dn >= 0) & (c < n_new))[None], _rel_lookup(rel, _rel_bucket(dn)), NEG_BIAS)
    far = _rel_lookup(rel, _rel_bucket(jnp.full((n_new, PAGE_SIZE), 2 * PAGE_SIZE, jnp.int32)))

    def rows(x):
        own = jnp.eye(H, dtype=bool)[:, None, None, :]
        y = jnp.where(own, x[..., None], NEG_BIAS).reshape(H, 1, n_new, PAGE_SIZE * H)
        return jnp.broadcast_to(y, (H, reps, n_new, PAGE_SIZE * H)).reshape(H * reps * n_new, PAGE_SIZE * H)

    return rows(far), jnp.concatenate([rows(last), rows(new)], axis=1)


def _s5_tables(a_re, a_im, log_dt, b_re, b_im, c_re, c_im, d_skip, w_glu):
    dt = jnp.exp(log_dt.astype(F32))[..., None]
    lam_re, lam_im = a_re.astype(F32), a_im.astype(F32)
    z_re, z_im = lam_re * dt, lam_im * dt
    mag = jnp.exp(z_re)
    e_re, e_im = mag * jnp.cos(z_im), mag * jnp.sin(z_im)
    den = lam_re * lam_re + lam_im * lam_im
    f_re = ((e_re - 1.0) * lam_re + e_im * lam_im) / den
    f_im = (e_im * lam_re - (e_re - 1.0) * lam_im) / den
    br, bi = b_re.astype(F32), b_im.astype(F32)
    bb_re = f_re[..., None] * br - f_im[..., None] * bi
    bb_im = f_re[..., None] * bi + f_im[..., None] * br
    eye = jnp.eye(G_B, dtype=F32)
    bd_in = lambda bb: jnp.einsum('lgpc,gh->lgchp', bb, eye).reshape(DEPTH, W_GROUP, NSTATE).astype(BF16)
    bd_out = lambda cc: jnp.einsum('lgcp,gh->lgphc', cc.astype(F32), eye).reshape(DEPTH, NSTATE, W_GROUP).astype(BF16)

    def powers(ts):
        tt = ts[None, :, None, None]
        pm = jnp.exp(z_re[:, None] * tt)
        return ((pm * jnp.cos(z_im[:, None] * tt)).reshape(DEPTH, -1, NSTATE),
                (pm * jnp.sin(z_im[:, None] * tt)).reshape(DEPTH, -1, NSTATE))

    n_steps = S5_GROUP.bit_length() - 1
    apr, api = powers(jnp.asarray([2.0 ** j for j in range(n_steps)], F32))
    live = (jnp.arange(S5_GROUP)[None, :] >= (2 ** jnp.arange(n_steps))[:, None]).astype(F32)
    apr = apr[:, :, None, :] * live[None, :, :, None]
    api = api[:, :, None, :] * live[None, :, :, None]
    pwr, pwi = powers(jnp.arange(1, S5_GROUP + 1, dtype=F32))
    return (bd_in(bb_re), bd_in(bb_im), apr, api, pwr, pwi, bd_out(c_re), bd_out(c_im),
            d_skip.astype(F32).reshape(DEPTH, 1, W_GROUP), w_glu.astype(BF16))


def _tail_weights(w_in_t):
    pad = jnp.zeros((w_in_t.shape[0], N_TAIL - 1624, w_in_t.shape[2]), w_in_t.dtype)
    return jnp.concatenate([w_in_t[:, 4168:5704], w_in_t[:, 4096:4168], w_in_t[:, 5704:5720], pad],
                           axis=1).astype(BF16)


def _channel_mix(ys, x, l, W, tm, tm_ffn):
    x1, x1b = _mix_ln(ys, W['w_out'], l, x, W['ln1_g'], W['ln1_b'], tm)
    hid = _swiglu(x1b, W['ffn_w_gate'], W['ffn_w_up'], l, tm_ffn, 512)
    return _mm_ln(hid, W['ffn_w_down'], l, x1, W['ln2_g'], W['ln2_b'], tm, D_FF // 4)


def _prompt_layer(x, xb, l, W, B, L):
    p8, a_k, a_v, c_k, c_v = _proj_main(xb, W['w_in_t'], l, PROJ_ROWS)
    tail = _proj_tail(xb, W['w_tail_t'], l, ROW_TILE)
    y_a = _flash_prompt(p8, (0, 1, 2), W['tab_a'], None, W['lam_row'][l], W['gain_a'][l],
                        B=B, L=L, T=ATT_TILE, groups=2, dk=DH_A, scale=DH_A ** -0.5, diff=True)
    zeros_h = jnp.zeros((B, 1, NSTATE), F32)
    y_b, h_re, h_im = _s5(p8, zeros_h, zeros_h, W['s5'], l, B=B, L=L, T=S5_BLOCK)
    mask = _idx_prompt(p8, tail, B=B, L=L, tq=IDX_QUERIES, topk=min(TOPK_MAX, L // 4))
    y_c = _flash_prompt(p8, (4, 5, 6), W['tab_c'], mask, W['lam_row'][l], W['gain_a'][l],
                        B=B, L=L, T=ATT_TILE, groups=1, dk=DH_C, scale=DH_C ** -0.5, diff=False)
    y_d, s_fin = _gla(tail, jnp.zeros((B, H_D, DK_D, DV_D), F32), W['gla_w_gate'], W['gla_b_gate'],
                      W['gla_norm'], l, B=B, L=L, TB=GLA_BLOCK, valid_len=GLA_BLOCK)
    x, xb = _channel_mix([y_a, y_b, y_c, y_d], x, l, W, ROW_TILE, SWIGLU_ROWS)
    news = (a_k.reshape(B, L, H_A, 128), a_v.reshape(B, L, H_A, 128),
            c_k.reshape(B, L, H_C, 128), c_v.reshape(B, L, H_C, 128),
            tail[:, TAIL_MISC:TAIL_MISC + D_IDX].reshape(B, L, D_IDX),
            h_re.reshape(B, G_B, P_B), h_im.reshape(B, G_B, P_B), s_fin)
    return x, xb, news


def _sample_layer(x, xb, l, W, C, Bs, Ls):
    M = Bs * Ls
    PPS = SAMPLE_PAGES
    p8, a_k, a_v, c_k, c_v = _proj_main(xb, W['w_in_t'], l, M)
    tail = _proj_tail(xb, W['w_tail_t'], l, M)
    pt = C['page_table']

    def pad_new(z):
        z = z.reshape(Bs, Ls, z.shape[-1])
        return jnp.pad(z, ((0, 0), (0, PAGE_SIZE - Ls), (0, 0)))

    def pad_rows(z):
        z = z.reshape(Bs, Ls * 4, LANES)
        return jnp.pad(z, ((0, 0), (0, PAGE_SIZE * 4 - Ls * 4), (0, 0)))

    qa = p8[0].reshape(Bs, Ls, H_A, 2, DH_A) * (DH_A ** -0.5 * LOG2E)
    q_a = jnp.einsum('bqhwd,wv->bhwqvd', qa, jnp.eye(2, dtype=F32)).reshape(Bs, H_A * 2 * Ls, LANES)
    y_a = _sattn(pt, q_a, C['a_k'], C['a_v'], pad_rows(a_k), pad_rows(a_v), W['far_a'], W['near_a'],
                 None, W['lam_row'][l], W['gain_a'][l], l, PPS=PPS, diff=True)
    y_b, h_re, h_im = _s5(p8, C['b_re'][l].reshape(Bs, 1, NSTATE), C['b_im'][l].reshape(Bs, 1, NSTATE),
                          W['s5'], l, B=Bs, L=Ls, T=Ls)
    misc = tail[:, TAIL_MISC:]
    iq_rows = p8[7].reshape(Bs, Ls, H_IDX, D_IDX).transpose(0, 2, 1, 3).reshape(Bs, H_IDX * Ls, D_IDX)
    iw = misc[:, D_IDX:D_IDX + H_IDX].reshape(Bs, Ls, H_IDX).transpose(0, 2, 1).reshape(Bs, H_IDX * Ls, 1)
    iw_tile = jnp.broadcast_to(iw, (Bs, H_IDX * Ls, LANES))
    mask = _sidx(pt, iq_rows, iw_tile, C['c_idx_t'], pad_new(misc[:, :D_IDX]).transpose(0, 2, 1), l, PPS=PPS,
                 topk=min(TOPK_MAX, (pt.shape[1] * PAGE_SIZE + Ls) // 4))
    q_c = (p8[4].reshape(Bs, Ls, H_C, DH_C) * (DH_C ** -0.5 * LOG2E)).transpose(0, 2, 1, 3)
    y_c = _sattn(pt, q_c.reshape(Bs, H_C * Ls, LANES), C['c_k'], C['c_v'], pad_rows(c_k), pad_rows(c_v),
                 W['far_c'], W['near_c'], mask, W['lam_row'][l], W['gain_a'][l], l, PPS=PPS, diff=False)
    tail_pad = pad_new(tail).reshape(Bs * PAGE_SIZE, N_TAIL)
    y_d, s_fin = _gla(tail_pad, C['d'][l], W['gla_w_gate'], W['gla_b_gate'], W['gla_norm'], l,
                      B=Bs, L=PAGE_SIZE, TB=PAGE_SIZE, valid_len=Ls)
    y_d = y_d.reshape(Bs, PAGE_SIZE, W_GROUP)[:, :Ls].reshape(M, W_GROUP)
    x, xb = _channel_mix([y_a.reshape(M, W_GROUP), y_b, y_c.reshape(M, W_GROUP), y_d], x, l, W, M, M)
    news = (a_k.reshape(Bs, Ls, H_A, 128), a_v.reshape(Bs, Ls, H_A, 128),
            c_k.reshape(Bs, Ls, H_C, 128), c_v.reshape(Bs, Ls, H_C, 128),
            misc[:, :D_IDX].reshape(Bs, Ls, D_IDX),
            h_re.reshape(Bs, G_B, P_B), h_im.reshape(Bs, G_B, P_B), s_fin)
    return x, xb, news


def kernel(x_prompt, x_sample, cache_a_k, cache_a_v, cache_c_k, cache_c_v, cache_c_idx, state_b_re, state_b_im, state_d, page_table, rel_bias, w_in, w_out, lam_q1, lam_k1, lam_q2, lam_k2, a_subln, s5_a_re, s5_a_im, s5_log_dt, s5_b_re, s5_b_im, s5_c_re, s5_c_im, s5_d, s5_w_glu, gla_w_gate, gla_b_gate, gla_norm, ln1_g, ln1_b, ffn_w_gate, ffn_w_up, ffn_w_down, ln2_g, ln2_b):
    B, L, _ = x_prompt.shape
    Bs, Ls, _ = x_sample.shape
    n_pool = cache_a_k.shape[1]

    lam_init = jnp.asarray([0.8 - 0.6 * math.exp(-0.3 * l) for l in range(DEPTH)], F32)
    lam = (jnp.exp(jnp.sum(lam_q1.astype(F32) * lam_k1.astype(F32), axis=-1))
           - jnp.exp(jnp.sum(lam_q2.astype(F32) * lam_k2.astype(F32), axis=-1)) + lam_init)
    s5_args = (s5_a_re, s5_a_im, s5_log_dt, s5_b_re, s5_b_im, s5_c_re, s5_c_im, s5_d, s5_w_glu)
    rel = rel_bias.astype(F32)
    w_in_t = jnp.swapaxes(w_in, 1, 2).astype(BF16)
    far_a, near_a = _sample_bias_tables(rel[:, :H_A], Ls, 2)
    far_c, near_c = _sample_bias_tables(rel[:, H_A:], Ls, 1)
    W = {
        'w_in_t': w_in_t, 'w_tail_t': _tail_weights(w_in_t), 'w_out': w_out.astype(BF16),
        'ffn_w_gate': ffn_w_gate, 'ffn_w_up': ffn_w_up, 'ffn_w_down': ffn_w_down.astype(BF16),
        'ln1_g': ln1_g.reshape(DEPTH, 1, D_MODEL), 'ln1_b': ln1_b.reshape(DEPTH, 1, D_MODEL),
        'ln2_g': ln2_g.reshape(DEPTH, 1, D_MODEL), 'ln2_b': ln2_b.reshape(DEPTH, 1, D_MODEL),
        'lam_row': jnp.broadcast_to(lam[:, None, None], (DEPTH, 1, LANES)),
        'gain_a': (a_subln.astype(F32) * (1.0 - lam_init)[:, None]).reshape(DEPTH, 1, LANES),
        'tab_a': _prompt_bias_tables(rel[:, :H_A], ATT_TILE),
        'tab_c': _prompt_bias_tables(rel[:, H_A:], ATT_TILE),
        'far_a': far_a, 'near_a': near_a, 'far_c': far_c, 'near_c': near_c,
        's5': _s5_tables(*s5_args),
        'gla_w_gate': gla_w_gate, 'gla_b_gate': gla_b_gate.reshape(DEPTH, 1, H_D * DK_D),
        'gla_norm': gla_norm.reshape(DEPTH, 1, DV_D),
    }
    C = {
        'page_table': page_table,
        'a_k': cache_a_k.reshape(DEPTH, n_pool, PAGE_SIZE * H_A, LANES),
        'a_v': cache_a_v.reshape(DEPTH, n_pool, PAGE_SIZE * H_A, LANES),
        'c_k': cache_c_k.reshape(DEPTH, n_pool, PAGE_SIZE * H_C, LANES),
        'c_v': cache_c_v.reshape(DEPTH, n_pool, PAGE_SIZE * H_C, LANES),
        'c_idx_t': jnp.swapaxes(cache_c_idx, 2, 3),
        'b_re': state_b_re, 'b_im': state_b_im, 'd': state_d,
    }

    xp = x_prompt.reshape(B * L, D_MODEL)
    xs = x_sample.reshape(Bs * Ls, D_MODEL)
    xpb, xsb = xp.astype(BF16), xs.astype(BF16)
    news_p, news_s = [], []
    for l in range(DEPTH):
        xp, xpb, n_p = _prompt_layer(xp, xpb, l, W, B, L)
        xs, xsb, n_s = _sample_layer(xs, xsb, l, W, C, Bs, Ls)
        news_p.append(n_p)
        news_s.append(n_s)
    stack = lambda news: [jnp.stack(z, axis=0) for z in zip(*news)]
    return (xp.reshape(B, L, D_MODEL), xs.reshape(Bs, Ls, D_MODEL), *stack(news_p), *stack(news_s))
```

```python
import functools
import math

import jax
import jax.numpy as jnp
from jax import lax
from jax.experimental import pallas as pl
from jax.experimental.pallas import tpu as pltpu

F32 = jnp.float32
BF16 = jnp.bfloat16

D_MODEL = 2048
DEPTH = 4
PAST_LEN = 16384
PAGE_SIZE = 128
W_GROUP = 512
H_A = 4
DH_A = 64
H_C = 4
DH_C = 128
H_IDX = 8
D_IDX = 64
TOPK_MAX = 256
G_B = 32
P_B = 64
S5_CH = 16
H_D = 4
DK_D = 64
DV_D = 128
GATE_RANK = 16
GATE_TAU = 16.0
GLA_CHUNK = 16
D_FF = 5632
REL_BUCKETS = 32
REL_MAX_EXACT = 16
REL_MAX_DIST = 128
DEEPNORM_ALPHA = (2 * DEPTH) ** 0.25
LN_EPS = 1e-5

N_MAIN = 4096
N_TAIL = 1664
TAIL_MISC = 1536
NSTATE = G_B * P_B

NEG_BIAS = -1e30
INT_MIN = -2 ** 31
KEY_NEG_INF = -2139095041

VMEM_LIMIT_BYTES = 56 * 1024 * 1024
LANES = 128
LOG2E = 1.4426950408889634
FLASH_ROW_BLOCK = 256
S5_GROUP = 8
S5_BLOCK = 256
PROJ_ROWS = 1024
ROW_TILE = 512
SWIGLU_ROWS = 1024
ATT_TILE = 512
IDX_QUERIES = 128
GLA_BLOCK = 128
SAMPLE_PAGES = 16


def _cparams(sem):
    return pltpu.CompilerParams(dimension_semantics=sem, vmem_limit_bytes=VMEM_LIMIT_BYTES)


def _dot(a, b):
    return jnp.dot(a, b, preferred_element_type=F32)


def _dot_nt(a, b):
    return lax.dot_general(a, b, (((1,), (1,)), ((), ())), preferred_element_type=F32)


KV_SLABS = (1, 2, 5, 6)


def _proj_main_kernel(a_ref, w_ref, o_ref, *kv_refs, tm):
    j = pl.program_id(1)
    res = _dot_nt(a_ref[...].astype(BF16), w_ref[...].astype(BF16))
    o_ref[...] = res
    for slab, kv_ref in zip(KV_SLABS, kv_refs):
        @pl.when(j == slab)
        def _(kv_ref=kv_ref):
            for h in range(4):
                kv_ref[pl.ds(h, tm, stride=4), :] = res[:, h * LANES:(h + 1) * LANES]


def _proj_main(x, w_in_t, l, tm):
    M, K = x.shape
    tn = W_GROUP
    kv_shape = jax.ShapeDtypeStruct((M * 4, LANES), F32)
    kv_spec = pl.BlockSpec((tm * 4, LANES), lambda i, j: (i, 0))
    return pl.pallas_call(
        functools.partial(_proj_main_kernel, tm=tm),
        out_shape=(jax.ShapeDtypeStruct((N_MAIN // tn, M, tn), F32),) + (kv_shape,) * 4,
        grid=(M // tm, N_MAIN // tn),
        in_specs=[pl.BlockSpec((tm, K), lambda i, j: (i, 0)),
                  pl.BlockSpec((None, tn, K), lambda i, j: (l, j, 0))],
        out_specs=(pl.BlockSpec((None, tm, tn), lambda i, j: (j, i, 0)),) + (kv_spec,) * 4,
        compiler_params=_cparams(("parallel", "arbitrary")),
    )(x, w_in_t)


def _proj_tail_kernel(a_ref, w_ref, o_ref):
    o_ref[...] = _dot_nt(a_ref[...].astype(BF16), w_ref[...])


def _proj_tail(x, w_tail_t, l, tm):
    M, K = x.shape
    return pl.pallas_call(
        _proj_tail_kernel,
        out_shape=jax.ShapeDtypeStruct((M, N_TAIL), F32),
        grid=(M // tm,),
        in_specs=[pl.BlockSpec((tm, K), lambda i: (i, 0)),
                  pl.BlockSpec((None, N_TAIL, K), lambda i: (l, 0, 0))],
        out_specs=pl.BlockSpec((tm, N_TAIL), lambda i: (i, 0)),
        compiler_params=_cparams(("parallel",)),
    )(x, w_tail_t)


def _deepnorm_ln(x_ref, acc, g_ref, b_ref, o32_ref, o16_ref):
    z = DEEPNORM_ALPHA * x_ref[...] + acc
    mu = jnp.mean(z, axis=1, keepdims=True)
    zc = z - mu
    var = jnp.mean(zc * zc, axis=1, keepdims=True)
    y = zc * lax.rsqrt(var + LN_EPS) * g_ref[...] + b_ref[...]
    o32_ref[...] = y
    o16_ref[...] = y.astype(BF16)


def _mix_ln_kernel(*refs, n_a, tk):
    a_refs = refs[:n_a]
    w_ref, x_ref, g_ref, b_ref, o32_ref, o16_ref = refs[n_a:]
    acc = _dot(a_refs[0][...].astype(BF16), w_ref[0:tk, :])
    for k in range(1, n_a):
        acc = acc + _dot(a_refs[k][...].astype(BF16), w_ref[k * tk:(k + 1) * tk, :])
    _deepnorm_ln(x_ref, acc, g_ref, b_ref, o32_ref, o16_ref)


def _mix_ln(a_list, w16, l, x, g, b, tm):
    M, N = x.shape
    tk = a_list[0].shape[1]
    K = tk * len(a_list)
    row = pl.BlockSpec((None, 1, N), lambda i: (l, 0, 0))
    tile = pl.BlockSpec((tm, N), lambda i: (i, 0))
    return pl.pallas_call(
        functools.partial(_mix_ln_kernel, n_a=len(a_list), tk=tk),
        out_shape=(jax.ShapeDtypeStruct((M, N), F32), jax.ShapeDtypeStruct((M, N), BF16)),
        grid=(M // tm,),
        in_specs=[pl.BlockSpec((tm, tk), lambda i: (i, 0)) for _ in a_list]
                 + [pl.BlockSpec((None, K, N), lambda i: (l, 0, 0)), tile, row, row],
        out_specs=(tile, tile),
        compiler_params=_cparams(("parallel",)),
    )(*a_list, w16, x, g, b)


def _mm_ln_kernel(a_ref, w_ref, x_ref, g_ref, b_ref, o32_ref, o16_ref, acc_ref, *, n_k):
    k = pl.program_id(1)

    @pl.when(k == 0)
    def _():
        acc_ref[...] = _dot(a_ref[...], w_ref[...])

    @pl.when(k > 0)
    def _():
        acc_ref[...] += _dot(a_ref[...], w_ref[...])

    @pl.when(k == n_k - 1)
    def _():
        _deepnorm_ln(x_ref, acc_ref[...], g_ref, b_ref, o32_ref, o16_ref)


def _mm_ln(a16, w16, l, x, g, b, tm, tk):
    M, N = x.shape
    n_k = a16.shape[1] // tk
    row = pl.BlockSpec((None, 1, N), lambda i, k: (l, 0, 0))
    tile = pl.BlockSpec((tm, N), lambda i, k: (i, 0))
    return pl.pallas_call(
        functools.partial(_mm_ln_kernel, n_k=n_k),
        out_shape=(jax.ShapeDtypeStruct((M, N), F32), jax.ShapeDtypeStruct((M, N), BF16)),
        grid=(M // tm, n_k),
        in_specs=[pl.BlockSpec((tm, tk), lambda i, k: (i, k)),
                  pl.BlockSpec((None, tk, N), lambda i, k: (l, k, 0)), tile, row, row],
        out_specs=(tile, tile),
        scratch_shapes=[pltpu.VMEM((tm, N), F32)],
        compiler_params=_cparams(("parallel", "arbitrary")),
    )(a16, w16, x, g, b)


def _swiglu_kernel(x_ref, wg_ref, wu_ref, o_ref):
    a = x_ref[...].astype(BF16)
    gate = _dot(a, wg_ref[...].astype(BF16))
    up = _dot(a, wu_ref[...].astype(BF16))
    o_ref[...] = (gate * jax.nn.sigmoid(gate) * up).astype(o_ref.dtype)


def _swiglu(x16, wg, wu, l, tm, tn):
    M, K = x16.shape
    return pl.pallas_call(
        _swiglu_kernel,
        out_shape=jax.ShapeDtypeStruct((M, D_FF), BF16),
        grid=(D_FF // tn, M // tm),
        in_specs=[pl.BlockSpec((tm, K), lambda j, i: (i, 0)),
                  pl.BlockSpec((None, K, tn), lambda j, i: (l, 0, j)),
                  pl.BlockSpec((None, K, tn), lambda j, i: (l, 0, j))],
        out_specs=pl.BlockSpec((tm, tn), lambda j, i: (i, j)),
        compiler_params=_cparams(("parallel", "arbitrary")),
    )(x16, wg, wu)


def _flash_kernel(*refs, groups, dk, scale, has_mask, diff, T):
    qi_ref, kj_ref = refs[:2]
    if has_mask:
        q_ref, k_ref, v_ref, bias_ref, mask_ref, lam_ref, gain_ref, o_ref, m_sc, l_sc, acc_sc = refs[2:]
    else:
        q_ref, k_ref, v_ref, bias_ref, lam_ref, gain_ref, o_ref, m_sc, l_sc, acc_sc = refs[2:]
    i = qi_ref[pl.program_id(2)]
    j = kj_ref[pl.program_id(2)]
    RB = FLASH_ROW_BLOCK

    @pl.when(j == 0)
    def _():
        m_sc[...] = jnp.full(m_sc.shape, -jnp.inf, F32)
        l_sc[...] = jnp.zeros_like(l_sc)
        acc_sc[...] = jnp.zeros_like(acc_sc)

    def step(diagonal):
        q = (q_ref[...] * (scale * LOG2E)).astype(BF16)
        k = k_ref[...].astype(BF16)
        v_ext = jnp.concatenate([v_ref[...].astype(BF16), jnp.ones((T, LANES), BF16)], axis=1)
        for r in range(T // RB):
            rows = slice(r * RB, (r + 1) * RB)
            nc = (r + 1) * RB if diagonal else T
            bias = bias_ref[rows, 0:nc]
            if has_mask:
                bias = bias + mask_ref[rows, 0:nc].astype(F32)
            for g in range(groups):
                s = _dot_nt(q[rows, g * dk:(g + 1) * dk], k[0:nc, g * dk:(g + 1) * dk]) + bias
                m_prev = m_sc[g, rows, :]
                m_new = jnp.maximum(m_prev, jnp.max(s, axis=1, keepdims=True))
                p = jnp.exp2(s - jnp.concatenate([m_new] * (nc // LANES), axis=1))
                a = jnp.exp2(m_prev - m_new)
                pv = _dot(p.astype(BF16), v_ext[0:nc])
                acc_sc[g, rows, :] = a * acc_sc[g, rows, :] + pv[:, :LANES]
                l_sc[g, rows, :] = a * l_sc[g, rows, :] + pv[:, LANES:]
                m_sc[g, rows, :] = m_new

    @pl.when(j < i)
    def _():
        step(False)

    @pl.when(j == i)
    def _():
        step(True)
        if diff:
            o = acc_sc[0] / l_sc[0] - lam_ref[...] * (acc_sc[1] / l_sc[1])
            ms = jnp.mean(o * o, axis=1, keepdims=True)
            o = o * lax.rsqrt(ms + LN_EPS) * gain_ref[...]
        else:
            o = acc_sc[0] / l_sc[0]
        o_ref[...] = o


def _flash_prompt(p8, slabs, bias_tab, mask, lam_row, gain_row, *, B, L, T, groups, dk, scale, diff):
    nq = L // T
    H = 4
    has_mask = mask is not None
    sq, sk, sv = slabs
    pairs = [(i, j) for i in range(nq) for j in range(i + 1)]
    qi = jnp.asarray([p[0] for p in pairs], jnp.int32)
    kj = jnp.asarray([p[1] for p in pairs], jnp.int32)
    qspec = pl.BlockSpec((None, T, LANES), lambda b, h, t, qi, kj: (sq, b * nq + qi[t], h))
    kspec = pl.BlockSpec((None, T, LANES), lambda b, h, t, qi, kj: (sk, b * nq + kj[t], h))
    vspec = pl.BlockSpec((None, T, LANES), lambda b, h, t, qi, kj: (sv, b * nq + kj[t], h))
    bspec = pl.BlockSpec((None, None, T, T),
                         lambda b, h, t, qi, kj: (h, jnp.minimum(qi[t] - kj[t], 2), 0, 0))
    rowspec = pl.BlockSpec((1, LANES), lambda b, h, t, qi, kj: (0, 0))
    in_specs = [qspec, kspec, vspec, bspec]
    args = [p8, p8, p8, bias_tab]
    if has_mask:
        in_specs.append(pl.BlockSpec((T, T), lambda b, h, t, qi, kj: (b * nq + qi[t], kj[t])))
        args.append(mask)
    in_specs += [rowspec, rowspec]
    args += [lam_row, gain_row]
    grid_spec = pltpu.PrefetchScalarGridSpec(
        num_scalar_prefetch=2,
        grid=(B, H, len(pairs)),
        in_specs=in_specs,
        out_specs=pl.BlockSpec((T, LANES), lambda b, h, t, qi, kj: (b * nq + qi[t], h)),
        scratch_shapes=[pltpu.VMEM((groups, T, LANES), F32), pltpu.VMEM((groups, T, LANES), F32),
                        pltpu.VMEM((groups, T, LANES), F32)],
    )
    return pl.pallas_call(
        functools.partial(_flash_kernel, groups=groups, dk=dk, scale=scale, has_mask=has_mask, diff=diff, T=T),
        out_shape=jax.ShapeDtypeStruct((B * L, H * LANES), F32),
        grid_spec=grid_spec,
        compiler_params=_cparams(("parallel", "parallel", "arbitrary")),
    )(qi, kj, *args)


def _sortable(x):
    i = lax.bitcast_convert_type(x, jnp.int32)
    return jnp.where(i < 0, i ^ jnp.int32(0x7FFFFFFF), i)


def _topk_bias(key_ref, o_ref, topk, n_chunks, cw):
    R = key_ref.shape[0]
    kf = float(topk)
    nl = cw // LANES

    def chunk_at(c):
        return pl.ds(pl.multiple_of(c * cw, LANES), cw)

    def count(pred):
        def body(c, acc):
            hit = jnp.where(pred(key_ref[:, chunk_at(c)]), 1.0, 0.0)
            parts = [hit[:, t * LANES:(t + 1) * LANES] for t in range(nl)]
            while len(parts) > 1:
                parts = [a + b for a, b in zip(parts[::2], parts[1::2])] + parts[len(parts) & ~1:]
            return acc + parts[0]
        acc = lax.fori_loop(0, n_chunks, body, jnp.zeros((R, LANES), F32))
        return jnp.sum(acc, axis=1, keepdims=True)

    base = jnp.where(count(lambda k: k >= 0) >= kf, 0, INT_MIN).astype(jnp.int32)

    def bit_step(it, base):
        cand = base | jnp.left_shift(jnp.int32(1), 30 - it)
        return jnp.where(count(lambda k: k >= cand) >= kf, cand, base)

    base = lax.fori_loop(0, 31, bit_step, base)
    has_excess = jnp.max(count(lambda k: k >= base)) > kf

    @pl.when(jnp.logical_not(has_excess))
    def _():
        def write(c, carry):
            k = key_ref[:, chunk_at(c)]
            sel = (k >= base) & (k > KEY_NEG_INF)
            o_ref[:, chunk_at(c)] = jnp.where(sel, 0.0, NEG_BIAS).astype(o_ref.dtype)
            return carry
        lax.fori_loop(0, n_chunks, write, 0)

    @pl.when(has_excess)
    def _():
        need = kf - count(lambda k: k > base)
        before = (lax.broadcasted_iota(jnp.int32, (LANES, LANES), 0)
                  < lax.broadcasted_iota(jnp.int32, (LANES, LANES), 1))
        before = jnp.where(before, 1.0, 0.0)

        def lane_tile(c, seen):
            sl = pl.ds(pl.multiple_of(c * LANES, LANES), LANES)
            k = key_ref[:, sl]
            eq = k == base
            eqf = jnp.where(eq, 1.0, 0.0)
            rank = _dot(eqf, before) + seen
            sel = ((k > base) | (eq & (rank < need))) & (k > KEY_NEG_INF)
            o_ref[:, sl] = jnp.where(sel, 0.0, NEG_BIAS).astype(o_ref.dtype)
            return seen + jnp.sum(eqf, axis=1, keepdims=True)

        lax.fori_loop(0, n_chunks * nl, lane_tile, jnp.zeros((R, 1), F32))


def _topk_bias_t(key_ref, o_ref, topk, n_chunks, kc):
    Q = key_ref.shape[1]
    kf = float(topk)
    SUB = 8

    def rows_at(c):
        return pl.ds(pl.multiple_of(c * kc, kc), kc)

    def count(pred):
        def body(c, acc):
            hit = jnp.where(pred(key_ref[rows_at(c), :]), 1.0, 0.0).reshape(kc // SUB, SUB, Q)
            n = kc // SUB
            while n > 1:
                n //= 2
                hit = hit[:n] + hit[n:2 * n]
            return acc + hit[0]
        acc = lax.fori_loop(0, n_chunks, body, jnp.zeros((SUB, Q), F32))
        return jnp.sum(acc, axis=0, keepdims=True)

    base = jnp.where(count(lambda k: k >= 0) >= kf, 0, INT_MIN).astype(jnp.int32)

    def bit_step(it, base):
        cand = base | jnp.left_shift(jnp.int32(1), 30 - it)
        return jnp.where(count(lambda k: k >= cand) >= kf, cand, base)

    base = lax.fori_loop(0, 31, bit_step, base)
    has_excess = jnp.max(count(lambda k: k >= base)) > kf

    def store(c, sel):
        bias_t = jnp.where(sel, 0.0, NEG_BIAS).astype(o_ref.dtype)
        o_ref[:, rows_at(c)] = jnp.transpose(bias_t)

    @pl.when(jnp.logical_not(has_excess))
    def _():
        def write(c, carry):
            k = key_ref[rows_at(c), :]
            store(c, (k >= base) & (k > KEY_NEG_INF))
            return carry
        lax.fori_loop(0, n_chunks, write, 0)

    @pl.when(has_excess)
    def _():
        need = kf - count(lambda k: k > base)
        earlier = (lax.broadcasted_iota(jnp.int32, (kc, kc), 1)
                   < lax.broadcasted_iota(jnp.int32, (kc, kc), 0))
        earlier = jnp.where(earlier, 1.0, 0.0)

        def write(c, seen):
            k = key_ref[rows_at(c), :]
            eq = k == base
            eqf = jnp.where(eq, 1.0, 0.0)
            rank = _dot(earlier, eqf) + seen
            store(c, ((k > base) | (eq & (rank < need))) & (k > KEY_NEG_INF))
            return seen + jnp.sum(eqf, axis=0, keepdims=True)

        lax.fori_loop(0, n_chunks, write, jnp.zeros((1, Q), F32))


def _idx_prompt_kernel(iq_ref, ik_ref, iwt_ref, o_ref, key_sc, *, tq, kc, topk):
    i = pl.program_id(1)
    iq = iq_ref[...]
    lhs = jnp.concatenate([iq[:, h * D_IDX:(h + 1) * D_IDX] for h in range(H_IDX)], axis=0).astype(BF16)
    iwt = iwt_ref[...]
    qpos = i * tq + lax.broadcasted_iota(jnp.int32, (1, tq), 1)
    n_need = (i * tq + tq + kc - 1) // kc
    o_ref[...] = jnp.full(o_ref.shape, NEG_BIAS, o_ref.dtype)

    def score_chunk(c, carry):
        off = pl.multiple_of(c * kc, kc)
        ik = ik_ref[pl.ds(off, kc), 0:D_IDX].astype(BF16)
        s = _dot_nt(ik, lhs)
        score = jnp.zeros((kc, tq), F32)
        for h in range(H_IDX):
            score = score + jnp.maximum(s[:, h * tq:(h + 1) * tq], 0.0) * iwt[h:h + 1, :]
        kpos = off + lax.broadcasted_iota(jnp.int32, (kc, 1), 0)
        score = jnp.where(kpos <= qpos, score, -jnp.inf)
        key_sc[pl.ds(off, kc), :] = _sortable(score)
        return carry

    lax.fori_loop(0, n_need, score_chunk, 0)
    _topk_bias_t(key_sc, o_ref, topk, n_need, kc)


def _idx_prompt(p8, tail, *, B, L, tq, topk):
    nq = L // tq
    nmisc = TAIL_MISC // LANES
    return pl.pallas_call(
        functools.partial(_idx_prompt_kernel, tq=tq, kc=min(L, 512), topk=topk),
        out_shape=jax.ShapeDtypeStruct((B * L, L), BF16),
        grid=(B, nq),
        in_specs=[pl.BlockSpec((None, tq, W_GROUP), lambda b, i: (7, b * nq + i, 0)),
                  pl.BlockSpec((L, LANES), lambda b, i: (b, nmisc)),
                  pl.BlockSpec((H_IDX, tq), lambda b, i: (0, b * nq + i))],
        out_specs=pl.BlockSpec((tq, L), lambda b, i: (b * nq + i, 0)),
        scratch_shapes=[pltpu.VMEM((L, tq), jnp.int32)],
        compiler_params=_cparams(("parallel", "arbitrary")),
    )(p8, tail, jnp.transpose(tail[:, TAIL_MISC + D_IDX:TAIL_MISC + D_IDX + H_IDX]))


def _gelu_tanh(x):
    return 0.5 * x * (1.0 + jnp.tanh(math.sqrt(2.0 / math.pi) * (x + 0.044715 * (x * x * x))))


def _s5_kernel(u_ref, h0r_ref, h0i_ref, bre_ref, bim_ref, apr_ref, api_ref, pwr_ref, pwi_ref,
               cre_ref, cim_ref, d_ref, wglu_ref, y_ref, hr_ref, hi_ref, cr_sc, ci_sc, hsr_sc, hsi_sc, *, T):
    t = pl.program_id(1)

    @pl.when(t == 0)
    def _():
        cr_sc[...] = h0r_ref[...]
        ci_sc[...] = h0i_ref[...]

    u = u_ref[...]
    ub = u.astype(BF16)
    G = T // S5_GROUP
    hr = _dot(ub, bre_ref[...]).reshape(G, S5_GROUP, NSTATE)
    hi = _dot(ub, bim_ref[...]).reshape(G, S5_GROUP, NSTATE)
    for j in range(S5_GROUP.bit_length() - 1):
        sr = pltpu.roll(hr, 1 << j, 1)
        si = pltpu.roll(hi, 1 << j, 1)
        ar = apr_ref[j]
        ai = api_ref[j]
        hr, hi = hr + (ar * sr - ai * si), hi + (ar * si + ai * sr)
    cr = cr_sc[...]
    ci = ci_sc[...]
    pr = pwr_ref[...]
    pi_ = pwi_ref[...]
    for g in range(G):
        rows = slice(g * S5_GROUP, (g + 1) * S5_GROUP)
        gr = hr[g] + (pr * cr - pi_ * ci)
        gi = hi[g] + (pr * ci + pi_ * cr)
        hsr_sc[rows, :] = gr
        hsi_sc[rows, :] = gi
        cr = gr[S5_GROUP - 1:S5_GROUP, :]
        ci = gi[S5_GROUP - 1:S5_GROUP, :]
    cr_sc[...] = cr
    ci_sc[...] = ci
    hr_ref[...] = cr
    hi_ref[...] = ci
    y = (_dot(hsr_sc[...].astype(BF16), cre_ref[...]) - _dot(hsi_sc[...].astype(BF16), cim_ref[...])
         + d_ref[...] * u)
    y = _gelu_tanh(y)
    gu = _dot(y.astype(BF16), wglu_ref[...])
    y_ref[...] = gu[:, :W_GROUP] * jax.nn.sigmoid(gu[:, W_GROUP:])


def _s5(p8, h0r, h0i, tabs, l, *, B, L, T):
    nt = L // T
    bre, bim, apr, api, pwr, pwi, cre, cim, dsk, wglu = tabs
    lay = lambda *shape: pl.BlockSpec((None,) + shape, lambda b, t: (l,) + (0,) * len(shape))
    st = pl.BlockSpec((None, 1, NSTATE), lambda b, t: (b, 0, 0))
    return pl.pallas_call(
        functools.partial(_s5_kernel, T=T),
        out_shape=(jax.ShapeDtypeStruct((B * L, W_GROUP), F32),
                   jax.ShapeDtypeStruct((B, 1, NSTATE), F32), jax.ShapeDtypeStruct((B, 1, NSTATE), F32)),
        grid=(B, nt),
        in_specs=[pl.BlockSpec((None, T, W_GROUP), lambda b, t: (3, b * nt + t, 0)), st, st,
                  lay(W_GROUP, NSTATE), lay(W_GROUP, NSTATE), lay(3, S5_GROUP, NSTATE), lay(3, S5_GROUP, NSTATE),
                  lay(S5_GROUP, NSTATE), lay(S5_GROUP, NSTATE), lay(NSTATE, W_GROUP), lay(NSTATE, W_GROUP),
                  lay(1, W_GROUP), lay(W_GROUP, 2 * W_GROUP)],
        out_specs=(pl.BlockSpec((T, W_GROUP), lambda b, t: (b * nt + t, 0)), st, st),
        scratch_shapes=[pltpu.VMEM((1, NSTATE), F32), pltpu.VMEM((1, NSTATE), F32),
                        pltpu.VMEM((T, NSTATE), F32), pltpu.VMEM((T, NSTATE), F32)],
        compiler_params=_cparams(("parallel", "arbitrary")),
    )(p8, h0r, h0i, bre, bim, apr, api, pwr, pwi, cre, cim, dsk, wglu)


def _log_sigmoid(x):
    return jnp.minimum(x, 0.0) - jnp.log1p(jnp.exp(-jnp.abs(x)))


def _gla_kernel(q_ref, k_ref, v_ref, r_ref, misc_ref, wg_ref, bg_ref, ng_ref, s0_ref,
                y_ref, sfin_ref, s_sc, *, TB, CH, nt, valid_len):
    t = pl.program_id(1)
    HK = H_D * DK_D
    HV = H_D * DV_D

    @pl.when(t == 0)
    def _():
        s_sc[...] = jnp.zeros_like(s_sc)
        for h in range(H_D):
            s_sc[h * DK_D:(h + 1) * DK_D, h * DV_D:(h + 1) * DV_D] = s0_ref[h]

    row = lax.broadcasted_iota(jnp.int32, (TB, 1), 0)
    glow = misc_ref[:, D_IDX + H_IDX:D_IDX + H_IDX + GATE_RANK]
    logit = _dot(glow.astype(BF16), wg_ref[...].astype(BF16)) + bg_ref[...]
    la = _log_sigmoid(logit) * (1.0 / GATE_TAU)
    k = k_ref[...]
    if valid_len < TB:
        la = jnp.where(row < valid_len, la, 0.0)
        k = jnp.where(row < valid_len, k, 0.0)
    ri = lax.broadcasted_iota(jnp.int32, (TB, TB), 0)
    ci = lax.broadcasted_iota(jnp.int32, (TB, TB), 1)
    same = (ri // CH) == (ci // CH)
    lower = same & (ci <= ri)
    tri = jnp.where(lower, 1.0, 0.0).astype(BF16)
    blk = jnp.where(same, 1.0, 0.0).astype(BF16)
    la_hi = la.astype(BF16)
    la_lo = (la - la_hi.astype(F32)).astype(BF16)
    bc = _dot(tri, la_hi) + _dot(tri, la_lo)
    bl = _dot(blk, la_hi) + _dot(blk, la_lo)
    q = q_ref[...] * (DK_D ** -0.5)
    v = v_ref[...]
    vb = v.astype(BF16)
    qt = (q * jnp.exp(bc)).astype(BF16)
    kc = (k * jnp.exp(-bc)).astype(BF16)
    kh_t = jnp.transpose(k * jnp.exp(bl - bc))
    dec_t = jnp.transpose(jnp.exp(bl))

    o_parts = []
    for h in range(H_D):
        att = _dot_nt(qt[:, h * DK_D:(h + 1) * DK_D], kc[:, h * DK_D:(h + 1) * DK_D])
        att = jnp.where(lower, att, 0.0)
        o_parts.append(_dot(att.astype(BF16), vb[:, h * DV_D:(h + 1) * DV_D]))
    o_intra = jnp.concatenate(o_parts, axis=1)

    head_blk = (lax.broadcasted_iota(jnp.int32, (HK, HV), 0) // DK_D
                == lax.broadcasted_iota(jnp.int32, (HK, HV), 1) // DV_D)
    col = lax.broadcasted_iota(jnp.int32, (1, TB), 1)
    S = s_sc[...]
    o_rows = []
    for c in range(TB // CH):
        o_rows.append(_dot(qt[c * CH:(c + 1) * CH, :], S.astype(BF16)))
        in_chunk = (col >= c * CH) & (col < (c + 1) * CH)
        kv = _dot(jnp.where(in_chunk, kh_t, 0.0).astype(BF16), vb)
        S = dec_t[:, c * CH:c * CH + 1] * S + jnp.where(head_blk, kv, 0.0)
    s_sc[...] = S
    o = o_intra + jnp.concatenate(o_rows, axis=0)

    outs = []
    for h in range(H_D):
        oh = o[:, h * DV_D:(h + 1) * DV_D]
        ms = jnp.mean(oh * oh, axis=1, keepdims=True)
        outs.append(oh * lax.rsqrt(ms + LN_EPS) * ng_ref[...])
    r = r_ref[...]
    y_ref[...] = jnp.concatenate(outs, axis=1) * (r * jax.nn.sigmoid(r))

    @pl.when(t == nt - 1)
    def _():
        for h in range(H_D):
            sfin_ref[h] = S[h * DK_D:(h + 1) * DK_D, h * DV_D:(h + 1) * DV_D]


def _gla(tail, s0, w_gate, b_gate, norm_g, l, *, B, L, TB, valid_len):
    nt = L // TB
    lay = lambda *shape: pl.BlockSpec((None,) + shape, lambda b, t: (l,) + (0,) * len(shape))
    sspec = pl.BlockSpec((None, H_D, DK_D, DV_D), lambda b, t: (b, 0, 0, 0))
    return pl.pallas_call(
        functools.partial(_gla_kernel, TB=TB, CH=GLA_CHUNK, nt=nt, valid_len=valid_len),
        out_shape=(jax.ShapeDtypeStruct((B * L, W_GROUP), F32),
                   jax.ShapeDtypeStruct((B, H_D, DK_D, DV_D), F32)),
        grid=(B, nt),
        in_specs=[pl.BlockSpec((TB, 256), lambda b, t: (b * nt + t, 0)),
                  pl.BlockSpec((TB, 256), lambda b, t: (b * nt + t, 1)),
                  pl.BlockSpec((TB, 512), lambda b, t: (b * nt + t, 1)),
                  pl.BlockSpec((TB, 512), lambda b, t: (b * nt + t, 2)),
                  pl.BlockSpec((TB, LANES), lambda b, t: (b * nt + t, TAIL_MISC // LANES)),
                  lay(GATE_RANK, H_D * DK_D), lay(1, H_D * DK_D), lay(1, DV_D), sspec],
        out_specs=(pl.BlockSpec((TB, W_GROUP), lambda b, t: (b * nt + t, 0)), sspec),
        scratch_shapes=[pltpu.VMEM((H_D * DK_D, H_D * DV_D), F32)],
        compiler_params=_cparams(("parallel", "arbitrary")),
    )(tail, tail, tail, tail, tail, w_gate, b_gate, norm_g, s0)


def _sidx_kernel(pt_ref, iq_ref, iw_ref, *rest, PPS, NS, n_new, topk):
    page_refs = rest[:PPS]
    new_ref, o_ref, key_sc = rest[PPS:]
    s_id = pl.program_id(1)
    iq = iq_ref[...].astype(BF16)
    iw = iw_ref[...]

    def scores(keys_t):
        hs = jnp.maximum(_dot(iq, keys_t.astype(BF16)), 0.0) * iw
        sc = jnp.zeros((n_new, LANES), F32)
        for h in range(H_IDX):
            sc = sc + hs[h * n_new:(h + 1) * n_new]
        return sc

    for p in range(PPS):
        off = pl.multiple_of((s_id * PPS + p) * LANES, LANES)
        key_sc[:, pl.ds(off, LANES)] = _sortable(scores(page_refs[p][...]))

    @pl.when(s_id == NS - 1)
    def _():
        sc = scores(new_ref[...])
        qi = lax.broadcasted_iota(jnp.int32, (n_new, LANES), 0)
        kj = lax.broadcasted_iota(jnp.int32, (n_new, LANES), 1)
        sc = jnp.where(kj <= qi, sc, -jnp.inf)
        key_sc[:, NS * PPS * LANES:(NS * PPS + 1) * LANES] = _sortable(sc)
        n_tiles = NS * PPS + 1
        n_chunks = 3 if n_tiles % 3 == 0 else 1
        _topk_bias(key_sc, o_ref, topk, n_chunks, n_tiles // n_chunks * LANES)


def _sidx(page_table, iq_rows, iw_tile, cache_idx, idx_new, l, *, PPS, topk):
    Bs, n_pages = page_table.shape
    NS = n_pages // PPS
    n_new = iq_rows.shape[1] // H_IDX
    W = (n_pages + 1) * LANES

    def page_spec(p):
        return pl.BlockSpec((None, None, D_IDX, PAGE_SIZE),
                            lambda b, s, pt: (l, pt[b, s * PPS + p], 0, 0))

    grid_spec = pltpu.PrefetchScalarGridSpec(
        num_scalar_prefetch=1,
        grid=(Bs, NS),
        in_specs=[pl.BlockSpec((None, H_IDX * n_new, D_IDX), lambda b, s, pt: (b, 0, 0)),
                  pl.BlockSpec((None, H_IDX * n_new, LANES), lambda b, s, pt: (b, 0, 0))]
                 + [page_spec(p) for p in range(PPS)]
                 + [pl.BlockSpec((None, D_IDX, PAGE_SIZE), lambda b, s, pt: (b, 0, 0))],
        out_specs=pl.BlockSpec((None, n_new, W), lambda b, s, pt: (b, 0, 0)),
        scratch_shapes=[pltpu.VMEM((n_new, W), jnp.int32)],
    )
    return pl.pallas_call(
        functools.partial(_sidx_kernel, PPS=PPS, NS=NS, n_new=n_new, topk=topk),
        out_shape=jax.ShapeDtypeStruct((Bs, n_new, W), F32),
        grid_spec=grid_spec,
        compiler_params=_cparams(("parallel", "arbitrary")),
    )(page_table, iq_rows, iw_tile, *([cache_idx] * PPS), idx_new)


def _sattn_kernel(pt_ref, q_ref, *rest, PPS, NS, R, n_new, has_mask, diff):
    k_refs = rest[:PPS]
    v_refs = rest[PPS:2 * PPS]
    rest = rest[2 * PPS:]
    knew_ref, vnew_ref, far_ref, near_ref = rest[:4]
    rest = rest[4:]
    if has_mask:
        mask_ref, masknew_ref, spread_ref = rest[:3]
        rest = rest[3:]
    lam_ref, gain_ref, o_ref, m_sc, l_sc, acc_sc = rest
    s_id = pl.program_id(1)

    @pl.when(s_id == 0)
    def _():
        m_sc[...] = jnp.full(m_sc.shape, -jnp.inf, F32)
        l_sc[...] = jnp.zeros_like(l_sc)
        acc_sc[...] = jnp.zeros_like(acc_sc)

    q = q_ref[...].astype(BF16)
    PR = PAGE_SIZE * 4

    def expand(m):
        return jnp.concatenate([_dot(m, spread_ref[...])] * (R // n_new), axis=0)

    def pages(kps, vps, biases):
        ss = [_dot_nt(q, kp.astype(BF16)) + b for kp, b in zip(kps, biases)]
        top = ss[0]
        for s in ss[1:]:
            top = jnp.maximum(top, s)
        m_prev = m_sc[...]
        m_new = jnp.maximum(m_prev, jnp.max(top, axis=1, keepdims=True))
        a = jnp.exp2(m_prev - m_new)
        ps = [jnp.exp2(s - m_new) for s in ss]
        tot = ps[0]
        for p in ps[1:]:
            tot = tot + p
        pv = _dot(ps[0].astype(BF16), vps[0].astype(BF16))
        for p, vp in zip(ps[1:], vps[1:]):
            pv = pv + _dot(p.astype(BF16), vp.astype(BF16))
        l_sc[...] = a * l_sc[...] + jnp.sum(tot, axis=1, keepdims=True)
        acc_sc[...] = a * acc_sc[...] + pv
        m_sc[...] = m_new

    far = far_ref[...]
    biases = []
    for p in range(PPS):
        bias = far
        if p == PPS - 1:
            bias = jnp.where(s_id == NS - 1, near_ref[:, 0:PR], far)
        if has_mask:
            bias = bias + expand(mask_ref[:, p * LANES:(p + 1) * LANES])
        biases.append(bias)
    pages([r[...] for r in k_refs], [r[...] for r in v_refs], biases)

    @pl.when(s_id == NS - 1)
    def _():
        bias = near_ref[:, PR:2 * PR]
        if has_mask:
            bias = bias + expand(masknew_ref[...])
        pages([knew_ref[...]], [vnew_ref[...]], [bias])
        o_all = acc_sc[...] / l_sc[...]
        outs = []
        for h in range(4):
            if diff:
                o1 = o_all[(2 * h) * n_new:(2 * h + 1) * n_new]
                o2 = o_all[(2 * h + 1) * n_new:(2 * h + 2) * n_new]
                o = o1 - lam_ref[...] * o2
                ms = jnp.mean(o * o, axis=1, keepdims=True)
                outs.append(o * lax.rsqrt(ms + LN_EPS) * gain_ref[...])
            else:
                outs.append(o_all[h * n_new:(h + 1) * n_new])
        o_ref[...] = jnp.concatenate(outs, axis=1)


def _sattn(page_table, q_rows, cache_k, cache_v, knew, vnew, far, near, mask, lam_row, gain_row, l,
           *, PPS, diff):
    Bs, n_pages = page_table.shape
    NS = n_pages // PPS
    R = q_rows.shape[1]
    n_new = 8
    PR = PAGE_SIZE * 4
    has_mask = mask is not None

    def page_spec(p):
        return pl.BlockSpec((None, None, PR, LANES), lambda b, s, pt: (l, pt[b, s * PPS + p], 0, 0))

    per_b = lambda *shape: pl.BlockSpec((None,) + shape, lambda b, s, pt: (b,) + (0,) * len(shape))
    const = lambda *shape: pl.BlockSpec(shape, lambda b, s, pt: (0,) * len(shape))
    in_specs = ([per_b(R, LANES)] + [page_spec(p) for p in range(PPS)] * 2
                + [per_b(PR, LANES), per_b(PR, LANES), const(R, PR), const(R, 2 * PR)])
    args = [q_rows] + [cache_k] * PPS + [cache_v] * PPS + [knew, vnew, far, near]
    if has_mask:
        spread = (jnp.arange(PR, dtype=jnp.int32)[None, :] // 4
                  == jnp.arange(PAGE_SIZE, dtype=jnp.int32)[:, None]).astype(F32)
        in_specs += [pl.BlockSpec((None, n_new, PPS * LANES), lambda b, s, pt: (b, 0, s)),
                     pl.BlockSpec((None, n_new, LANES), lambda b, s, pt: (b, 0, n_pages)),
                     const(PAGE_SIZE, PR)]
        args += [mask, mask, spread]
    in_specs += [const(1, LANES), const(1, LANES)]
    args += [lam_row, gain_row]
    grid_spec = pltpu.PrefetchScalarGridSpec(
        num_scalar_prefetch=1,
        grid=(Bs, NS),
        in_specs=in_specs,
        out_specs=pl.BlockSpec((None, n_new, W_GROUP), lambda b, s, pt: (b, 0, 0)),
        scratch_shapes=[pltpu.VMEM((R, 1), F32), pltpu.VMEM((R, 1), F32), pltpu.VMEM((R, LANES), F32)],
    )
    return pl.pallas_call(
        functools.partial(_sattn_kernel, PPS=PPS, NS=NS, R=R, n_new=n_new,
                          has_mask=has_mask, diff=diff),
        out_shape=jax.ShapeDtypeStruct((Bs, n_new, W_GROUP), F32),
        grid_spec=grid_spec,
        compiler_params=_cparams(("parallel", "arbitrary")),
    )(page_table, *args)


def _rel_bucket(dist):
    n = jnp.maximum(dist, 0)
    nf = jnp.maximum(n, REL_MAX_EXACT).astype(F32)
    large = REL_MAX_EXACT + (jnp.log(nf / REL_MAX_EXACT) / math.log(REL_MAX_DIST / REL_MAX_EXACT)
                             * (REL_BUCKETS - REL_MAX_EXACT)).astype(jnp.int32)
    large = jnp.minimum(large, REL_BUCKETS - 1)
    return jnp.where(n < REL_MAX_EXACT, n, large)


def _rel_lookup(rel, bucket):
    out = jnp.zeros((rel.shape[1],) + bucket.shape, F32)
    for b in range(REL_BUCKETS):
        out = jnp.where(bucket[None] == b, rel[b].reshape((-1,) + (1,) * bucket.ndim), out)
    return out * LOG2E


def _prompt_bias_tables(rel, T):
    assert T >= REL_MAX_DIST
    r = jnp.arange(T, dtype=jnp.int32)[:, None]
    c = jnp.arange(T, dtype=jnp.int32)[None, :]
    d0 = r - c
    t0 = jnp.where((d0 >= 0)[None], _rel_lookup(rel, _rel_bucket(d0)), NEG_BIAS)
    t1 = _rel_lookup(rel, _rel_bucket(T + r - c))
    t2 = _rel_lookup(rel, _rel_bucket(jnp.full((T, T), 2 * T, jnp.int32)))
    return jnp.stack([t0, t1, t2], axis=1)


def _sample_bias_tables(rel, n_new, reps):
    H = rel.shape[1]
    t = jnp.arange(n_new, dtype=jnp.int32)[:, None]
    c = jnp.arange(PAGE_SIZE, dtype=jnp.int32)[None, :]
    last = _rel_lookup(rel, _rel_bucket(PAGE_SIZE + t - c))
    dn = t - c
    new = jnp.where(((dn >= 0) & (c < n_new))[None], _rel_lookup(rel, _rel_bucket(dn)), NEG_BIAS)
    far = _rel_lookup(rel, _rel_bucket(jnp.full((n_new, PAGE_SIZE), 2 * PAGE_SIZE, jnp.int32)))

    def rows(x):
        own = jnp.eye(H, dtype=bool)[:, None, None, :]
        y = jnp.where(own, x[..., None], NEG_BIAS).reshape(H, 1, n_new, PAGE_SIZE * H)
        return jnp.broadcast_to(y, (H, reps, n_new, PAGE_SIZE * H)).reshape(H * reps * n_new, PAGE_SIZE * H)

    return rows(far), jnp.concatenate([rows(last), rows(new)], axis=1)


def _s5_tables(a_re, a_im, log_dt, b_re, b_im, c_re, c_im, d_skip, w_glu):
    dt = jnp.exp(log_dt.astype(F32))[..., None]
    lam_re, lam_im = a_re.astype(F32), a_im.astype(F32)
    z_re, z_im = lam_re * dt, lam_im * dt
    mag = jnp.exp(z_re)
    e_re, e_im = mag * jnp.cos(z_im), mag * jnp.sin(z_im)
    den = lam_re * lam_re + lam_im * lam_im
    f_re = ((e_re - 1.0) * lam_re + e_im * lam_im) / den
    f_im = (e_im * lam_re - (e_re - 1.0) * lam_im) / den
    br, bi = b_re.astype(F32), b_im.astype(F32)
    bb_re = f_re[..., None] * br - f_im[..., None] * bi
    bb_im = f_re[..., None] * bi + f_im[..., None] * br
    eye = jnp.eye(G_B, dtype=F32)
    bd_in = lambda bb: jnp.einsum('lgpc,gh->lgchp', bb, eye).reshape(DEPTH, W_GROUP, NSTATE).astype(BF16)
    bd_out = lambda cc: jnp.einsum('lgcp,gh->lgphc', cc.astype(F32), eye).reshape(DEPTH, NSTATE, W_GROUP).astype(BF16)

    def powers(ts):
        tt = ts[None, :, None, None]
        pm = jnp.exp(z_re[:, None] * tt)
        return ((pm * jnp.cos(z_im[:, None] * tt)).reshape(DEPTH, -1, NSTATE),
                (pm * jnp.sin(z_im[:, None] * tt)).reshape(DEPTH, -1, NSTATE))

    n_steps = S5_GROUP.bit_length() - 1
    apr, api = powers(jnp.asarray([2.0 ** j for j in range(n_steps)], F32))
    live = (jnp.arange(S5_GROUP)[None, :] >= (2 ** jnp.arange(n_steps))[:, None]).astype(F32)
    apr = apr[:, :, None, :] * live[None, :, :, None]
    api = api[:, :, None, :] * live[None, :, :, None]
    pwr, pwi = powers(jnp.arange(1, S5_GROUP + 1, dtype=F32))
    return (bd_in(bb_re), bd_in(bb_im), apr, api, pwr, pwi, bd_out(c_re), bd_out(c_im),
            d_skip.astype(F32).reshape(DEPTH, 1, W_GROUP), w_glu.astype(BF16))


def _tail_weights(w_in_t):
    pad = jnp.zeros((w_in_t.shape[0], N_TAIL - 1624, w_in_t.shape[2]), w_in_t.dtype)
    return jnp.concatenate([w_in_t[:, 4168:5704], w_in_t[:, 4096:4168], w_in_t[:, 5704:5720], pad],
                           axis=1).astype(BF16)


def _channel_mix(ys, x, l, W, tm, tm_ffn):
    x1, x1b = _mix_ln(ys, W['w_out'], l, x, W['ln1_g'], W['ln1_b'], tm)
    hid = _swiglu(x1b, W['ffn_w_gate'], W['ffn_w_up'], l, tm_ffn, 512)
    return _mm_ln(hid, W['ffn_w_down'], l, x1, W['ln2_g'], W['ln2_b'], tm, D_FF // 4)


def _prompt_layer(x, xb, l, W, B, L):
    p8, a_k, a_v, c_k, c_v = _proj_main(xb, W['w_in_t'], l, PROJ_ROWS)
    tail = _proj_tail(xb, W['w_tail_t'], l, ROW_TILE)
    y_a = _flash_prompt(p8, (0, 1, 2), W['tab_a'], None, W['lam_row'][l], W['gain_a'][l],
                        B=B, L=L, T=ATT_TILE, groups=2, dk=DH_A, scale=DH_A ** -0.5, diff=True)
    zeros_h = jnp.zeros((B, 1, NSTATE), F32)
    y_b, h_re, h_im = _s5(p8, zeros_h, zeros_h, W['s5'], l, B=B, L=L, T=S5_BLOCK)
    mask = _idx_prompt(p8, tail, B=B, L=L, tq=IDX_QUERIES, topk=min(TOPK_MAX, L // 4))
    y_c = _flash_prompt(p8, (4, 5, 6), W['tab_c'], mask, W['lam_row'][l], W['gain_a'][l],
                        B=B, L=L, T=ATT_TILE, groups=1, dk=DH_C, scale=DH_C ** -0.5, diff=False)
    y_d, s_fin = _gla(tail, jnp.zeros((B, H_D, DK_D, DV_D), F32), W['gla_w_gate'], W['gla_b_gate'],
                      W['gla_norm'], l, B=B, L=L, TB=GLA_BLOCK, valid_len=GLA_BLOCK)
    x, xb = _channel_mix([y_a, y_b, y_c, y_d], x, l, W, ROW_TILE, SWIGLU_ROWS)
    news = (a_k.reshape(B, L, H_A, 128), a_v.reshape(B, L, H_A, 128),
            c_k.reshape(B, L, H_C, 128), c_v.reshape(B, L, H_C, 128),
            tail[:, TAIL_MISC:TAIL_MISC + D_IDX].reshape(B, L, D_IDX),
            h_re.reshape(B, G_B, P_B), h_im.reshape(B, G_B, P_B), s_fin)
    return x, xb, news


def _sample_layer(x, xb, l, W, C, Bs, Ls):
    M = Bs * Ls
    PPS = SAMPLE_PAGES
    p8, a_k, a_v, c_k, c_v = _proj_main(xb, W['w_in_t'], l, M)
    tail = _proj_tail(xb, W['w_tail_t'], l, M)
    pt = C['page_table']

    def pad_new(z):
        z = z.reshape(Bs, Ls, z.shape[-1])
        return jnp.pad(z, ((0, 0), (0, PAGE_SIZE - Ls), (0, 0)))

    def pad_rows(z):
        z = z.reshape(Bs, Ls * 4, LANES)
        return jnp.pad(z, ((0, 0), (0, PAGE_SIZE * 4 - Ls * 4), (0, 0)))

    qa = p8[0].reshape(Bs, Ls, H_A, 2, DH_A) * (DH_A ** -0.5 * LOG2E)
    q_a = jnp.einsum('bqhwd,wv->bhwqvd', qa, jnp.eye(2, dtype=F32)).reshape(Bs, H_A * 2 * Ls, LANES)
    y_a = _sattn(pt, q_a, C['a_k'], C['a_v'], pad_rows(a_k), pad_rows(a_v), W['far_a'], W['near_a'],
                 None, W['lam_row'][l], W['gain_a'][l], l, PPS=PPS, diff=True)
    y_b, h_re, h_im = _s5(p8, C['b_re'][l].reshape(Bs, 1, NSTATE), C['b_im'][l].reshape(Bs, 1, NSTATE),
                          W['s5'], l, B=Bs, L=Ls, T=Ls)
    misc = tail[:, TAIL_MISC:]
    iq_rows = p8[7].reshape(Bs, Ls, H_IDX, D_IDX).transpose(0, 2, 1, 3).reshape(Bs, H_IDX * Ls, D_IDX)
    iw = misc[:, D_IDX:D_IDX + H_IDX].reshape(Bs, Ls, H_IDX).transpose(0, 2, 1).reshape(Bs, H_IDX * Ls, 1)
    iw_tile = jnp.broadcast_to(iw, (Bs, H_IDX * Ls, LANES))
    mask = _sidx(pt, iq_rows, iw_tile, C['c_idx_t'], pad_new(misc[:, :D_IDX]).transpose(0, 2, 1), l, PPS=PPS,
                 topk=min(TOPK_MAX, (pt.shape[1] * PAGE_SIZE + Ls) // 4))
    q_c = (p8[4].reshape(Bs, Ls, H_C, DH_C) * (DH_C ** -0.5 * LOG2E)).transpose(0, 2, 1, 3)
    y_c = _sattn(pt, q_c.reshape(Bs, H_C * Ls, LANES), C['c_k'], C['c_v'], pad_rows(c_k), pad_rows(c_v),
                 W['far_c'], W['near_c'], mask, W['lam_row'][l], W['gain_a'][l], l, PPS=PPS, diff=False)
    tail_pad = pad_new(tail).reshape(Bs * PAGE_SIZE, N_TAIL)
    y_d, s_fin = _gla(tail_pad, C['d'][l], W['gla_w_gate'], W['gla_b_gate'], W['gla_norm'], l,
                      B=Bs, L=PAGE_SIZE, TB=PAGE_SIZE, valid_len=Ls)
    y_d = y_d.reshape(Bs, PAGE_SIZE, W_GROUP)[:, :Ls].reshape(M, W_GROUP)
    x, xb = _channel_mix([y_a.reshape(M, W_GROUP), y_b, y_c.reshape(M, W_GROUP), y_d], x, l, W, M, M)
    news = (a_k.reshape(Bs, Ls, H_A, 128), a_v.reshape(Bs, Ls, H_A, 128),
            c_k.reshape(Bs, Ls, H_C, 128), c_v.reshape(Bs, Ls, H_C, 128),
            misc[:, :D_IDX].reshape(Bs, Ls, D_IDX),
            h_re.reshape(Bs, G_B, P_B), h_im.reshape(Bs, G_B, P_B), s_fin)
    return x, xb, news


def kernel(x_prompt, x_sample, cache_a_k, cache_a_v, cache_c_k, cache_c_v, cache_c_idx, state_b_re, state_b_im, state_d, page_table, rel_bias, w_in, w_out, lam_q1, lam_k1, lam_q2, lam_k2, a_subln, s5_a_re, s5_a_im, s5_log_dt, s5_b_re, s5_b_im, s5_c_re, s5_c_im, s5_d, s5_w_glu, gla_w_gate, gla_b_gate, gla_norm, ln1_g, ln1_b, ffn_w_gate, ffn_w_up, ffn_w_down, ln2_g, ln2_b):
    B, L, _ = x_prompt.shape
    Bs, Ls, _ = x_sample.shape
    n_pool = cache_a_k.shape[1]

    lam_init = jnp.asarray([0.8 - 0.6 * math.exp(-0.3 * l) for l in range(DEPTH)], F32)
    lam = (jnp.exp(jnp.sum(lam_q1.astype(F32) * lam_k1.astype(F32), axis=-1))
           - jnp.exp(jnp.sum(lam_q2.astype(F32) * lam_k2.astype(F32), axis=-1)) + lam_init)
    s5_args = (s5_a_re, s5_a_im, s5_log_dt, s5_b_re, s5_b_im, s5_c_re, s5_c_im, s5_d, s5_w_glu)
    rel = rel_bias.astype(F32)
    w_in_t = jnp.swapaxes(w_in, 1, 2).astype(BF16)
    far_a, near_a = _sample_bias_tables(rel[:, :H_A], Ls, 2)
    far_c, near_c = _sample_bias_tables(rel[:, H_A:], Ls, 1)
    W = {
        'w_in_t': w_in_t, 'w_tail_t': _tail_weights(w_in_t), 'w_out': w_out.astype(BF16),
        'ffn_w_gate': ffn_w_gate, 'ffn_w_up': ffn_w_up, 'ffn_w_down': ffn_w_down.astype(BF16),
        'ln1_g': ln1_g.reshape(DEPTH, 1, D_MODEL), 'ln1_b': ln1_b.reshape(DEPTH, 1, D_MODEL),
        'ln2_g': ln2_g.reshape(DEPTH, 1, D_MODEL), 'ln2_b': ln2_b.reshape(DEPTH, 1, D_MODEL),
        'lam_row': jnp.broadcast_to(lam[:, None, None], (DEPTH, 1, LANES)),
        'gain_a': (a_subln.astype(F32) * (1.0 - lam_init)[:, None]).reshape(DEPTH, 1, LANES),
        'tab_a': _prompt_bias_tables(rel[:, :H_A], ATT_TILE),
        'tab_c': _prompt_bias_tables(rel[:, H_A:], ATT_TILE),
        'far_a': far_a, 'near_a': near_a, 'far_c': far_c, 'near_c': near_c,
        's5': _s5_tables(*s5_args),
        'gla_w_gate': gla_w_gate, 'gla_b_gate': gla_b_gate.reshape(DEPTH, 1, H_D * DK_D),
        'gla_norm': gla_norm.reshape(DEPTH, 1, DV_D),
    }
    C = {
        'page_table': page_table,
        'a_k': cache_a_k.reshape(DEPTH, n_pool, PAGE_SIZE * H_A, LANES),
        'a_v': cache_a_v.reshape(DEPTH, n_pool, PAGE_SIZE * H_A, LANES),
        'c_k': cache_c_k.reshape(DEPTH, n_pool, PAGE_SIZE * H_C, LANES),
        'c_v': cache_c_v.reshape(DEPTH, n_pool, PAGE_SIZE * H_C, LANES),
        'c_idx_t': jnp.swapaxes(cache_c_idx, 2, 3),
        'b_re': state_b_re, 'b_im': state_b_im, 'd': state_d,
    }

    xp = x_prompt.reshape(B * L, D_MODEL)
    xs = x_sample.reshape(Bs * Ls, D_MODEL)
    xpb, xsb = xp.astype(BF16), xs.astype(BF16)
    news_p, news_s = [], []
    for l in range(DEPTH):
        xp, xpb, n_p = _prompt_layer(xp, xpb, l, W, B, L)
        xs, xsb, n_s = _sample_layer(xs, xsb, l, W, C, Bs, Ls)
        news_p.append(n_p)
        news_s.append(n_s)
    stack = lambda news: [jnp.stack(z, axis=0) for z in zip(*news)]
    return (xp.reshape(B, L, D_MODEL), xs.reshape(Bs, Ls, D_MODEL), *stack(news_p), *stack(news_s))
```
